```python
import math
import jax, jax.numpy as jnp
from jax import lax
import numpy as np

D_MODEL = 2048
BATCH = 4
SEQ = 2048
DEPTH = 2

D_FF = 256 * ((8 * D_MODEL // 3 + 255) // 256)
HEAD_DIM = 128
A_HEADS = D_MODEL // (2 * HEAD_DIM)
A_WIDTH = A_HEADS * HEAD_DIM
MOBA_BLOCK = 256
MOBA_TOPK = 3
MOBA_QUERY_BLOCK = 64
ROPE_THETA = 10000.0
G_GROUPS = A_HEADS
G_DIM = HEAD_DIM
G_WIDTH = G_GROUPS * G_DIM
GMLP_CHUNK = 128
AB_IN = 3 * A_WIDTH + 2 * G_WIDTH
AB_MIX = A_WIDTH + G_WIDTH
DN_QK_HEADS = D_MODEL // 128
DN_V_HEADS = 2 * DN_QK_HEADS
DN_HEAD_DIM = 128
DN_QK_WIDTH = DN_QK_HEADS * DN_HEAD_DIM
DN_V_WIDTH = DN_V_HEADS * DN_HEAD_DIM
DN_CONV_DIM = 2 * DN_QK_WIDTH + DN_V_WIDTH
DN_IN = DN_CONV_DIM + DN_V_WIDTH + 2 * DN_V_HEADS
DN_CONV = 4
DN_CHUNK = 64
DEEPNORM_ALPHA = (2 * DEPTH) ** 0.25
DEEPNORM_BETA = (8 * DEPTH) ** -0.25
LN_EPS = 1e-5
RMS_EPS = 1e-6
NEG_INF = -1e30

kernel_name = 'hybrid_moba_gmlp_gdn_macaron_deepnorm'


def layer_norm(x, g, b):
    xf = x.astype(jnp.float32)
    mu = jnp.mean(xf, axis=-1, keepdims=True)
    xc = xf - mu
    var = jnp.mean(xc * xc, axis=-1, keepdims=True)
    y = xc * lax.rsqrt(var + LN_EPS) * g.astype(jnp.float32) + b.astype(jnp.float32)
    return y.astype(x.dtype)


def swiglu(x, w_gate, w_up, w_down):
    return (jax.nn.silu(x @ w_gate) * (x @ w_up)) @ w_down


def rope(x):
    s, d = x.shape[2], x.shape[3]
    half = d // 2
    inv = jnp.exp(-math.log(ROPE_THETA) * jnp.arange(half, dtype=jnp.float32) * (2.0 / d))
    ang = jnp.arange(s, dtype=jnp.float32)[:, None] * inv[None, :]
    cos, sin = jnp.cos(ang), jnp.sin(ang)
    xf = x.astype(jnp.float32)
    x1, x2 = xf[..., :half], xf[..., half:]
    return jnp.concatenate([x1 * cos - x2 * sin, x2 * cos + x1 * sin], axis=-1).astype(x.dtype)


def moba_attention(q, k, v):
    b, h, s, d = q.shape
    nb = -(-s // MOBA_BLOCK)
    pad = nb * MOBA_BLOCK - s
    kp = jnp.pad(k, ((0, 0), (0, 0), (0, pad), (0, 0)))
    vp = jnp.pad(v, ((0, 0), (0, 0), (0, pad), (0, 0)))
    k_blocks = kp.reshape(b, h, nb, MOBA_BLOCK, d)
    v_blocks = vp.reshape(b, h, nb, MOBA_BLOCK, d)
    k_mean = jnp.mean(k_blocks.astype(jnp.float32), axis=3)
    gate = jnp.einsum('bhsd,bhnd->bhsn', q.astype(jnp.float32), k_mean)
    q_blk = jnp.arange(s) // MOBA_BLOCK
    past = jnp.arange(nb)[None, :] < q_blk[:, None]
    gate = jnp.where(past[None, None], gate, NEG_INF)
    n_sel = min(MOBA_TOPK, nb)
    _, sel = lax.top_k(gate, n_sel)
    sel_valid = jnp.arange(n_sel)[None, :] < q_blk[:, None]
    scale = d ** -0.5
    b_idx = jnp.arange(b)[:, None, None, None]
    h_idx = jnp.arange(h)[None, :, None, None]
    qb = MOBA_QUERY_BLOCK

    def attend_block(i):
        start = i * qb
        qi = lax.dynamic_slice_in_dim(q, start, qb, axis=2)
        si = lax.dynamic_slice_in_dim(sel, start, qb, axis=2)
        vi = lax.dynamic_slice_in_dim(sel_valid, start, qb, axis=0)
        qpos = start + jnp.arange(qb)
        kg = k_blocks[b_idx, h_idx, si]
        vg = v_blocks[b_idx, h_idx, si]
        s_sel = jnp.einsum('bhqd,bhqnkd->bhqnk', qi, kg, preferred_element_type=jnp.float32) * scale
        s_sel = jnp.where(vi[None, None, :, :, None], s_sel, NEG_INF).reshape(b, h, qb, n_sel * MOBA_BLOCK)
        own = start // MOBA_BLOCK
        k_own = lax.dynamic_index_in_dim(k_blocks, own, axis=2, keepdims=False)
        v_own = lax.dynamic_index_in_dim(v_blocks, own, axis=2, keepdims=False)
        kpos = own * MOBA_BLOCK + jnp.arange(MOBA_BLOCK)
        s_own = jnp.einsum('bhqd,bhkd->bhqk', qi, k_own, preferred_element_type=jnp.float32) * scale
        s_own = jnp.where((kpos[None, :] <= qpos[:, None])[None, None], s_own, NEG_INF)
        p = jax.nn.softmax(jnp.concatenate([s_sel, s_own], axis=-1), axis=-1).astype(v.dtype)
        p_sel = p[..., :n_sel * MOBA_BLOCK].reshape(b, h, qb, n_sel, MOBA_BLOCK)
        p_own = p[..., n_sel * MOBA_BLOCK:]
        return (jnp.einsum('bhqnk,bhqnkd->bhqd', p_sel, vg)
                + jnp.einsum('bhqk,bhkd->bhqd', p_own, v_own))

    out = lax.map(attend_block, jnp.arange(s // qb))
    return out.transpose(1, 2, 0, 3, 4).reshape(b, h, s, d)


def chunked_spatial_gating(u, v, ln_g, ln_b, w_s, b_s):
    b, s, _ = u.shape
    u = jax.nn.gelu(u).reshape(b, s, G_GROUPS, G_DIM)
    vf = jax.nn.gelu(v).astype(jnp.float32).reshape(b, s, G_GROUPS, G_DIM)
    mu = jnp.mean(vf, axis=-1, keepdims=True)
    vc = vf - mu
    var = jnp.mean(vc * vc, axis=-1, keepdims=True)
    vn = (vc * lax.rsqrt(var + LN_EPS) * ln_g.reshape(G_GROUPS, G_DIM).astype(jnp.float32)
          + ln_b.reshape(G_GROUPS, G_DIM).astype(jnp.float32)).astype(u.dtype)
    vn = vn.reshape(b, s // GMLP_CHUNK, GMLP_CHUNK, G_GROUPS, G_DIM)
    causal = jnp.tril(jnp.ones((GMLP_CHUNK, GMLP_CHUNK), dtype=bool))
    w = jnp.where(causal[None], w_s, jnp.zeros_like(w_s))
    mixed = jnp.einsum('gts,bnsgd->bntgd', w, vn) + b_s.T[None, None, :, :, None]
    return (u * mixed.reshape(b, s, G_GROUPS, G_DIM)).reshape(b, s, G_WIDTH)


def mixer_ab(x, w_in, ln_g, ln_b, w_s, b_s, w_out):
    b, s, _ = x.shape
    hcat = x @ w_in
    q = hcat[..., :A_WIDTH]
    k = hcat[..., A_WIDTH:2 * A_WIDTH]
    v = hcat[..., 2 * A_WIDTH:3 * A_WIDTH]
    gu = hcat[..., 3 * A_WIDTH:3 * A_WIDTH + G_WIDTH]
    gv = hcat[..., 3 * A_WIDTH + G_WIDTH:]
    heads = lambda t: t.reshape(b, s, A_HEADS, HEAD_DIM).transpose(0, 2, 1, 3)
    a_out = moba_attention(rope(heads(q)), rope(heads(k)), heads(v))
    a_out = a_out.transpose(0, 2, 1, 3).reshape(b, s, A_WIDTH)
    g_out = chunked_spatial_gating(gu, gv, ln_g, ln_b, w_s, b_s)
    return jnp.concatenate([a_out, g_out], axis=-1) @ w_out


def causal_conv_silu(x, w):
    kw = w.shape[0]
    s = x.shape[1]
    xp = jnp.pad(x, ((0, 0), (kw - 1, 0), (0, 0)))
    y = xp[:, 0:s] * w[0]
    for j in range(1, kw):
        y = y + xp[:, j:j + s] * w[j]
    return jax.nn.silu(y)


def gated_delta_rule(q, k, v, g, beta):
    b, s, h, dk = q.shape
    dv = v.shape[-1]
    c = DN_CHUNK
    n = s // c

    def to_chunks(t):
        t = t.reshape(b, n, c, h, *t.shape[3:])
        return jnp.swapaxes(jnp.moveaxis(t, 1, 0), 2, 3)

    q, k, v, g, beta = map(to_chunks, (q, k, v, g, beta))
    gc = jnp.cumsum(g, axis=-1)
    incl = jnp.tril(jnp.ones((c, c), dtype=bool))
    strict = jnp.tril(jnp.ones((c, c), dtype=bool), -1)
    decay = jnp.exp(jnp.where(incl, gc[..., :, None] - gc[..., None, :], -jnp.inf))
    kb = k * beta[..., None]
    vb = v * beta[..., None]
    a_low = jnp.where(strict, jnp.einsum('nbhck,nbhek->nbhce', kb, k) * decay, 0.0)
    t_sys = a_low + jnp.eye(c, dtype=a_low.dtype)
    u = lax.linalg.triangular_solve(t_sys, vb, left_side=True, lower=True, unit_diagonal=True)
    w = lax.linalg.triangular_solve(t_sys, kb * jnp.exp(gc)[..., None], left_side=True, lower=True,
                                    unit_diagonal=True)
    attn = jnp.einsum('nbhck,nbhek->nbhce', q, k) * decay
    q_dec = q * jnp.exp(gc)[..., None]
    k_dec = k * jnp.exp(gc[..., -1:] - gc)[..., None]
    g_last = jnp.exp(gc[..., -1])

    def step(state, xs):
        u_c, w_c, qd_c, a_c, kd_c, gl_c = xs
        v_new = u_c - jnp.einsum('bhck,bhkv->bhcv', w_c, state)
        o_c = jnp.einsum('bhck,bhkv->bhcv', qd_c, state) + jnp.einsum('bhce,bhev->bhcv', a_c, v_new)
        state = state * gl_c[..., None, None] + jnp.einsum('bhck,bhcv->bhkv', kd_c, v_new)
        return state, o_c

    state0 = jnp.zeros((b, h, dk, dv), jnp.float32)
    _, o = lax.scan(step, state0, (u, w, q_dec, attn, k_dec, g_last))
    o = jnp.moveaxis(jnp.swapaxes(o, 2, 3), 0, 1)
    return o.reshape(b, s, h, dv)


def mixer_dn(x, w_in, conv_w, a_log, dt_bias, norm_g, w_out):
    b, s, _ = x.shape
    hcat = x @ w_in
    off = DN_CONV_DIM + DN_V_WIDTH
    qkv = causal_conv_silu(hcat[..., :DN_CONV_DIM], conv_w)
    z = hcat[..., DN_CONV_DIM:off]
    bb = hcat[..., off:off + DN_V_HEADS]
    aa = hcat[..., off + DN_V_HEADS:]
    q = qkv[..., :DN_QK_WIDTH].reshape(b, s, DN_QK_HEADS, DN_HEAD_DIM).astype(jnp.float32)
    k = qkv[..., DN_QK_WIDTH:2 * DN_QK_WIDTH].reshape(b, s, DN_QK_HEADS, DN_HEAD_DIM).astype(jnp.float32)
    v = qkv[..., 2 * DN_QK_WIDTH:].reshape(b, s, DN_V_HEADS, DN_HEAD_DIM).astype(jnp.float32)
    q = q * lax.rsqrt(jnp.sum(q * q, axis=-1, keepdims=True) + RMS_EPS) * (DN_HEAD_DIM ** -0.5)
    k = k * lax.rsqrt(jnp.sum(k * k, axis=-1, keepdims=True) + RMS_EPS)
    rep = DN_V_HEADS // DN_QK_HEADS
    q = jnp.repeat(q, rep, axis=2)
    k = jnp.repeat(k, rep, axis=2)
    beta = jax.nn.sigmoid(bb.astype(jnp.float32))
    g = -jnp.exp(a_log.astype(jnp.float32)) * jax.nn.softplus(aa.astype(jnp.float32) + dt_bias.astype(jnp.float32))
    o = gated_delta_rule(q, k, v, g, beta)
    zf = z.reshape(b, s, DN_V_HEADS, DN_HEAD_DIM).astype(jnp.float32)
    o = (o * lax.rsqrt(jnp.mean(o * o, axis=-1, keepdims=True) + RMS_EPS)
         * norm_g.astype(jnp.float32) * jax.nn.silu(zf))
    return o.reshape(b, s, DN_V_WIDTH).astype(x.dtype) @ w_out


def setup_inputs(seed: int = 0) -> dict:
    key = jax.random.key(seed)
    ks = jax.random.split(key, 24)
    f32 = jnp.float32
    ne = (DEPTH + 1) // 2
    no = DEPTH // 2

    def nrm(k, shape, scale):
        return jax.random.normal(k, shape, f32) * scale

    dt = jnp.exp(jax.random.uniform(ks[20], (no, DN_V_HEADS), f32)
                 * (math.log(0.1) - math.log(0.001)) + math.log(0.001))
    dt = jnp.maximum(dt, 1e-4)
    return {
        'x': nrm(ks[0], (BATCH, SEQ, D_MODEL), 1.0),
        'ffn1_w_gate': nrm(ks[1], (DEPTH, D_MODEL, D_FF), D_MODEL ** -0.5),
        'ffn1_w_up': nrm(ks[2], (DEPTH, D_MODEL, D_FF), D_MODEL ** -0.5),
        'ffn1_w_down': nrm(ks[3], (DEPTH, D_FF, D_MODEL), DEEPNORM_BETA * D_FF ** -0.5),
        'ffn2_w_gate': nrm(ks[4], (DEPTH, D_MODEL, D_FF), D_MODEL ** -0.5),
        'ffn2_w_up': nrm(ks[5], (DEPTH, D_MODEL, D_FF), D_MODEL ** -0.5),
        'ffn2_w_down': nrm(ks[6], (DEPTH, D_FF, D_MODEL), DEEPNORM_BETA * D_FF ** -0.5),
        'ln_g': 1.0 + nrm(ks[7], (DEPTH, 3, D_MODEL), 0.02),
        'ln_b': nrm(ks[8], (DEPTH, 3, D_MODEL), 0.02),
        'ab_w_in': nrm(ks[9], (ne, D_MODEL, AB_IN), D_MODEL ** -0.5),
        'ab_gmlp_ln_g': 1.0 + nrm(ks[10], (ne, G_WIDTH), 0.02),
        'ab_gmlp_ln_b': nrm(ks[11], (ne, G_WIDTH), 0.02),
        'ab_gmlp_w_s': nrm(ks[12], (ne, G_GROUPS, GMLP_CHUNK, GMLP_CHUNK), 0.5 * GMLP_CHUNK ** -0.5),
        'ab_gmlp_b_s': 1.0 + nrm(ks[13], (ne, G_GROUPS, GMLP_CHUNK), 0.02),
        'ab_w_out': nrm(ks[14], (ne, AB_MIX, D_MODEL), DEEPNORM_BETA * AB_MIX ** -0.5),
        'dn_w_in': nrm(ks[15], (no, D_MODEL, DN_IN), D_MODEL ** -0.5),
        'dn_conv_w': nrm(ks[16], (no, DN_CONV, DN_CONV_DIM), DN_CONV ** -0.5),
        'dn_a_log': jnp.log(jax.random.uniform(ks[17], (no, DN_V_HEADS), f32, 1.0, 16.0)),
        'dn_dt_bias': dt + jnp.log(-jnp.expm1(-dt)),
        'dn_norm_g': 1.0 + nrm(ks[18], (no, DN_HEAD_DIM), 0.02),
        'dn_w_out': nrm(ks[19], (no, DN_V_WIDTH, D_MODEL), DEEPNORM_BETA * DN_V_WIDTH ** -0.5),
    }


def reference(x, ffn1_w_gate, ffn1_w_up, ffn1_w_down, ffn2_w_gate, ffn2_w_up, ffn2_w_down,
              ln_g, ln_b, ab_w_in, ab_gmlp_ln_g, ab_gmlp_ln_b, ab_gmlp_w_s, ab_gmlp_b_s, ab_w_out,
              dn_w_in, dn_conv_w, dn_a_log, dn_dt_bias, dn_norm_g, dn_w_out):
    for i in range(DEPTH):
        f1 = swiglu(x, ffn1_w_gate[i], ffn1_w_up[i], ffn1_w_down[i])
        x = layer_norm(DEEPNORM_ALPHA * x + 0.5 * f1, ln_g[i, 0], ln_b[i, 0])
        j = i // 2
        if i % 2 == 0:
            m = mixer_ab(x, ab_w_in[j], ab_gmlp_ln_g[j], ab_gmlp_ln_b[j], ab_gmlp_w_s[j],
                         ab_gmlp_b_s[j], ab_w_out[j])
        else:
            m = mixer_dn(x, dn_w_in[j], dn_conv_w[j], dn_a_log[j], dn_dt_bias[j], dn_norm_g[j],
                         dn_w_out[j])
        x = layer_norm(DEEPNORM_ALPHA * x + m, ln_g[i, 1], ln_b[i, 1])
        f2 = swiglu(x, ffn2_w_gate[i], ffn2_w_up[i], ffn2_w_down[i])
        x = layer_norm(DEEPNORM_ALPHA * x + 0.5 * f2, ln_g[i, 2], ln_b[i, 2])
    return x
```

```python
import functools
import math

import jax
import jax.numpy as jnp
from jax import lax
from jax.experimental import pallas as pl
from jax.experimental.pallas import tpu as pltpu

F32 = jnp.float32
BF16 = jnp.bfloat16

D_MODEL = 2048
SEQ = 2048
DEPTH = 2
HEAD_DIM = 128
A_HEADS = 8
A_WIDTH = A_HEADS * HEAD_DIM
MOBA_BLOCK = 256
MOBA_TOPK = 3
ROPE_THETA = 10000.0
G_GROUPS = 8
G_DIM = 128
G_WIDTH = G_GROUPS * G_DIM
GMLP_CHUNK = 128
DN_QK_HEADS = 16
DN_V_HEADS = 32
DN_HEAD_DIM = 128
DN_QK_WIDTH = DN_QK_HEADS * DN_HEAD_DIM
DN_V_WIDTH = DN_V_HEADS * DN_HEAD_DIM
DN_CONV_DIM = 2 * DN_QK_WIDTH + DN_V_WIDTH
DN_MAIN = DN_CONV_DIM + DN_V_WIDTH
DN_CONV = 4
DN_CHUNK = 64
DEEPNORM_ALPHA = (2 * DEPTH) ** 0.25
LN_EPS = 1e-5
RMS_EPS = 1e-6
NEG_INF = -1e30

VMEM_LIMIT_BYTES = 56 * 1024 * 1024

NN = (((1,), (0,)), ((), ()))
NT = (((1,), (1,)), ((), ()))
TN = (((0,), (0,)), ((), ()))


def _dot(a, b, dims=NN):
    return lax.dot_general(a, b, dims, preferred_element_type=F32)


def _split2(a):
    hi = a.astype(BF16)
    lo = (a - hi.astype(F32)).astype(BF16)
    return hi, lo


def _dot_hp(a, b, dims=NN):
    a_hi, a_lo = _split2(a)
    b_hi, b_lo = _split2(b)
    return _dot(a_hi, b_hi, dims) + (_dot(a_hi, b_lo, dims) + _dot(a_lo, b_hi, dims))


def _dot_exact_lhs(a_bf16, b, dims=NN):
    b0 = b.astype(BF16)
    r1 = b - b0.astype(F32)
    b1 = r1.astype(BF16)
    b2 = (r1 - b1.astype(F32)).astype(BF16)
    return _dot(a_bf16, b0, dims) + (_dot(a_bf16, b1, dims) + _dot(a_bf16, b2, dims))


def _dot_exact_rhs(a, b_bf16, dims=NN):
    a0 = a.astype(BF16)
    r1 = a - a0.astype(F32)
    a1 = r1.astype(BF16)
    a2 = (r1 - a1.astype(F32)).astype(BF16)
    return _dot(a0, b_bf16, dims) + (_dot(a1, b_bf16, dims) + _dot(a2, b_bf16, dims))


def _layer_norm(y, g, b):
    mu = jnp.mean(y, axis=-1, keepdims=True)
    yc = y - mu
    var = jnp.mean(yc * yc, axis=-1, keepdims=True)
    return yc * lax.rsqrt(var + LN_EPS) * g + b


def _silu(x):
    return x * jax.nn.sigmoid(x)


def _params(*sem):
    return pltpu.CompilerParams(dimension_semantics=sem, vmem_limit_bytes=VMEM_LIMIT_BYTES)


def _ffn_kernel(x_ref, wg_ref, wu_ref, wd_ref, g_ref, b_ref, o_ref, ob_ref, xb_ref, acc_ref):
    j = pl.program_id(1)

    @pl.when(j == 0)
    def _():
        xb_ref[...] = x_ref[...].astype(BF16)
        acc_ref[...] = jnp.zeros_like(acc_ref)

    xb = xb_ref[...]
    gate = _dot(xb, wg_ref[...])
    up = _dot(xb, wu_ref[...])
    h = (_silu(gate) * up).astype(BF16)
    acc_ref[...] += _dot(h, wd_ref[...])

    @pl.when(j == pl.num_programs(1) - 1)
    def _():
        y = DEEPNORM_ALPHA * x_ref[...] + 0.5 * acc_ref[...]
        y = _layer_norm(y, g_ref[...], b_ref[...])
        o_ref[...] = y
        ob_ref[...] = y.astype(BF16)


def _ffn_ln(x, wg, wu, wd, ln_g, ln_b, tm=512, tf=512):
    t, d = x.shape
    dff = wg.shape[1]
    return pl.pallas_call(
        _ffn_kernel,
        grid=(t // tm, dff // tf),
        in_specs=[
            pl.BlockSpec((tm, d), lambda i, j: (i, 0)),
            pl.BlockSpec((d, tf), lambda i, j: (0, j)),
            pl.BlockSpec((d, tf), lambda i, j: (0, j)),
            pl.BlockSpec((tf, d), lambda i, j: (j, 0)),
            pl.BlockSpec((1, d), lambda i, j: (0, 0)),
            pl.BlockSpec((1, d), lambda i, j: (0, 0)),
        ],
        out_specs=[
            pl.BlockSpec((tm, d), lambda i, j: (i, 0)),
            pl.BlockSpec((tm, d), lambda i, j: (i, 0)),
        ],
        out_shape=[jax.ShapeDtypeStruct((t, d), F32), jax.ShapeDtypeStruct((t, d), BF16)],
        scratch_shapes=[pltpu.VMEM((tm, d), BF16), pltpu.VMEM((tm, d), F32)],
        compiler_params=_params("parallel", "arbitrary"),
        name="ffn_ln",
    )(x, wg, wu, wd, ln_g.reshape(1, d), ln_b.reshape(1, d))


def _matmul_kernel(x_ref, w_ref, o_ref):
    o_ref[...] = _dot(x_ref[...], w_ref[...])


def _matmul(x, w, tm=1024, tn=512):
    t, k = x.shape
    n = w.shape[1]
    return pl.pallas_call(
        _matmul_kernel,
        grid=(t // tm, n // tn),
        in_specs=[
            pl.BlockSpec((tm, k), lambda i, j: (i, 0)),
            pl.BlockSpec((k, tn), lambda i, j: (0, j)),
        ],
        out_specs=pl.BlockSpec((tm, tn), lambda i, j: (i, j)),
        out_shape=jax.ShapeDtypeStruct((t, n), F32),
        compiler_params=_params("parallel", "arbitrary"),
        name="in_proj",
    )(x, w)


def _proj_ln_kernel(a_ref, w_ref, x_ref, g_ref, b_ref, o_ref, ob_ref, acc_ref):
    j = pl.program_id(1)

    @pl.when(j == 0)
    def _():
        acc_ref[...] = jnp.zeros_like(acc_ref)

    acc_ref[...] += _dot(a_ref[...], w_ref[...])

    @pl.when(j == pl.num_programs(1) - 1)
    def _():
        y = DEEPNORM_ALPHA * x_ref[...] + acc_ref[...]
        y = _layer_norm(y, g_ref[...], b_ref[...])
        o_ref[...] = y
        ob_ref[...] = y.astype(BF16)


def _proj_ln(a, w, x, ln_g, ln_b, tm=512, tk=1024):
    t, k = a.shape
    d = w.shape[1]
    return pl.pallas_call(
        _proj_ln_kernel,
        grid=(t // tm, k // tk),
        in_specs=[
            pl.BlockSpec((tm, tk), lambda i, j: (i, j)),
            pl.BlockSpec((tk, d), lambda i, j: (j, 0)),
            pl.BlockSpec((tm, d), lambda i, j: (i, 0)),
            pl.BlockSpec((1, d), lambda i, j: (0, 0)),
            pl.BlockSpec((1, d), lambda i, j: (0, 0)),
        ],
        out_specs=[
            pl.BlockSpec((tm, d), lambda i, j: (i, 0)),
            pl.BlockSpec((tm, d), lambda i, j: (i, 0)),
        ],
        out_shape=[jax.ShapeDtypeStruct((t, d), F32), jax.ShapeDtypeStruct((t, d), BF16)],
        scratch_shapes=[pltpu.VMEM((tm, d), F32)],
        compiler_params=_params("parallel", "arbitrary"),
        name="proj_ln",
    )(a, w, x, ln_g.reshape(1, d), ln_b.reshape(1, d))


def _rope(x, cos_full, sin_signed):
    return x * cos_full + pltpu.roll(x, HEAD_DIM // 2, 1) * sin_signed


def _moba_kernel(q_ref, k_ref, v_ref, cos_ref, sin_ref, o_ref):
    s = q_ref.shape[0]
    nb = s // MOBA_BLOCK
    blk = MOBA_BLOCK
    cos_full = cos_ref[...]
    sin_signed = sin_ref[...]
    qr = _rope(q_ref[...], cos_full, sin_signed)
    kr = _rope(k_ref[...], cos_full, sin_signed)

    row = lax.broadcasted_iota(jnp.int32, (HEAD_DIM, s), 0)
    col = lax.broadcasted_iota(jnp.int32, (HEAD_DIM, s), 1)
    avg = jnp.where(col // blk == row, 1.0 / blk, 0.0).astype(BF16)
    k_mean = _dot_exact_lhs(avg, kr)
    gate = _dot_hp(qr, k_mean, NT)

    qb = qr.astype(BF16)
    kb = kr.astype(BF16)
    vb = v_ref[...].astype(BF16)
    scale = HEAD_DIM ** -0.5

    lane = lax.broadcasted_iota(jnp.int32, (blk, HEAD_DIM), 1)
    qpos = lax.broadcasted_iota(jnp.int32, (blk, blk), 0)
    kpos = lax.broadcasted_iota(jnp.int32, (blk, blk), 1)
    causal = kpos <= qpos

    for i in range(nb):
        rows = slice(i * blk, (i + 1) * blk)
        qi = qb[rows]
        ranks = []
        if i > MOBA_TOPK:
            gi = gate[rows]
            valid = lane < i
            for n in range(i):
                gn = gi[:, n:n + 1]
                beats = valid & ((gi > gn) | ((gi == gn) & (lane < n)))
                ranks.append(jnp.sum(jnp.where(beats, 1.0, 0.0), axis=-1, keepdims=True))
        scores = []
        for n in range(i + 1):
            sn = _dot(qi, kb[n * blk:(n + 1) * blk], NT) * scale
            if n == i:
                sn = jnp.where(causal, sn, NEG_INF)
            elif ranks:
                selected = jnp.broadcast_to(ranks[n], sn.shape) < float(MOBA_TOPK)
                sn = jnp.where(selected, sn, NEG_INF)
            scores.append(sn)
        m = scores[0].max(axis=-1, keepdims=True)
        for sn in scores[1:]:
            m = jnp.maximum(m, sn.max(axis=-1, keepdims=True))
        l = jnp.zeros((blk, 1), F32)
        acc = jnp.zeros((blk, HEAD_DIM), F32)
        for n, sn in enumerate(scores):
            p = jnp.exp(sn - m)
            l = l + p.sum(axis=-1, keepdims=True)
            acc = acc + _dot(p.astype(BF16), vb[n * blk:(n + 1) * blk])
        o_ref[rows, :] = (acc / l).astype(o_ref.dtype)


def _moba(hcat, cos_full, sin_signed, batch):
    t = hcat.shape[0]
    s = t // batch
    spec = lambda off: pl.BlockSpec((s, HEAD_DIM), lambda b, h: (b, off + h))
    tab = pl.BlockSpec((s, HEAD_DIM), lambda b, h: (0, 0))
    return pl.pallas_call(
        _moba_kernel,
        grid=(batch, A_HEADS),
        in_specs=[spec(0), spec(A_HEADS), spec(2 * A_HEADS), tab, tab],
        out_specs=pl.BlockSpec((s, HEAD_DIM), lambda b, h: (b, h)),
        out_shape=jax.ShapeDtypeStruct((t, A_WIDTH), BF16),
        compiler_params=_params("parallel", "parallel"),
        name="moba",
    )(hcat, hcat, hcat, cos_full, sin_signed)


def _gmlp_kernel(u_ref, v_ref, lng_ref, lnb_ref, ws_ref, bs_ref, o_ref):
    rows = u_ref.shape[0]
    tpos = lax.broadcasted_iota(jnp.int32, (GMLP_CHUNK, GMLP_CHUNK), 0)
    spos = lax.broadcasted_iota(jnp.int32, (GMLP_CHUNK, GMLP_CHUNK), 1)
    causal = spos <= tpos
    for g in range(G_GROUPS):
        cols = slice(g * G_DIM, (g + 1) * G_DIM)
        w = jnp.where(causal, ws_ref[g], 0.0).astype(BF16)
        bias = bs_ref[:, g:g + 1]
        ln_g = lng_ref[:, cols]
        ln_b = lnb_ref[:, cols]
        for c in range(rows // GMLP_CHUNK):
            r = slice(c * GMLP_CHUNK, (c + 1) * GMLP_CHUNK)
            vn = _layer_norm(jax.nn.gelu(v_ref[r, cols]), ln_g, ln_b)
            mixed = _dot(w, vn.astype(BF16)) + bias
            o_ref[r, cols] = (jax.nn.gelu(u_ref[r, cols]) * mixed).astype(o_ref.dtype)


def _gmlp(hcat, ln_g, ln_b, w_s, b_s, rows=512):
    t = hcat.shape[0]
    u_blk = 3 * A_WIDTH // G_WIDTH
    return pl.pallas_call(
        _gmlp_kernel,
        grid=(t // rows,),
        in_specs=[
            pl.BlockSpec((rows, G_WIDTH), lambda i: (i, u_blk)),
            pl.BlockSpec((rows, G_WIDTH), lambda i: (i, u_blk + 1)),
            pl.BlockSpec((1, G_WIDTH), lambda i: (0, 0)),
            pl.BlockSpec((1, G_WIDTH), lambda i: (0, 0)),
            pl.BlockSpec((G_GROUPS, GMLP_CHUNK, GMLP_CHUNK), lambda i: (0, 0, 0)),
            pl.BlockSpec((GMLP_CHUNK, G_GROUPS), lambda i: (0, 0)),
        ],
        out_specs=pl.BlockSpec((rows, G_WIDTH), lambda i: (i, 0)),
        out_shape=jax.ShapeDtypeStruct((t, G_WIDTH), BF16),
        compiler_params=_params("parallel"),
        name="gmlp",
    )(hcat, hcat, ln_g.reshape(1, G_WIDTH), ln_b.reshape(1, G_WIDTH), w_s, b_s.T)


GATE_BLK = 128


def _softplus(x):
    return jnp.maximum(x, 0.0) + jnp.log1p(jnp.exp(-jnp.abs(x)))


def _dn_gate_kernel(x_ref, w_ref, wt_ref, ng_ref, dtb_ref, ngc_ref, dtbc_ref,
                    bg_ref, gc_ref, gct_ref):
    s = x_ref.shape[0]
    hv = DN_V_HEADS
    x = x_ref[...]
    ba = _dot(x, w_ref[...])
    bat = _dot(wt_ref[...], x, NT)
    lane = lax.broadcasted_iota(jnp.int32, (s, 2 * hv), 1)
    g = ng_ref[...] * _softplus(ba + dtb_ref[...])
    bg = jnp.where(lane < hv, jax.nn.sigmoid(ba), g)
    bg_ref[...] = bg
    gt = ngc_ref[...] * _softplus(bat + dtbc_ref[...])

    r = lax.broadcasted_iota(jnp.int32, (GATE_BLK, GATE_BLK), 0)
    c = lax.broadcasted_iota(jnp.int32, (GATE_BLK, GATE_BLK), 1)
    same = (r // DN_CHUNK) == (c // DN_CHUNK)
    lower = jnp.where(same & (c <= r), 1.0, 0.0).astype(BF16)
    upper = jnp.where(same & (r <= c), 1.0, 0.0).astype(BF16)
    for i in range(s // GATE_BLK):
        sl = slice(i * GATE_BLK, (i + 1) * GATE_BLK)
        gc_ref[sl, :] = _dot_exact_lhs(lower, bg[sl, :])
        gct_ref[:, sl] = _dot_exact_rhs(gt[:, sl], upper)


def _dn_gates(xb, w_ba, a_log, dt_bias, batch):
    t, d = xb.shape
    s = t // batch
    hv = DN_V_HEADS
    neg_rate = -jnp.exp(a_log.astype(F32))
    zeros = jnp.zeros((hv,), F32)
    ng = jnp.concatenate([zeros, neg_rate]).reshape(1, 2 * hv)
    dtb = jnp.concatenate([zeros, dt_bias.astype(F32)]).reshape(1, 2 * hv)
    small = lambda shape: pl.BlockSpec(shape, lambda b: (0, 0))
    return pl.pallas_call(
        _dn_gate_kernel,
        grid=(batch,),
        in_specs=[
            pl.BlockSpec((s, d), lambda b: (b, 0)),
            small((d, 2 * hv)),
            small((2 * hv, d)),
            small((1, 2 * hv)),
            small((1, 2 * hv)),
            small((2 * hv, 1)),
            small((2 * hv, 1)),
        ],
        out_specs=[
            pl.BlockSpec((s, 2 * hv), lambda b: (b, 0)),
            pl.BlockSpec((s, 2 * hv), lambda b: (b, 0)),
            pl.BlockSpec((2 * hv, s), lambda b: (0, b)),
        ],
        out_shape=[
            jax.ShapeDtypeStruct((t, 2 * hv), F32),
            jax.ShapeDtypeStruct((t, 2 * hv), F32),
            jax.ShapeDtypeStruct((2 * hv, t), F32),
        ],
        compiler_params=_params("parallel"),
        name="dn_gates",
    )(xb, w_ba, w_ba.T, ng, dtb, ng.reshape(2 * hv, 1), dtb.reshape(2 * hv, 1))


DN_TB = 128
DN_HB = 8
CONV_PAD = 8


def _conv_silu(x_ref, w_ref, buf_ref, first):
    tb = x_ref.shape[0]

    @pl.when(first)
    def _():
        buf_ref[0:CONV_PAD, :] = jnp.zeros((CONV_PAD, buf_ref.shape[1]), F32)

    buf_ref[CONV_PAD:CONV_PAD + tb, :] = x_ref[...]
    y = buf_ref[CONV_PAD:CONV_PAD + tb, :] * w_ref[DN_CONV - 1:DN_CONV, :]
    for j in range(1, DN_CONV):
        tap = DN_CONV - 1 - j
        y = y + buf_ref[CONV_PAD - j:CONV_PAD - j + tb, :] * w_ref[tap:tap + 1, :]
    buf_ref[0:CONV_PAD, :] = buf_ref[tb:tb + CONV_PAD, :]
    return _silu(y)


def _unit_lower_inverse(a_low):
    c = a_low.shape[0]
    eye = jnp.where(lax.broadcasted_iota(jnp.int32, (c, c), 0)
                    == lax.broadcasted_iota(jnp.int32, (c, c), 1), 1.0, 0.0)
    inv = eye - a_low
    power = a_low
    span = 2
    while span < c:
        power = _dot_hp(power, power)
        inv = inv + _dot_hp(inv, power)
        span *= 2
    return inv


def _delta_kernel(q_ref, k_ref, v_ref, z_ref, wq_ref, wk_ref, wv_ref, bg_ref, gc_ref, gct_ref,
                  ng_ref, o_ref, state_ref, qbuf_ref, kbuf_ref, vbuf_ref):
    tb = q_ref.shape[0]
    hv = DN_V_HEADS
    c = DN_CHUNK
    dk = DN_HEAD_DIM
    rep = DN_V_HEADS // DN_QK_HEADS
    group = pl.program_id(1)
    first = pl.program_id(2) == 0

    @pl.when(first)
    def _():
        state_ref[...] = jnp.zeros_like(state_ref)

    q_all = _conv_silu(q_ref, wq_ref, qbuf_ref, first)
    k_all = _conv_silu(k_ref, wk_ref, kbuf_ref, first)
    v_all = _conv_silu(v_ref, wv_ref, vbuf_ref, first)

    ri = lax.broadcasted_iota(jnp.int32, (c, c), 0)
    ci = lax.broadcasted_iota(jnp.int32, (c, c), 1)
    incl = ci <= ri
    strict = ci < ri
    norm_g = ng_ref[...]

    bg = bg_ref[...]
    gcs = gc_ref[...]
    gct = gct_ref[...]
    head_lane = lax.broadcasted_iota(jnp.int32, (tb, 2 * hv), 1)
    head_row = lax.broadcasted_iota(jnp.int32, (hv, tb), 0)

    for jq in range(DN_HB // rep):
        qcols = slice(jq * dk, (jq + 1) * dk)
        q_h = q_all[:, qcols]
        k_h = k_all[:, qcols]
        q_h = q_h * lax.rsqrt(jnp.sum(q_h * q_h, axis=-1, keepdims=True) + RMS_EPS) * (dk ** -0.5)
        k_h = k_h * lax.rsqrt(jnp.sum(k_h * k_h, axis=-1, keepdims=True) + RMS_EPS)
        for jr in range(rep):
            hl = jq * rep + jr
            head = group * DN_HB + hl
            vcols = slice(hl * dk, (hl + 1) * dk)
            beta_col = jnp.sum(jnp.where(head_lane == head, bg, 0.0), axis=-1, keepdims=True)
            gc_col = jnp.sum(jnp.where(head_lane == head + hv, gcs, 0.0), axis=-1, keepdims=True)
            gc_row = jnp.sum(jnp.where(head_row == head, gct, 0.0), axis=0, keepdims=True)
            state = state_ref[hl]
            for ch in range(tb // c):
                rows = slice(ch * c, (ch + 1) * c)
                q_c = q_h[rows]
                k_c = k_h[rows]
                v_c = v_all[rows, vcols]
                beta_c = beta_col[rows]
                gcc = gc_col[rows]
                gcr = gc_row[:, rows]
                g_last = gcc[c - 1:c, :]
                decay = jnp.exp(jnp.where(incl, gcc - gcr, -jnp.inf))
                k_bf = k_c.astype(BF16)
                qk = _dot(q_c.astype(BF16), k_bf, NT)
                kk = _dot(k_bf, k_bf, NT)
                a_low = jnp.where(strict, kk * beta_c * decay, 0.0)
                t_inv = _unit_lower_inverse(a_low)
                egc = jnp.exp(gcc)
                kb = k_c * beta_c
                u = _dot_hp(t_inv, v_c * beta_c)
                w = _dot_hp(t_inv, kb * egc)
                attn = qk * decay
                state_bf = state.astype(BF16)
                v_new = u - _dot(w.astype(BF16), state_bf)
                v_new_bf = v_new.astype(BF16)
                o_c = (_dot((q_c * egc).astype(BF16), state_bf)
                       + _dot(attn.astype(BF16), v_new_bf))
                k_dec = (k_c * jnp.exp(g_last - gcc)).astype(BF16)
                state = state * jnp.exp(g_last) + _dot(k_dec, v_new_bf, TN)
                z_c = z_ref[rows, vcols]
                o_n = (o_c * lax.rsqrt(jnp.mean(o_c * o_c, axis=-1, keepdims=True) + RMS_EPS)
                       * norm_g * _silu(z_c))
                o_ref[rows, vcols] = o_n.astype(o_ref.dtype)
            state_ref[hl] = state


def _delta(hcat, conv_w, bg, gcs, gct, norm_g, batch):
    t = hcat.shape[0]
    s = t // batch
    nt = s // DN_TB
    rep = DN_V_HEADS // DN_QK_HEADS
    wq = DN_HB // rep * DN_HEAD_DIM
    wv = DN_HB * DN_HEAD_DIM
    k_off = DN_QK_WIDTH // wq
    v_off = 2 * DN_QK_WIDTH // wv
    z_off = DN_CONV_DIM // wv
    hv2 = 2 * DN_V_HEADS
    row = lambda b, g, i: b * nt + i
    return pl.pallas_call(
        _delta_kernel,
        grid=(batch, DN_V_HEADS // DN_HB, nt),
        in_specs=[
            pl.BlockSpec((DN_TB, wq), lambda b, g, i: (row(b, g, i), g)),
            pl.BlockSpec((DN_TB, wq), lambda b, g, i: (row(b, g, i), k_off + g)),
            pl.BlockSpec((DN_TB, wv), lambda b, g, i: (row(b, g, i), v_off + g)),
            pl.BlockSpec((DN_TB, wv), lambda b, g, i: (row(b, g, i), z_off + g)),
            pl.BlockSpec((DN_CONV, wq), lambda b, g, i: (0, g)),
            pl.BlockSpec((DN_CONV, wq), lambda b, g, i: (0, k_off + g)),
            pl.BlockSpec((DN_CONV, wv), lambda b, g, i: (0, v_off + g)),
            pl.BlockSpec((DN_TB, hv2), lambda b, g, i: (row(b, g, i), 0)),
            pl.BlockSpec((DN_TB, hv2), lambda b, g, i: (row(b, g, i), 0)),
            pl.BlockSpec((DN_V_HEADS, DN_TB), lambda b, g, i: (1, row(b, g, i))),
            pl.BlockSpec((1, DN_HEAD_DIM), lambda b, g, i: (0, 0)),
        ],
        out_specs=pl.BlockSpec((DN_TB, wv), lambda b, g, i: (row(b, g, i), g)),
        out_shape=jax.ShapeDtypeStruct((t, DN_V_WIDTH), BF16),
        scratch_shapes=[
            pltpu.VMEM((DN_HB, DN_HEAD_DIM, DN_HEAD_DIM), F32),
            pltpu.VMEM((CONV_PAD + DN_TB, wq), F32),
            pltpu.VMEM((CONV_PAD + DN_TB, wq), F32),
            pltpu.VMEM((CONV_PAD + DN_TB, wv), F32),
        ],
        compiler_params=_params("parallel", "parallel", "arbitrary"),
        name="delta_rule",
    )(hcat, hcat, hcat, hcat, conv_w, conv_w, conv_w, bg, gcs, gct,
      norm_g.reshape(1, DN_HEAD_DIM))


def _rope_tables(s):
    half = HEAD_DIM // 2
    inv = jnp.exp(-math.log(ROPE_THETA) * jnp.arange(half, dtype=F32) * (2.0 / HEAD_DIM))
    ang = jnp.arange(s, dtype=F32)[:, None] * inv[None, :]
    cos, sin = jnp.cos(ang), jnp.sin(ang)
    return jnp.concatenate([cos, cos], axis=-1), jnp.concatenate([-sin, sin], axis=-1)


def kernel(x, ffn1_w_gate, ffn1_w_up, ffn1_w_down, ffn2_w_gate, ffn2_w_up, ffn2_w_down, ln_g, ln_b, ab_w_in, ab_gmlp_ln_g, ab_gmlp_ln_b, ab_gmlp_w_s, ab_gmlp_b_s, ab_w_out, dn_w_in, dn_conv_w, dn_a_log, dn_dt_bias, dn_norm_g, dn_w_out):
    batch, s, d = x.shape
    t = batch * s
    bf = lambda w: w.astype(BF16)
    xf = x.reshape(t, d)
    cos_full, sin_signed = _rope_tables(s)

    for i in range(DEPTH):
        xf, xb = _ffn_ln(xf, bf(ffn1_w_gate[i]), bf(ffn1_w_up[i]), bf(ffn1_w_down[i]),
                         ln_g[i, 0], ln_b[i, 0])
        j = i // 2
        if i % 2 == 0:
            hcat = _matmul(xb, bf(ab_w_in[j]))
            a_out = _moba(hcat, cos_full, sin_signed, batch)
            g_out = _gmlp(hcat, ab_gmlp_ln_g[j], ab_gmlp_ln_b[j], ab_gmlp_w_s[j], ab_gmlp_b_s[j])
            mix = jnp.concatenate([a_out, g_out], axis=-1)
            w_out = bf(ab_w_out[j])
        else:
            w_in = dn_w_in[j]
            hcat = _matmul(xb, bf(w_in[:, :DN_MAIN]))
            bg, gcs, gct = _dn_gates(xb, bf(w_in[:, DN_MAIN:]), dn_a_log[j], dn_dt_bias[j], batch)
            mix = _delta(hcat, dn_conv_w[j], bg, gcs, gct, dn_norm_g[j], batch)
            w_out = bf(dn_w_out[j])
        xf, xb = _proj_ln(mix, w_out, xf, ln_g[i, 1], ln_b[i, 1])
        xf, xb = _ffn_ln(xf, bf(ffn2_w_gate[i]), bf(ffn2_w_up[i]), bf(ffn2_w_down[i]),
                         ln_g[i, 2], ln_b[i, 2])
    return xf.reshape(batch, s, d)
```

```python
import functools
import math

import jax
import jax.numpy as jnp
from jax import lax
from jax.experimental import pallas as pl
from jax.experimental.pallas import tpu as pltpu

F32 = jnp.float32
BF16 = jnp.bfloat16

D_MODEL = 2048
SEQ = 2048
DEPTH = 2
HEAD_DIM = 128
A_HEADS = 8
A_WIDTH = A_HEADS * HEAD_DIM
MOBA_BLOCK = 256
MOBA_TOPK = 3
ROPE_THETA = 10000.0
G_GROUPS = 8
G_DIM = 128
G_WIDTH = G_GROUPS * G_DIM
GMLP_CHUNK = 128
DN_QK_HEADS = 16
DN_V_HEADS = 32
DN_HEAD_DIM = 128
DN_QK_WIDTH = DN_QK_HEADS * DN_HEAD_DIM
DN_V_WIDTH = DN_V_HEADS * DN_HEAD_DIM
DN_CONV_DIM = 2 * DN_QK_WIDTH + DN_V_WIDTH
DN_MAIN = DN_CONV_DIM + DN_V_WIDTH
DN_CONV = 4
DN_CHUNK = 64
DEEPNORM_ALPHA = (2 * DEPTH) ** 0.25
LN_EPS = 1e-5
RMS_EPS = 1e-6
NEG_INF = -1e30

VMEM_LIMIT_BYTES = 56 * 1024 * 1024

NN = (((1,), (0,)), ((), ()))
NT = (((1,), (1,)), ((), ()))
TN = (((0,), (0,)), ((), ()))


def _dot(a, b, dims=NN):
    return lax.dot_general(a, b, dims, preferred_element_type=F32)


def _split2(a):
    hi = a.astype(BF16)
    lo = (a - hi.astype(F32)).astype(BF16)
    return hi, lo


def _dot_hp(a, b, dims=NN):
    a_hi, a_lo = _split2(a)
    b_hi, b_lo = _split2(b)
    return _dot(a_hi, b_hi, dims) + (_dot(a_hi, b_lo, dims) + _dot(a_lo, b_hi, dims))


def _dot_exact_lhs(a_bf16, b, dims=NN):
    b0 = b.astype(BF16)
    r1 = b - b0.astype(F32)
    b1 = r1.astype(BF16)
    b2 = (r1 - b1.astype(F32)).astype(BF16)
    return _dot(a_bf16, b0, dims) + (_dot(a_bf16, b1, dims) + _dot(a_bf16, b2, dims))


def _dot_exact_rhs(a, b_bf16, dims=NN):
    a0 = a.astype(BF16)
    r1 = a - a0.astype(F32)
    a1 = r1.astype(BF16)
    a2 = (r1 - a1.astype(F32)).astype(BF16)
    return _dot(a0, b_bf16, dims) + (_dot(a1, b_bf16, dims) + _dot(a2, b_bf16, dims))


def _layer_norm(y, g, b):
    mu = jnp.mean(y, axis=-1, keepdims=True)
    yc = y - mu
    var = jnp.mean(yc * yc, axis=-1, keepdims=True)
    return yc * lax.rsqrt(var + LN_EPS) * g + b


def _silu(x):
    return x * jax.nn.sigmoid(x)


def _params(*sem):
    return pltpu.CompilerParams(dimension_semantics=sem, vmem_limit_bytes=VMEM_LIMIT_BYTES)


def _ffn_kernel(x_ref, wg_ref, wu_ref, wd_ref, g_ref, b_ref, o_ref, ob_ref, xb_ref, acc_ref):
    j = pl.program_id(1)

    @pl.when(j == 0)
    def _():
        xb_ref[...] = x_ref[...].astype(BF16)
        acc_ref[...] = jnp.zeros_like(acc_ref)

    xb = xb_ref[...]
    gate = _dot(xb, wg_ref[...])
    up = _dot(xb, wu_ref[...])
    h = (_silu(gate) * up).astype(BF16)
    acc_ref[...] += _dot(h, wd_ref[...])

    @pl.when(j == pl.num_programs(1) - 1)
    def _():
        y = DEEPNORM_ALPHA * x_ref[...] + 0.5 * acc_ref[...]
        y = _layer_norm(y, g_ref[...], b_ref[...])
        o_ref[...] = y
        ob_ref[...] = y.astype(BF16)


def _ffn_ln(x, wg, wu, wd, ln_g, ln_b, tm=512, tf=512):
    t, d = x.shape
    dff = wg.shape[1]
    return pl.pallas_call(
        _ffn_kernel,
        grid=(t // tm, dff // tf),
        in_specs=[
            pl.BlockSpec((tm, d), lambda i, j: (i, 0)),
            pl.BlockSpec((d, tf), lambda i, j: (0, j)),
            pl.BlockSpec((d, tf), lambda i, j: (0, j)),
            pl.BlockSpec((tf, d), lambda i, j: (j, 0)),
            pl.BlockSpec((1, d), lambda i, j: (0, 0)),
            pl.BlockSpec((1, d), lambda i, j: (0, 0)),
        ],
        out_specs=[
            pl.BlockSpec((tm, d), lambda i, j: (i, 0)),
            pl.BlockSpec((tm, d), lambda i, j: (i, 0)),
        ],
        out_shape=[jax.ShapeDtypeStruct((t, d), F32), jax.ShapeDtypeStruct((t, d), BF16)],
        scratch_shapes=[pltpu.VMEM((tm, d), BF16), pltpu.VMEM((tm, d), F32)],
        compiler_params=_params("parallel", "arbitrary"),
        name="ffn_ln",
    )(x, wg, wu, wd, ln_g.reshape(1, d), ln_b.reshape(1, d))


def _matmul_kernel(x_ref, w_ref, o_ref):
    o_ref[...] = _dot(x_ref[...], w_ref[...])


def _matmul(x, w, tm=1024, tn=512):
    t, k = x.shape
    n = w.shape[1]
    return pl.pallas_call(
        _matmul_kernel,
        grid=(t // tm, n // tn),
        in_specs=[
            pl.BlockSpec((tm, k), lambda i, j: (i, 0)),
            pl.BlockSpec((k, tn), lambda i, j: (0, j)),
        ],
        out_specs=pl.BlockSpec((tm, tn), lambda i, j: (i, j)),
        out_shape=jax.ShapeDtypeStruct((t, n), F32),
        compiler_params=_params("parallel", "arbitrary"),
        name="in_proj",
    )(x, w)


def _proj_ln_kernel(a_ref, w_ref, x_ref, g_ref, b_ref, o_ref, ob_ref, acc_ref):
    j = pl.program_id(1)

    @pl.when(j == 0)
    def _():
        acc_ref[...] = jnp.zeros_like(acc_ref)

    acc_ref[...] += _dot(a_ref[...], w_ref[...])

    @pl.when(j == pl.num_programs(1) - 1)
    def _():
        y = DEEPNORM_ALPHA * x_ref[...] + acc_ref[...]
        y = _layer_norm(y, g_ref[...], b_ref[...])
        o_ref[...] = y
        ob_ref[...] = y.astype(BF16)


def _proj_ln(a, w, x, ln_g, ln_b, tm=512, tk=1024):
    t, k = a.shape
    d = w.shape[1]
    return pl.pallas_call(
        _proj_ln_kernel,
        grid=(t // tm, k // tk),
        in_specs=[
            pl.BlockSpec((tm, tk), lambda i, j: (i, j)),
            pl.BlockSpec((tk, d), lambda i, j: (j, 0)),
            pl.BlockSpec((tm, d), lambda i, j: (i, 0)),
            pl.BlockSpec((1, d), lambda i, j: (0, 0)),
            pl.BlockSpec((1, d), lambda i, j: (0, 0)),
        ],
        out_specs=[
            pl.BlockSpec((tm, d), lambda i, j: (i, 0)),
            pl.BlockSpec((tm, d), lambda i, j: (i, 0)),
        ],
        out_shape=[jax.ShapeDtypeStruct((t, d), F32), jax.ShapeDtypeStruct((t, d), BF16)],
        scratch_shapes=[pltpu.VMEM((tm, d), F32)],
        compiler_params=_params("parallel", "arbitrary"),
        name="proj_ln",
    )(a, w, x, ln_g.reshape(1, d), ln_b.reshape(1, d))


def _rope(x, cos_full, sin_signed):
    return x * cos_full + pltpu.roll(x, HEAD_DIM // 2, 1) * sin_signed


def _moba_kernel(q_ref, k_ref, v_ref, cos_ref, sin_ref, o_ref):
    s = q_ref.shape[0]
    nb = s // MOBA_BLOCK
    blk = MOBA_BLOCK
    cos_full = cos_ref[...]
    sin_signed = sin_ref[...]
    qr = _rope(q_ref[...], cos_full, sin_signed)
    kr = _rope(k_ref[...], cos_full, sin_signed)

    row = lax.broadcasted_iota(jnp.int32, (HEAD_DIM, s), 0)
    col = lax.broadcasted_iota(jnp.int32, (HEAD_DIM, s), 1)
    avg = jnp.where(col // blk == row, 1.0 / blk, 0.0).astype(BF16)
    k_mean = _dot_exact_lhs(avg, kr)
    gate = _dot_hp(qr, k_mean, NT)

    qb = qr.astype(BF16)
    kb = kr.astype(BF16)
    vb = v_ref[...].astype(BF16)
    scale = HEAD_DIM ** -0.5

    lane = lax.broadcasted_iota(jnp.int32, (blk, HEAD_DIM), 1)
    qpos = lax.broadcasted_iota(jnp.int32, (blk, blk), 0)
    kpos = lax.broadcasted_iota(jnp.int32, (blk, blk), 1)
    causal = kpos <= qpos

    for i in range(nb):
        rows = slice(i * blk, (i + 1) * blk)
        qi = qb[rows]
        ranks = []
        if i > MOBA_TOPK:
            gi = gate[rows]
            valid = lane < i
            for n in range(i):
                gn = gi[:, n:n + 1]
                beats = valid & ((gi > gn) | ((gi == gn) & (lane < n)))
                ranks.append(jnp.sum(jnp.where(beats, 1.0, 0.0), axis=-1, keepdims=True))
        scores = []
        for n in range(i + 1):
            sn = _dot(qi, kb[n * blk:(n + 1) * blk], NT) * scale
            if n == i:
                sn = jnp.where(causal, sn, NEG_INF)
            elif ranks:
                selected = jnp.broadcast_to(ranks[n], sn.shape) < float(MOBA_TOPK)
                sn = jnp.where(selected, sn, NEG_INF)
            scores.append(sn)
        m = scores[0].max(axis=-1, keepdims=True)
        for sn in scores[1:]:
            m = jnp.maximum(m, sn.max(axis=-1, keepdims=True))
        l = jnp.zeros((blk, 1), F32)
        acc = jnp.zeros((blk, HEAD_DIM), F32)
        for n, sn in enumerate(scores):
            p = jnp.exp(sn - m)
            l = l + p.sum(axis=-1, keepdims=True)
            acc = acc + _dot(p.astype(BF16), vb[n * blk:(n + 1) * blk])
        o_ref[rows, :] = (acc / l).astype(o_ref.dtype)


def _moba(hcat, cos_full, sin_signed, batch):
    t = hcat.shape[0]
    s = t // batch
    spec = lambda off: pl.BlockSpec((s, HEAD_DIM), lambda b, h: (b, off + h))
    tab = pl.BlockSpec((s, HEAD_DIM), lambda b, h: (0, 0))
    return pl.pallas_call(
        _moba_kernel,
        grid=(batch, A_HEADS),
        in_specs=[spec(0), spec(A_HEADS), spec(2 * A_HEADS), tab, tab],
        out_specs=pl.BlockSpec((s, HEAD_DIM), lambda b, h: (b, h)),
        out_shape=jax.ShapeDtypeStruct((t, A_WIDTH), BF16),
        compiler_params=_params("parallel", "parallel"),
        name="moba",
    )(hcat, hcat, hcat, cos_full, sin_signed)


def _gmlp_kernel(u_ref, v_ref, lng_ref, lnb_ref, ws_ref, bs_ref, o_ref):
    rows = u_ref.shape[0]
    tpos = lax.broadcasted_iota(jnp.int32, (GMLP_CHUNK, GMLP_CHUNK), 0)
    spos = lax.broadcasted_iota(jnp.int32, (GMLP_CHUNK, GMLP_CHUNK), 1)
    causal = spos <= tpos
    for g in range(G_GROUPS):
        cols = slice(g * G_DIM, (g + 1) * G_DIM)
        w = jnp.where(causal, ws_ref[g], 0.0).astype(BF16)
        bias = bs_ref[:, g:g + 1]
        ln_g = lng_ref[:, cols]
        ln_b = lnb_ref[:, cols]
        for c in range(rows // GMLP_CHUNK):
            r = slice(c * GMLP_CHUNK, (c + 1) * GMLP_CHUNK)
            vn = _layer_norm(jax.nn.gelu(v_ref[r, cols]), ln_g, ln_b)
            mixed = _dot(w, vn.astype(BF16)) + bias
            o_ref[r, cols] = (jax.nn.gelu(u_ref[r, cols]) * mixed).astype(o_ref.dtype)


def _gmlp(hcat, ln_g, ln_b, w_s, b_s, rows=512):
    t = hcat.shape[0]
    u_blk = 3 * A_WIDTH // G_WIDTH
    return pl.pallas_call(
        _gmlp_kernel,
        grid=(t // rows,),
        in_specs=[
            pl.BlockSpec((rows, G_WIDTH), lambda i: (i, u_blk)),
            pl.BlockSpec((rows, G_WIDTH), lambda i: (i, u_blk + 1)),
            pl.BlockSpec((1, G_WIDTH), lambda i: (0, 0)),
            pl.BlockSpec((1, G_WIDTH), lambda i: (0, 0)),
            pl.BlockSpec((G_GROUPS, GMLP_CHUNK, GMLP_CHUNK), lambda i: (0, 0, 0)),
            pl.BlockSpec((GMLP_CHUNK, G_GROUPS), lambda i: (0, 0)),
        ],
        out_specs=pl.BlockSpec((rows, G_WIDTH), lambda i: (i, 0)),
        out_shape=jax.ShapeDtypeStruct((t, G_WIDTH), BF16),
        compiler_params=_params("parallel"),
        name="gmlp",
    )(hcat, hcat, ln_g.reshape(1, G_WIDTH), ln_b.reshape(1, G_WIDTH), w_s, b_s.T)


GATE_BLK = 128


def _softplus(x):
    return jnp.maximum(x, 0.0) + jnp.log1p(jnp.exp(-jnp.abs(x)))


def _dn_gate_kernel(x_ref, w_ref, wt_ref, ng_ref, dtb_ref, ngc_ref, dtbc_ref,
                    bg_ref, gc_ref, gct_ref):
    s = x_ref.shape[0]
    hv = DN_V_HEADS
    x = x_ref[...]
    ba = _dot(x, w_ref[...])
    bat = _dot(wt_ref[...], x, NT)
    lane = lax.broadcasted_iota(jnp.int32, (s, 2 * hv), 1)
    g = ng_ref[...] * _softplus(ba + dtb_ref[...])
    bg = jnp.where(lane < hv, jax.nn.sigmoid(ba), g)
    bg_ref[...] = bg
    gt = ngc_ref[...] * _softplus(bat + dtbc_ref[...])

    r = lax.broadcasted_iota(jnp.int32, (GATE_BLK, GATE_BLK), 0)
    c = lax.broadcasted_iota(jnp.int32, (GATE_BLK, GATE_BLK), 1)
    same = (r // DN_CHUNK) == (c // DN_CHUNK)
    lower = jnp.where(same & (c <= r), 1.0, 0.0).astype(BF16)
    upper = jnp.where(same & (r <= c), 1.0, 0.0).astype(BF16)
    for i in range(s // GATE_BLK):
        sl = slice(i * GATE_BLK, (i + 1) * GATE_BLK)
        gc_ref[sl, :] = _dot_exact_lhs(lower, bg[sl, :])
        gct_ref[:, sl] = _dot_exact_rhs(gt[:, sl], upper)


def _dn_gates(xb, w_ba, a_log, dt_bias, batch):
    t, d = xb.shape
    s = t // batch
    hv = DN_V_HEADS
    neg_rate = -jnp.exp(a_log.astype(F32))
    zeros = jnp.zeros((hv,), F32)
    ng = jnp.concatenate([zeros, neg_rate]).reshape(1, 2 * hv)
    dtb = jnp.concatenate([zeros, dt_bias.astype(F32)]).reshape(1, 2 * hv)
    small = lambda shape: pl.BlockSpec(shape, lambda b: (0, 0))
    return pl.pallas_call(
        _dn_gate_kernel,
        grid=(batch,),
        in_specs=[
            pl.BlockSpec((s, d), lambda b: (b, 0)),
            small((d, 2 * hv)),
            small((2 * hv, d)),
            small((1, 2 * hv)),
            small((1, 2 * hv)),
            small((2 * hv, 1)),
            small((2 * hv, 1)),
        ],
        out_specs=[
            pl.BlockSpec((s, 2 * hv), lambda b: (b, 0)),
            pl.BlockSpec((s, 2 * hv), lambda b: (b, 0)),
            pl.BlockSpec((2 * hv, s), lambda b: (0, b)),
        ],
        out_shape=[
            jax.ShapeDtypeStruct((t, 2 * hv), F32),
            jax.ShapeDtypeStruct((t, 2 * hv), F32),
            jax.ShapeDtypeStruct((2 * hv, t), F32),
        ],
        compiler_params=_params("parallel"),
        name="dn_gates",
    )(xb, w_ba, w_ba.T, ng, dtb, ng.reshape(2 * hv, 1), dtb.reshape(2 * hv, 1))


DN_TB = 128
DN_HB = 8
CONV_PAD = 8


def _conv_silu(x_ref, w_ref, buf_ref, first):
    tb = x_ref.shape[0]

    @pl.when(first)
    def _():
        buf_ref[0:CONV_PAD, :] = jnp.zeros((CONV_PAD, buf_ref.shape[1]), F32)

    buf_ref[CONV_PAD:CONV_PAD + tb, :] = x_ref[...]
    y = buf_ref[CONV_PAD:CONV_PAD + tb, :] * w_ref[DN_CONV - 1:DN_CONV, :]
    for j in range(1, DN_CONV):
        tap = DN_CONV - 1 - j
        y = y + buf_ref[CONV_PAD - j:CONV_PAD - j + tb, :] * w_ref[tap:tap + 1, :]
    buf_ref[0:CONV_PAD, :] = buf_ref[tb:tb + CONV_PAD, :]
    return _silu(y)


def _hp_lhs(a_dup, first_half):
    hi = a_dup.astype(BF16).astype(F32)
    sel = jnp.where(first_half, hi, a_dup - hi).astype(BF16)
    return jnp.concatenate([sel, sel], axis=1)


def _hp_rhs(b):
    hi, lo = _split2(b)
    return jnp.concatenate([hi, hi, lo, lo], axis=0)


def _delta_kernel(q_ref, k_ref, v_ref, z_ref, wq_ref, wk_ref, wv_ref, bg_ref, gc_ref, gct_ref,
                  ng_ref, o_ref, state_ref, qbuf_ref, kbuf_ref, vbuf_ref):
    tb = q_ref.shape[0]
    hv = DN_V_HEADS
    c = DN_CHUNK
    dk = DN_HEAD_DIM
    rep = DN_V_HEADS // DN_QK_HEADS
    group = pl.program_id(1)
    first = pl.program_id(2) == 0

    @pl.when(first)
    def _():
        state_ref[...] = jnp.zeros_like(state_ref)

    q_all = _conv_silu(q_ref, wq_ref, qbuf_ref, first)
    k_all = _conv_silu(k_ref, wk_ref, kbuf_ref, first)
    v_all = _conv_silu(v_ref, wv_ref, vbuf_ref, first)

    ri = lax.broadcasted_iota(jnp.int32, (c, 2 * c), 0)
    ci2 = lax.broadcasted_iota(jnp.int32, (c, 2 * c), 1)
    first_half = ci2 < c
    ci = jnp.where(first_half, ci2, ci2 - c)
    incl = ci <= ri
    strict = ci < ri
    eye_dup = jnp.where(ci == ri, 1.0, 0.0)
    norm_g = ng_ref[...]
    n_ch = tb // c
    n_qk = DN_HB // rep

    bg = bg_ref[...]
    gcs = gc_ref[...]
    gct = gct_ref[...]
    gct_swapped = pltpu.roll(gct, c, 1)
    chunk0_lanes = lax.broadcasted_iota(jnp.int32, (hv, tb), 1) < c
    gct_dup = [jnp.where(chunk0_lanes, gct, gct_swapped),
               jnp.where(chunk0_lanes, gct_swapped, gct)]
    head_lane = lax.broadcasted_iota(jnp.int32, (tb, 2 * hv), 1)
    head_row = lax.broadcasted_iota(jnp.int32, (hv, tb), 0)

    q_n, k_n = [], []
    for jq in range(n_qk):
        qcols = slice(jq * dk, (jq + 1) * dk)
        q_h = q_all[:, qcols]
        k_h = k_all[:, qcols]
        q_n.append(q_h * lax.rsqrt(jnp.sum(q_h * q_h, axis=-1, keepdims=True) + RMS_EPS)
                   * (dk ** -0.5))
        k_n.append(k_h * lax.rsqrt(jnp.sum(k_h * k_h, axis=-1, keepdims=True) + RMS_EPS))

    qk_dup, kk_dup = {}, {}
    for jq in range(n_qk):
        for ch in range(n_ch):
            rows = slice(ch * c, (ch + 1) * c)
            q_bf = q_n[jq][rows].astype(BF16)
            k_bf = k_n[jq][rows].astype(BF16)
            g = _dot(jnp.concatenate([q_bf, k_bf], axis=0),
                     jnp.concatenate([k_bf, k_bf], axis=0), NT)
            qk_dup[jq, ch] = g[:c]
            kk_dup[jq, ch] = g[c:]

    pairs = [(hl, ch) for hl in range(DN_HB) for ch in range(n_ch)]
    power, inv, attn_bf, rhs_cat, q_dec, k_dec, g_last = {}, {}, {}, {}, {}, {}, {}
    for hl in range(DN_HB):
        jq = hl // rep
        head = group * DN_HB + hl
        vcols = slice(hl * dk, (hl + 1) * dk)
        beta_col = jnp.sum(jnp.where(head_lane == head, bg, 0.0), axis=-1, keepdims=True)
        gc_col = jnp.sum(jnp.where(head_lane == head + hv, gcs, 0.0), axis=-1, keepdims=True)
        for ch in range(n_ch):
            rows = slice(ch * c, (ch + 1) * c)
            gcr = jnp.sum(jnp.where(head_row == head, gct_dup[ch], 0.0), axis=0, keepdims=True)
            q_c = q_n[jq][rows]
            k_c = k_n[jq][rows]
            beta_c = beta_col[rows]
            gcc = gc_col[rows]
            gl = gcc[c - 1:c, :]
            decay = jnp.exp(jnp.where(incl, gcc - gcr, -jnp.inf))
            a_low = jnp.where(strict, kk_dup[jq, ch] * beta_c * decay, 0.0)
            power[hl, ch] = a_low
            inv[hl, ch] = eye_dup - a_low
            attn_bf[hl, ch] = (qk_dup[jq, ch] * decay)[:, :c].astype(BF16)
            egc = jnp.exp(gcc)
            kb = k_c * beta_c
            rhs_cat[hl, ch] = _hp_rhs(jnp.concatenate(
                [v_all[rows, vcols] * beta_c, kb * egc], axis=1))
            q_dec[hl, ch] = (q_c * egc).astype(BF16)
            k_dec[hl, ch] = (k_c * jnp.exp(gl - gcc)).astype(BF16)
            g_last[hl, ch] = jnp.exp(gl)

    power_rhs = {p: _hp_rhs(power[p]) for p in pairs}
    span = 2
    while span < c:
        for p in pairs:
            power[p] = _dot(_hp_lhs(power[p], first_half), power_rhs[p])
        for p in pairs:
            power_rhs[p] = _hp_rhs(power[p])
        for p in pairs:
            inv[p] = inv[p] + _dot(_hp_lhs(inv[p], first_half), power_rhs[p])
        span *= 2

    u, w_bf = {}, {}
    for p in pairs:
        uw = _dot(_hp_lhs(inv[p], first_half), rhs_cat[p])
        u[p] = uw[:, :dk]
        w_bf[p] = uw[:, dk:].astype(BF16)

    states = [state_ref[hl] for hl in range(DN_HB)]
    for ch in range(n_ch):
        rows = slice(ch * c, (ch + 1) * c)
        for hl in range(DN_HB):
            p = (hl, ch)
            vcols = slice(hl * dk, (hl + 1) * dk)
            state_bf = states[hl].astype(BF16)
            ws_qs = _dot(jnp.concatenate([w_bf[p], q_dec[p]], axis=0), state_bf)
            v_new_bf = (u[p] - ws_qs[:c]).astype(BF16)
            o_c = ws_qs[c:] + _dot(attn_bf[p], v_new_bf)
            states[hl] = states[hl] * g_last[p] + _dot(k_dec[p], v_new_bf, TN)
            z_c = z_ref[rows, vcols]
            o_n = (o_c * lax.rsqrt(jnp.mean(o_c * o_c, axis=-1, keepdims=True) + RMS_EPS)
                   * norm_g * _silu(z_c))
            o_ref[rows, vcols] = o_n.astype(o_ref.dtype)
    for hl in range(DN_HB):
        state_ref[hl] = states[hl]


def _delta(hcat, conv_w, bg, gcs, gct, norm_g, batch):
    t = hcat.shape[0]
    s = t // batch
    nt = s // DN_TB
    rep = DN_V_HEADS // DN_QK_HEADS
    wq = DN_HB // rep * DN_HEAD_DIM
    wv = DN_HB * DN_HEAD_DIM
    k_off = DN_QK_WIDTH // wq
    v_off = 2 * DN_QK_WIDTH // wv
    z_off = DN_CONV_DIM // wv
    hv2 = 2 * DN_V_HEADS
    row = lambda b, g, i: b * nt + i
    return pl.pallas_call(
        _delta_kernel,
        grid=(batch, DN_V_HEADS // DN_HB, nt),
        in_specs=[
            pl.BlockSpec((DN_TB, wq), lambda b, g, i: (row(b, g, i), g)),
            pl.BlockSpec((DN_TB, wq), lambda b, g, i: (row(b, g, i), k_off + g)),
            pl.BlockSpec((DN_TB, wv), lambda b, g, i: (row(b, g, i), v_off + g)),
            pl.BlockSpec((DN_TB, wv), lambda b, g, i: (row(b, g, i), z_off + g)),
            pl.BlockSpec((DN_CONV, wq), lambda b, g, i: (0, g)),
            pl.BlockSpec((DN_CONV, wq), lambda b, g, i: (0, k_off + g)),
            pl.BlockSpec((DN_CONV, wv), lambda b, g, i: (0, v_off + g)),
            pl.BlockSpec((DN_TB, hv2), lambda b, g, i: (row(b, g, i), 0)),
            pl.BlockSpec((DN_TB, hv2), lambda b, g, i: (row(b, g, i), 0)),
            pl.BlockSpec((DN_V_HEADS, DN_TB), lambda b, g, i: (1, row(b, g, i))),
            pl.BlockSpec((1, DN_HEAD_DIM), lambda b, g, i: (0, 0)),
        ],
        out_specs=pl.BlockSpec((DN_TB, wv), lambda b, g, i: (row(b, g, i), g)),
        out_shape=jax.ShapeDtypeStruct((t, DN_V_WIDTH), BF16),
        scratch_shapes=[
            pltpu.VMEM((DN_HB, DN_HEAD_DIM, DN_HEAD_DIM), F32),
            pltpu.VMEM((CONV_PAD + DN_TB, wq), F32),
            pltpu.VMEM((CONV_PAD + DN_TB, wq), F32),
            pltpu.VMEM((CONV_PAD + DN_TB, wv), F32),
        ],
        compiler_params=_params("parallel", "parallel", "arbitrary"),
        name="delta_rule",
    )(hcat, hcat, hcat, hcat, conv_w, conv_w, conv_w, bg, gcs, gct,
      norm_g.reshape(1, DN_HEAD_DIM))


def _rope_tables(s):
    half = HEAD_DIM // 2
    inv = jnp.exp(-math.log(ROPE_THETA) * jnp.arange(half, dtype=F32) * (2.0 / HEAD_DIM))
    ang = jnp.arange(s, dtype=F32)[:, None] * inv[None, :]
    cos, sin = jnp.cos(ang), jnp.sin(ang)
    return jnp.concatenate([cos, cos], axis=-1), jnp.concatenate([-sin, sin], axis=-1)


def kernel(x, ffn1_w_gate, ffn1_w_up, ffn1_w_down, ffn2_w_gate, ffn2_w_up, ffn2_w_down, ln_g, ln_b, ab_w_in, ab_gmlp_ln_g, ab_gmlp_ln_b, ab_gmlp_w_s, ab_gmlp_b_s, ab_w_out, dn_w_in, dn_conv_w, dn_a_log, dn_dt_bias, dn_norm_g, dn_w_out):
    batch, s, d = x.shape
    t = batch * s
    bf = lambda w: w.astype(BF16)
    xf = x.reshape(t, d)
    cos_full, sin_signed = _rope_tables(s)

    for i in range(DEPTH):
        xf, xb = _ffn_ln(xf, bf(ffn1_w_gate[i]), bf(ffn1_w_up[i]), bf(ffn1_w_down[i]),
                         ln_g[i, 0], ln_b[i, 0])
        j = i // 2
        if i % 2 == 0:
            hcat = _matmul(xb, bf(ab_w_in[j]))
            a_out = _moba(hcat, cos_full, sin_signed, batch)
            g_out = _gmlp(hcat, ab_gmlp_ln_g[j], ab_gmlp_ln_b[j], ab_gmlp_w_s[j], ab_gmlp_b_s[j])
            mix = jnp.concatenate([a_out, g_out], axis=-1)
            w_out = bf(ab_w_out[j])
        else:
            w_in = dn_w_in[j]
            hcat = _matmul(xb, bf(w_in[:, :DN_MAIN]))
            bg, gcs, gct = _dn_gates(xb, bf(w_in[:, DN_MAIN:]), dn_a_log[j], dn_dt_bias[j], batch)
            mix = _delta(hcat, dn_conv_w[j], bg, gcs, gct, dn_norm_g[j], batch)
            w_out = bf(dn_w_out[j])
        xf, xb = _proj_ln(mix, w_out, xf, ln_g[i, 1], ln_b[i, 1])
        xf, xb = _ffn_ln(xf, bf(ffn2_w_gate[i]), bf(ffn2_w_up[i]), bf(ffn2_w_down[i]),
                         ln_g[i, 2], ln_b[i, 2])
    return xf.reshape(batch, s, d)
```

```python
import functools
import math

import jax
import jax.numpy as jnp
from jax import lax
from jax.experimental import pallas as pl
from jax.experimental.pallas import tpu as pltpu

F32 = jnp.float32
BF16 = jnp.bfloat16

D_MODEL = 2048
SEQ = 2048
DEPTH = 2
HEAD_DIM = 128
A_HEADS = 8
A_WIDTH = A_HEADS * HEAD_DIM
MOBA_BLOCK = 256
MOBA_TOPK = 3
ROPE_THETA = 10000.0
G_GROUPS = 8
G_DIM = 128
G_WIDTH = G_GROUPS * G_DIM
GMLP_CHUNK = 128
AB_IN = 3 * A_WIDTH + 2 * G_WIDTH
DN_QK_HEADS = 16
DN_V_HEADS = 32
DN_HEAD_DIM = 128
DN_QK_WIDTH = DN_QK_HEADS * DN_HEAD_DIM
DN_V_WIDTH = DN_V_HEADS * DN_HEAD_DIM
DN_CONV_DIM = 2 * DN_QK_WIDTH + DN_V_WIDTH
DN_MAIN = DN_CONV_DIM + DN_V_WIDTH
DN_CONV = 4
DN_CHUNK = 64
DEEPNORM_ALPHA = (2 * DEPTH) ** 0.25
LN_EPS = 1e-5
RMS_EPS = 1e-6
NEG_INF = -1e30

VMEM_LIMIT_BYTES = 60 * 1024 * 1024

NN = (((1,), (0,)), ((), ()))
NT = (((1,), (1,)), ((), ()))
TN = (((0,), (0,)), ((), ()))


def _dot(a, b, dims=NN):
    return lax.dot_general(a, b, dims, preferred_element_type=F32)


def _split2(a):
    hi = a.astype(BF16)
    lo = (a - hi.astype(F32)).astype(BF16)
    return hi, lo


def _dot_hp(a, b, dims=NN):
    a_hi, a_lo = _split2(a)
    b_hi, b_lo = _split2(b)
    return _dot(a_hi, b_hi, dims) + (_dot(a_hi, b_lo, dims) + _dot(a_lo, b_hi, dims))


def _dot_exact_lhs(a_bf16, b, dims=NN):
    b0 = b.astype(BF16)
    r1 = b - b0.astype(F32)
    b1 = r1.astype(BF16)
    b2 = (r1 - b1.astype(F32)).astype(BF16)
    return _dot(a_bf16, b0, dims) + (_dot(a_bf16, b1, dims) + _dot(a_bf16, b2, dims))


def _dot_exact_rhs(a, b_bf16, dims=NN):
    a0 = a.astype(BF16)
    r1 = a - a0.astype(F32)
    a1 = r1.astype(BF16)
    a2 = (r1 - a1.astype(F32)).astype(BF16)
    return _dot(a0, b_bf16, dims) + (_dot(a1, b_bf16, dims) + _dot(a2, b_bf16, dims))


def _layer_norm(y, g, b):
    mu = jnp.mean(y, axis=-1, keepdims=True)
    yc = y - mu
    var = jnp.mean(yc * yc, axis=-1, keepdims=True)
    return yc * lax.rsqrt(var + LN_EPS) * g + b


def _silu(x):
    return x * jax.nn.sigmoid(x)


def _params(*sem):
    return pltpu.CompilerParams(dimension_semantics=sem, vmem_limit_bytes=VMEM_LIMIT_BYTES)


FFN_ROWS = 512


def _ffn_kernel(x_ref, wg_ref, wu_ref, wd_ref, g_ref, b_ref, o_ref, xb_ref):
    j = pl.program_id(1)

    @pl.when(j == 0)
    def _():
        xb_ref[...] = x_ref[...].astype(BF16)
        o_ref[...] = jnp.zeros_like(o_ref)

    for r in range(0, o_ref.shape[0], FFN_ROWS):
        rows = slice(r, r + FFN_ROWS)
        xb = xb_ref[rows, :]
        gate = _dot(xb, wg_ref[...])
        up = _dot(xb, wu_ref[...])
        h = (_silu(gate) * up).astype(BF16)
        o_ref[rows, :] += _dot(h, wd_ref[...])

    @pl.when(j == pl.num_programs(1) - 1)
    def _():
        for r in range(0, o_ref.shape[0], FFN_ROWS):
            rows = slice(r, r + FFN_ROWS)
            y = DEEPNORM_ALPHA * x_ref[rows, :] + 0.5 * o_ref[rows, :]
            o_ref[rows, :] = _layer_norm(y, g_ref[...], b_ref[...])


def _ffn_ln(x, wg, wu, wd, layer, ln_g, ln_b, tm=1024, tf=512):
    t, d = x.shape
    dff = wg.shape[2]
    return pl.pallas_call(
        _ffn_kernel,
        grid=(t // tm, dff // tf),
        in_specs=[
            pl.BlockSpec((tm, d), lambda i, j: (i, 0)),
            pl.BlockSpec((None, d, tf), lambda i, j: (layer, 0, j)),
            pl.BlockSpec((None, d, tf), lambda i, j: (layer, 0, j)),
            pl.BlockSpec((None, tf, d), lambda i, j: (layer, j, 0)),
            pl.BlockSpec((1, d), lambda i, j: (0, 0)),
            pl.BlockSpec((1, d), lambda i, j: (0, 0)),
        ],
        out_specs=pl.BlockSpec((tm, d), lambda i, j: (i, 0)),
        out_shape=jax.ShapeDtypeStruct((t, d), F32),
        scratch_shapes=[pltpu.VMEM((tm, d), BF16)],
        compiler_params=_params("parallel", "arbitrary"),
        name="ffn_ln",
    )(x, wg, wu, wd, ln_g.reshape(1, d), ln_b.reshape(1, d))


def _matmul_kernel(x_ref, w_ref, o_ref, xb_ref):
    @pl.when(pl.program_id(1) == 0)
    def _():
        xb_ref[...] = x_ref[...].astype(BF16)

    o_ref[...] = _dot(xb_ref[...], w_ref[...])


def _matmul(x, w, n, tm=1024, tn=1024):
    t, k = x.shape
    return pl.pallas_call(
        _matmul_kernel,
        grid=(t // tm, n // tn),
        in_specs=[
            pl.BlockSpec((tm, k), lambda i, j: (i, 0)),
            pl.BlockSpec((None, k, tn), lambda i, j: (0, 0, j)),
        ],
        out_specs=pl.BlockSpec((tm, tn), lambda i, j: (i, j)),
        out_shape=jax.ShapeDtypeStruct((t, n), F32),
        scratch_shapes=[pltpu.VMEM((tm, k), BF16)],
        compiler_params=_params("parallel", "arbitrary"),
        name="in_proj",
    )(x, w)


def _proj_ln_kernel(*refs, n_in):
    a_refs = refs[:n_in]
    w_ref, x_ref, g_ref, b_ref, o_ref = refs[n_in:]
    acc = None
    row = 0
    for a_ref in a_refs:
        kc = a_ref.shape[1]
        part = _dot(a_ref[...], w_ref[row:row + kc, :])
        acc = part if acc is None else acc + part
        row += kc
    y = DEEPNORM_ALPHA * x_ref[...] + acc
    o_ref[...] = _layer_norm(y, g_ref[...], b_ref[...])


def _proj_ln(a_parts, w, x, ln_g, ln_b, tm=512):
    t, d = x.shape
    k = w.shape[1]
    in_specs = [pl.BlockSpec((tm, a.shape[1]), lambda i: (i, 0)) for a in a_parts]
    in_specs += [
        pl.BlockSpec((None, k, d), lambda i: (0, 0, 0), pipeline_mode=pl.Buffered(1)),
        pl.BlockSpec((tm, d), lambda i: (i, 0)),
        pl.BlockSpec((1, d), lambda i: (0, 0)),
        pl.BlockSpec((1, d), lambda i: (0, 0)),
    ]
    return pl.pallas_call(
        functools.partial(_proj_ln_kernel, n_in=len(a_parts)),
        grid=(t // tm,),
        in_specs=in_specs,
        out_specs=pl.BlockSpec((tm, d), lambda i: (i, 0)),
        out_shape=jax.ShapeDtypeStruct((t, d), F32),
        compiler_params=_params("parallel"),
        name="proj_ln",
    )(*a_parts, w, x, ln_g.reshape(1, d), ln_b.reshape(1, d))


def _rope(x, cos_full, sin_signed):
    return x * cos_full + pltpu.roll(x, HEAD_DIM // 2, 1) * sin_signed


def _moba_kernel(q_ref, k_ref, v_ref, cos_ref, sin_ref, o_ref):
    s = q_ref.shape[0]
    nb = s // MOBA_BLOCK
    blk = MOBA_BLOCK
    cos_full = cos_ref[...]
    sin_signed = sin_ref[...]
    qr = _rope(q_ref[...], cos_full, sin_signed)
    kr = _rope(k_ref[...], cos_full, sin_signed)

    row = lax.broadcasted_iota(jnp.int32, (HEAD_DIM, s), 0)
    col = lax.broadcasted_iota(jnp.int32, (HEAD_DIM, s), 1)
    avg = jnp.where(col // blk == row, 1.0 / blk, 0.0).astype(BF16)
    k_mean = _dot_exact_lhs(avg, kr)
    gate = _dot_hp(qr, k_mean, NT)

    qb = qr.astype(BF16)
    kb = kr.astype(BF16)
    vb = v_ref[...].astype(BF16)
    scale = HEAD_DIM ** -0.5

    lane = lax.broadcasted_iota(jnp.int32, (blk, HEAD_DIM), 1)
    qpos = lax.broadcasted_iota(jnp.int32, (blk, blk), 0)
    kpos = lax.broadcasted_iota(jnp.int32, (blk, blk), 1)
    causal = kpos <= qpos

    for i in range(nb):
        rows = slice(i * blk, (i + 1) * blk)
        qi = qb[rows]
        ranks = []
        if i > MOBA_TOPK:
            gi = gate[rows]
            valid = lane < i
            for n in range(i):
                gn = gi[:, n:n + 1]
                beats = valid & ((gi > gn) | ((gi == gn) & (lane < n)))
                ranks.append(jnp.sum(jnp.where(beats, 1.0, 0.0), axis=-1, keepdims=True))
        scores = []
        for n in range(i + 1):
            sn = _dot(qi, kb[n * blk:(n + 1) * blk], NT) * scale
            if n == i:
                sn = jnp.where(causal, sn, NEG_INF)
            elif ranks:
                selected = jnp.broadcast_to(ranks[n], sn.shape) < float(MOBA_TOPK)
                sn = jnp.where(selected, sn, NEG_INF)
            scores.append(sn)
        m = scores[0].max(axis=-1, keepdims=True)
        for sn in scores[1:]:
            m = jnp.maximum(m, sn.max(axis=-1, keepdims=True))
        l = jnp.zeros((blk, 1), F32)
        acc = jnp.zeros((blk, HEAD_DIM), F32)
        for n, sn in enumerate(scores):
            p = jnp.exp(sn - m)
            l = l + p.sum(axis=-1, keepdims=True)
            acc = acc + _dot(p.astype(BF16), vb[n * blk:(n + 1) * blk])
        o_ref[rows, :] = (acc / l).astype(o_ref.dtype)


def _moba(hcat, cos_full, sin_signed, batch):
    t = hcat.shape[0]
    s = t // batch
    spec = lambda off: pl.BlockSpec((s, HEAD_DIM), lambda b, h: (b, off + h))
    tab = pl.BlockSpec((s, HEAD_DIM), lambda b, h: (0, 0))
    return pl.pallas_call(
        _moba_kernel,
        grid=(batch, A_HEADS),
        in_specs=[spec(0), spec(A_HEADS), spec(2 * A_HEADS), tab, tab],
        out_specs=pl.BlockSpec((s, HEAD_DIM), lambda b, h: (b, h)),
        out_shape=jax.ShapeDtypeStruct((t, A_WIDTH), BF16),
        compiler_params=_params("parallel", "parallel"),
        name="moba",
    )(hcat, hcat, hcat, cos_full, sin_signed)


def _gmlp_kernel(u_ref, v_ref, lng_ref, lnb_ref, ws_ref, bs_ref, o_ref):
    rows = u_ref.shape[0]
    tpos = lax.broadcasted_iota(jnp.int32, (GMLP_CHUNK, GMLP_CHUNK), 0)
    spos = lax.broadcasted_iota(jnp.int32, (GMLP_CHUNK, GMLP_CHUNK), 1)
    causal = spos <= tpos
    for g in range(G_GROUPS):
        cols = slice(g * G_DIM, (g + 1) * G_DIM)
        w = jnp.where(causal, ws_ref[g], 0.0).astype(BF16)
        bias = bs_ref[:, g:g + 1]
        ln_g = lng_ref[:, cols]
        ln_b = lnb_ref[:, cols]
        for c in range(rows // GMLP_CHUNK):
            r = slice(c * GMLP_CHUNK, (c + 1) * GMLP_CHUNK)
            vn = _layer_norm(jax.nn.gelu(v_ref[r, cols]), ln_g, ln_b)
            mixed = _dot(w, vn.astype(BF16)) + bias
            o_ref[r, cols] = (jax.nn.gelu(u_ref[r, cols]) * mixed).astype(o_ref.dtype)


def _gmlp(hcat, ln_g, ln_b, w_s, b_s, rows=512):
    t = hcat.shape[0]
    u_blk = 3 * A_WIDTH // G_WIDTH
    return pl.pallas_call(
        _gmlp_kernel,
        grid=(t // rows,),
        in_specs=[
            pl.BlockSpec((rows, G_WIDTH), lambda i: (i, u_blk)),
            pl.BlockSpec((rows, G_WIDTH), lambda i: (i, u_blk + 1)),
            pl.BlockSpec((1, G_WIDTH), lambda i: (0, 0)),
            pl.BlockSpec((1, G_WIDTH), lambda i: (0, 0)),
            pl.BlockSpec((G_GROUPS, GMLP_CHUNK, GMLP_CHUNK), lambda i: (0, 0, 0)),
            pl.BlockSpec((GMLP_CHUNK, G_GROUPS), lambda i: (0, 0)),
        ],
        out_specs=pl.BlockSpec((rows, G_WIDTH), lambda i: (i, 0)),
        out_shape=jax.ShapeDtypeStruct((t, G_WIDTH), BF16),
        compiler_params=_params("parallel"),
        name="gmlp",
    )(hcat, hcat, ln_g.reshape(1, G_WIDTH), ln_b.reshape(1, G_WIDTH), w_s, b_s.T)


GATE_BLK = 128


def _softplus(x):
    return jnp.maximum(x, 0.0) + jnp.log1p(jnp.exp(-jnp.abs(x)))


def _dn_gate_kernel(x_ref, w_ref, wt_ref, ng_ref, dtb_ref, ngc_ref, dtbc_ref,
                    bg_ref, gc_ref, gct_ref):
    s = x_ref.shape[0]
    hv = DN_V_HEADS
    x = x_ref[...].astype(BF16)
    ba = _dot(x, w_ref[...])
    bat = _dot(wt_ref[...], x, NT)
    lane = lax.broadcasted_iota(jnp.int32, (s, 2 * hv), 1)
    g = ng_ref[...] * _softplus(ba + dtb_ref[...])
    bg = jnp.where(lane < hv, jax.nn.sigmoid(ba), g)
    bg_ref[...] = bg
    gt = ngc_ref[...] * _softplus(bat + dtbc_ref[...])

    r = lax.broadcasted_iota(jnp.int32, (GATE_BLK, GATE_BLK), 0)
    c = lax.broadcasted_iota(jnp.int32, (GATE_BLK, GATE_BLK), 1)
    same = (r // DN_CHUNK) == (c // DN_CHUNK)
    lower = jnp.where(same & (c <= r), 1.0, 0.0).astype(BF16)
    upper = jnp.where(same & (r <= c), 1.0, 0.0).astype(BF16)
    for i in range(s // GATE_BLK):
        sl = slice(i * GATE_BLK, (i + 1) * GATE_BLK)
        gc_ref[sl, :] = _dot_exact_lhs(lower, bg[sl, :])
        gct_ref[:, sl] = _dot_exact_rhs(gt[:, sl], upper)


def _dn_gates(x, w_ba, a_log, dt_bias, rows=512):
    t, d = x.shape
    s = rows
    hv = DN_V_HEADS
    neg_rate = -jnp.exp(a_log.astype(F32))
    zeros = jnp.zeros((hv,), F32)
    ng = jnp.concatenate([zeros, neg_rate]).reshape(1, 2 * hv)
    dtb = jnp.concatenate([zeros, dt_bias.astype(F32)]).reshape(1, 2 * hv)
    small = lambda shape: pl.BlockSpec(shape, lambda b: (0, 0))
    return pl.pallas_call(
        _dn_gate_kernel,
        grid=(t // rows,),
        in_specs=[
            pl.BlockSpec((s, d), lambda b: (b, 0)),
            small((d, 2 * hv)),
            small((2 * hv, d)),
            small((1, 2 * hv)),
            small((1, 2 * hv)),
            small((2 * hv, 1)),
            small((2 * hv, 1)),
        ],
        out_specs=[
            pl.BlockSpec((s, 2 * hv), lambda b: (b, 0)),
            pl.BlockSpec((s, 2 * hv), lambda b: (b, 0)),
            pl.BlockSpec((2 * hv, s), lambda b: (0, b)),
        ],
        out_shape=[
            jax.ShapeDtypeStruct((t, 2 * hv), F32),
            jax.ShapeDtypeStruct((t, 2 * hv), F32),
            jax.ShapeDtypeStruct((2 * hv, t), F32),
        ],
        compiler_params=_params("parallel"),
        name="dn_gates",
    )(x, w_ba, w_ba.T, ng, dtb, ng.reshape(2 * hv, 1), dtb.reshape(2 * hv, 1))


DN_TB = 128
DN_HB = 8
CONV_PAD = 8


def _conv_silu(x_ref, w_ref, buf_ref, first):
    tb = x_ref.shape[0]

    @pl.when(first)
    def _():
        buf_ref[0:CONV_PAD, :] = jnp.zeros((CONV_PAD, buf_ref.shape[1]), F32)

    buf_ref[CONV_PAD:CONV_PAD + tb, :] = x_ref[...]
    y = buf_ref[CONV_PAD:CONV_PAD + tb, :] * w_ref[DN_CONV - 1:DN_CONV, :]
    for j in range(1, DN_CONV):
        tap = DN_CONV - 1 - j
        y = y + buf_ref[CONV_PAD - j:CONV_PAD - j + tb, :] * w_ref[tap:tap + 1, :]
    buf_ref[0:CONV_PAD, :] = buf_ref[tb:tb + CONV_PAD, :]
    return _silu(y)


def _hp_lhs(a_dup, first_half):
    hi = a_dup.astype(BF16).astype(F32)
    sel = jnp.where(first_half, hi, a_dup - hi).astype(BF16)
    return jnp.concatenate([sel, sel], axis=1)


def _hp_rhs(b):
    hi, lo = _split2(b)
    return jnp.concatenate([hi, hi, lo, lo], axis=0)


def _delta_kernel(q_ref, k_ref, v_ref, z_ref, wq_ref, wk_ref, wv_ref, bg_ref, gc_ref, gct_ref,
                  ng_ref, o_ref, state_ref, qbuf_ref, kbuf_ref, vbuf_ref):
    tb = q_ref.shape[0]
    hv = DN_V_HEADS
    c = DN_CHUNK
    dk = DN_HEAD_DIM
    rep = DN_V_HEADS // DN_QK_HEADS
    group = pl.program_id(1)
    first = pl.program_id(2) == 0

    @pl.when(first)
    def _():
        state_ref[...] = jnp.zeros_like(state_ref)

    q_all = _conv_silu(q_ref, wq_ref, qbuf_ref, first)
    k_all = _conv_silu(k_ref, wk_ref, kbuf_ref, first)
    v_all = _conv_silu(v_ref, wv_ref, vbuf_ref, first)

    ri = lax.broadcasted_iota(jnp.int32, (c, 2 * c), 0)
    ci2 = lax.broadcasted_iota(jnp.int32, (c, 2 * c), 1)
    first_half = ci2 < c
    ci = jnp.where(first_half, ci2, ci2 - c)
    incl = ci <= ri
    strict = ci < ri
    eye_dup = jnp.where(ci == ri, 1.0, 0.0)
    norm_g = ng_ref[...]
    n_ch = tb // c
    n_qk = DN_HB // rep

    bg = bg_ref[...]
    gcs = gc_ref[...]
    gct = gct_ref[...]
    gct_swapped = pltpu.roll(gct, c, 1)
    chunk0_lanes = lax.broadcasted_iota(jnp.int32, (hv, tb), 1) < c
    gct_dup = [jnp.where(chunk0_lanes, gct, gct_swapped),
               jnp.where(chunk0_lanes, gct_swapped, gct)]
    head_lane = lax.broadcasted_iota(jnp.int32, (tb, 2 * hv), 1)
    head_row = lax.broadcasted_iota(jnp.int32, (hv, tb), 0)

    q_n, k_n = [], []
    for jq in range(n_qk):
        qcols = slice(jq * dk, (jq + 1) * dk)
        q_h = q_all[:, qcols]
        k_h = k_all[:, qcols]
        q_n.append(q_h * lax.rsqrt(jnp.sum(q_h * q_h, axis=-1, keepdims=True) + RMS_EPS)
                   * (dk ** -0.5))
        k_n.append(k_h * lax.rsqrt(jnp.sum(k_h * k_h, axis=-1, keepdims=True) + RMS_EPS))

    qk_dup, kk_dup = {}, {}
    for jq in range(n_qk):
        for ch in range(n_ch):
            rows = slice(ch * c, (ch + 1) * c)
            q_bf = q_n[jq][rows].astype(BF16)
            k_bf = k_n[jq][rows].astype(BF16)
            g = _dot(jnp.concatenate([q_bf, k_bf], axis=0),
                     jnp.concatenate([k_bf, k_bf], axis=0), NT)
            qk_dup[jq, ch] = g[:c]
            kk_dup[jq, ch] = g[c:]

    pairs = [(hl, ch) for hl in range(DN_HB) for ch in range(n_ch)]
    power, inv, attn_bf, rhs_cat, q_dec, k_dec, g_last = {}, {}, {}, {}, {}, {}, {}
    for hl in range(DN_HB):
        jq = hl // rep
        head = group * DN_HB + hl
        vcols = slice(hl * dk, (hl + 1) * dk)
        beta_col = jnp.sum(jnp.where(head_lane == head, bg, 0.0), axis=-1, keepdims=True)
        gc_col = jnp.sum(jnp.where(head_lane == head + hv, gcs, 0.0), axis=-1, keepdims=True)
        for ch in range(n_ch):
            rows = slice(ch * c, (ch + 1) * c)
            gcr = jnp.sum(jnp.where(head_row == head, gct_dup[ch], 0.0), axis=0, keepdims=True)
            q_c = q_n[jq][rows]
            k_c = k_n[jq][rows]
            beta_c = beta_col[rows]
            gcc = gc_col[rows]
            gl = gcc[c - 1:c, :]
            decay = jnp.exp(jnp.where(incl, gcc - gcr, -jnp.inf))
            a_low = jnp.where(strict, kk_dup[jq, ch] * beta_c * decay, 0.0)
            power[hl, ch] = a_low
            inv[hl, ch] = eye_dup - a_low
            attn_bf[hl, ch] = (qk_dup[jq, ch] * decay)[:, :c].astype(BF16)
            egc = jnp.exp(gcc)
            kb = k_c * beta_c
            rhs_cat[hl, ch] = _hp_rhs(jnp.concatenate(
                [v_all[rows, vcols] * beta_c, kb * egc], axis=1))
            q_dec[hl, ch] = (q_c * egc).astype(BF16)
            k_dec[hl, ch] = (k_c * jnp.exp(gl - gcc)).astype(BF16)
            g_last[hl, ch] = jnp.exp(gl)

    power_rhs = {p: _hp_rhs(power[p]) for p in pairs}
    span = 2
    while span < c:
        for p in pairs:
            power[p] = _dot(_hp_lhs(power[p], first_half), power_rhs[p])
        for p in pairs:
            power_rhs[p] = _hp_rhs(power[p])
        for p in pairs:
            inv[p] = inv[p] + _dot(_hp_lhs(inv[p], first_half), power_rhs[p])
        span *= 2

    u, w_bf = {}, {}
    for p in pairs:
        uw = _dot(_hp_lhs(inv[p], first_half), rhs_cat[p])
        u[p] = uw[:, :dk]
        w_bf[p] = uw[:, dk:].astype(BF16)

    states = [state_ref[hl] for hl in range(DN_HB)]
    for ch in range(n_ch):
        rows = slice(ch * c, (ch + 1) * c)
        for hl in range(DN_HB):
            p = (hl, ch)
            vcols = slice(hl * dk, (hl + 1) * dk)
            state_bf = states[hl].astype(BF16)
            ws_qs = _dot(jnp.concatenate([w_bf[p], q_dec[p]], axis=0), state_bf)
            v_new_bf = (u[p] - ws_qs[:c]).astype(BF16)
            o_c = ws_qs[c:] + _dot(attn_bf[p], v_new_bf)
            states[hl] = states[hl] * g_last[p] + _dot(k_dec[p], v_new_bf, TN)
            z_c = z_ref[rows, vcols]
            o_n = (o_c * lax.rsqrt(jnp.mean(o_c * o_c, axis=-1, keepdims=True) + RMS_EPS)
                   * norm_g * _silu(z_c))
            o_ref[rows, vcols] = o_n.astype(o_ref.dtype)
    for hl in range(DN_HB):
        state_ref[hl] = states[hl]


def _delta(hcat, conv_w, bg, gcs, gct, norm_g, batch):
    t = hcat.shape[0]
    s = t // batch
    nt = s // DN_TB
    rep = DN_V_HEADS // DN_QK_HEADS
    wq = DN_HB // rep * DN_HEAD_DIM
    wv = DN_HB * DN_HEAD_DIM
    k_off = DN_QK_WIDTH // wq
    v_off = 2 * DN_QK_WIDTH // wv
    z_off = DN_CONV_DIM // wv
    hv2 = 2 * DN_V_HEADS
    row = lambda b, g, i: b * nt + i
    return pl.pallas_call(
        _delta_kernel,
        grid=(batch, DN_V_HEADS // DN_HB, nt),
        in_specs=[
            pl.BlockSpec((DN_TB, wq), lambda b, g, i: (row(b, g, i), g)),
            pl.BlockSpec((DN_TB, wq), lambda b, g, i: (row(b, g, i), k_off + g)),
            pl.BlockSpec((DN_TB, wv), lambda b, g, i: (row(b, g, i), v_off + g)),
            pl.BlockSpec((DN_TB, wv), lambda b, g, i: (row(b, g, i), z_off + g)),
            pl.BlockSpec((DN_CONV, wq), lambda b, g, i: (0, g)),
            pl.BlockSpec((DN_CONV, wq), lambda b, g, i: (0, k_off + g)),
            pl.BlockSpec((DN_CONV, wv), lambda b, g, i: (0, v_off + g)),
            pl.BlockSpec((DN_TB, hv2), lambda b, g, i: (row(b, g, i), 0)),
            pl.BlockSpec((DN_TB, hv2), lambda b, g, i: (row(b, g, i), 0)),
            pl.BlockSpec((DN_V_HEADS, DN_TB), lambda b, g, i: (1, row(b, g, i))),
            pl.BlockSpec((1, DN_HEAD_DIM), lambda b, g, i: (0, 0)),
        ],
        out_specs=pl.BlockSpec((DN_TB, wv), lambda b, g, i: (row(b, g, i), g)),
        out_shape=jax.ShapeDtypeStruct((t, DN_V_WIDTH), BF16),
        scratch_shapes=[
            pltpu.VMEM((DN_HB, DN_HEAD_DIM, DN_HEAD_DIM), F32),
            pltpu.VMEM((CONV_PAD + DN_TB, wq), F32),
            pltpu.VMEM((CONV_PAD + DN_TB, wq), F32),
            pltpu.VMEM((CONV_PAD + DN_TB, wv), F32),
        ],
        compiler_params=_params("parallel", "parallel", "arbitrary"),
        name="delta_rule",
    )(hcat, hcat, hcat, hcat, conv_w, conv_w, conv_w, bg, gcs, gct,
      norm_g.reshape(1, DN_HEAD_DIM))


def _rope_tables(s):
    half = HEAD_DIM // 2
    inv = jnp.exp(-math.log(ROPE_THETA) * jnp.arange(half, dtype=F32) * (2.0 / HEAD_DIM))
    ang = jnp.arange(s, dtype=F32)[:, None] * inv[None, :]
    cos, sin = jnp.cos(ang), jnp.sin(ang)
    return jnp.concatenate([cos, cos], axis=-1), jnp.concatenate([-sin, sin], axis=-1)


def kernel(x, ffn1_w_gate, ffn1_w_up, ffn1_w_down, ffn2_w_gate, ffn2_w_up, ffn2_w_down, ln_g, ln_b, ab_w_in, ab_gmlp_ln_g, ab_gmlp_ln_b, ab_gmlp_w_s, ab_gmlp_b_s, ab_w_out, dn_w_in, dn_conv_w, dn_a_log, dn_dt_bias, dn_norm_g, dn_w_out):
    batch, s, d = x.shape
    t = batch * s
    bf = lambda w: w.astype(BF16)
    xf = x.reshape(t, d)
    cos_full, sin_signed = _rope_tables(s)
    ffn1 = (bf(ffn1_w_gate), bf(ffn1_w_up), bf(ffn1_w_down))
    ffn2 = (bf(ffn2_w_gate), bf(ffn2_w_up), bf(ffn2_w_down))
    ab_in, ab_out, dn_in, dn_out = bf(ab_w_in), bf(ab_w_out), bf(dn_w_in), bf(dn_w_out)

    for i in range(DEPTH):
        xf = _ffn_ln(xf, *ffn1, i, ln_g[i, 0], ln_b[i, 0])
        j = i // 2
        if i % 2 == 0:
            hcat = _matmul(xf, ab_in[j:j + 1], AB_IN)
            a_out = _moba(hcat, cos_full, sin_signed, batch)
            g_out = _gmlp(hcat, ab_gmlp_ln_g[j], ab_gmlp_ln_b[j], ab_gmlp_w_s[j], ab_gmlp_b_s[j])
            mix, w_out = [a_out, g_out], ab_out[j:j + 1]
        else:
            hcat = _matmul(xf, dn_in[j:j + 1], DN_MAIN)
            bg, gcs, gct = _dn_gates(xf, dn_in[j, :, DN_MAIN:], dn_a_log[j], dn_dt_bias[j])
            mix = [_delta(hcat, dn_conv_w[j], bg, gcs, gct, dn_norm_g[j], batch)]
            w_out = dn_out[j:j + 1]
        xf = _proj_ln(mix, w_out, xf, ln_g[i, 1], ln_b[i, 1])
        xf = _ffn_ln(xf, *ffn2, i, ln_g[i, 2], ln_b[i, 2])
    return xf.reshape(batch, s, d)
```

```python
import functools
import math

import jax
import jax.numpy as jnp
from jax import lax
from jax.experimental import pallas as pl
from jax.experimental.pallas import tpu as pltpu

F32 = jnp.float32
BF16 = jnp.bfloat16

D_MODEL = 2048
SEQ = 2048
DEPTH = 2
HEAD_DIM = 128
A_HEADS = 8
A_WIDTH = A_HEADS * HEAD_DIM
MOBA_BLOCK = 256
MOBA_TOPK = 3
ROPE_THETA = 10000.0
G_GROUPS = 8
G_DIM = 128
G_WIDTH = G_GROUPS * G_DIM
GMLP_CHUNK = 128
AB_IN = 3 * A_WIDTH + 2 * G_WIDTH
DN_QK_HEADS = 16
DN_V_HEADS = 32
DN_HEAD_DIM = 128
DN_QK_WIDTH = DN_QK_HEADS * DN_HEAD_DIM
DN_V_WIDTH = DN_V_HEADS * DN_HEAD_DIM
DN_CONV_DIM = 2 * DN_QK_WIDTH + DN_V_WIDTH
DN_MAIN = DN_CONV_DIM + DN_V_WIDTH
DN_CONV = 4
DN_CHUNK = 64
DEEPNORM_ALPHA = (2 * DEPTH) ** 0.25
LN_EPS = 1e-5
RMS_EPS = 1e-6
NEG_INF = -1e30

VMEM_LIMIT_BYTES = 60 * 1024 * 1024

NN = (((1,), (0,)), ((), ()))
NT = (((1,), (1,)), ((), ()))
TN = (((0,), (0,)), ((), ()))


def _dot(a, b, dims=NN):
    return lax.dot_general(a, b, dims, preferred_element_type=F32)


def _split2(a):
    hi = a.astype(BF16)
    lo = (a - hi.astype(F32)).astype(BF16)
    return hi, lo


def _dot_hp(a, b, dims=NN):
    a_hi, a_lo = _split2(a)
    b_hi, b_lo = _split2(b)
    return _dot(a_hi, b_hi, dims) + (_dot(a_hi, b_lo, dims) + _dot(a_lo, b_hi, dims))


def _dot_exact_lhs(a_bf16, b, dims=NN):
    b0 = b.astype(BF16)
    r1 = b - b0.astype(F32)
    b1 = r1.astype(BF16)
    b2 = (r1 - b1.astype(F32)).astype(BF16)
    return _dot(a_bf16, b0, dims) + (_dot(a_bf16, b1, dims) + _dot(a_bf16, b2, dims))


def _dot_exact_rhs(a, b_bf16, dims=NN):
    a0 = a.astype(BF16)
    r1 = a - a0.astype(F32)
    a1 = r1.astype(BF16)
    a2 = (r1 - a1.astype(F32)).astype(BF16)
    return _dot(a0, b_bf16, dims) + (_dot(a1, b_bf16, dims) + _dot(a2, b_bf16, dims))


def _layer_norm(y, g, b):
    mu = jnp.mean(y, axis=-1, keepdims=True)
    yc = y - mu
    var = jnp.mean(yc * yc, axis=-1, keepdims=True)
    return yc * lax.rsqrt(var + LN_EPS) * g + b


def _silu(x):
    return x * jax.nn.sigmoid(x)


def _params(*sem):
    return pltpu.CompilerParams(dimension_semantics=sem, vmem_limit_bytes=VMEM_LIMIT_BYTES)


FFN_ROWS = 512


def _ffn_kernel(x_ref, wg_ref, wu_ref, wd_ref, g_ref, b_ref, o_ref, xb_ref):
    j = pl.program_id(1)

    @pl.when(j == 0)
    def _():
        xb_ref[...] = x_ref[...].astype(BF16)
        o_ref[...] = jnp.zeros_like(o_ref)

    for r in range(0, o_ref.shape[0], FFN_ROWS):
        rows = slice(r, r + FFN_ROWS)
        xb = xb_ref[rows, :]
        gate = _dot(xb, wg_ref[...])
        up = _dot(xb, wu_ref[...])
        h = (_silu(gate) * up).astype(BF16)
        o_ref[rows, :] += _dot(h, wd_ref[...])

    @pl.when(j == pl.num_programs(1) - 1)
    def _():
        for r in range(0, o_ref.shape[0], FFN_ROWS):
            rows = slice(r, r + FFN_ROWS)
            y = DEEPNORM_ALPHA * x_ref[rows, :] + 0.5 * o_ref[rows, :]
            o_ref[rows, :] = _layer_norm(y, g_ref[...], b_ref[...])


def _ffn_ln(x, wg, wu, wd, layer, ln_g, ln_b, tm=1024, tf=512):
    t, d = x.shape
    dff = wg.shape[2]
    return pl.pallas_call(
        _ffn_kernel,
        grid=(t // tm, dff // tf),
        in_specs=[
            pl.BlockSpec((tm, d), lambda i, j: (i, 0)),
            pl.BlockSpec((None, d, tf), lambda i, j: (layer, 0, j)),
            pl.BlockSpec((None, d, tf), lambda i, j: (layer, 0, j)),
            pl.BlockSpec((None, tf, d), lambda i, j: (layer, j, 0)),
            pl.BlockSpec((1, d), lambda i, j: (0, 0)),
            pl.BlockSpec((1, d), lambda i, j: (0, 0)),
        ],
        out_specs=pl.BlockSpec((tm, d), lambda i, j: (i, 0)),
        out_shape=jax.ShapeDtypeStruct((t, d), F32),
        scratch_shapes=[pltpu.VMEM((tm, d), BF16)],
        compiler_params=_params("parallel", "arbitrary"),
        name="ffn_ln",
    )(x, wg, wu, wd, ln_g.reshape(1, d), ln_b.reshape(1, d))


def _matmul_kernel(x_ref, w_ref, o_ref, xb_ref):
    @pl.when(pl.program_id(1) == 0)
    def _():
        xb_ref[...] = x_ref[...].astype(BF16)

    o_ref[...] = _dot(xb_ref[...], w_ref[...])


def _matmul(x, w, n, tm=1024, tn=1024):
    t, k = x.shape
    return pl.pallas_call(
        _matmul_kernel,
        grid=(t // tm, n // tn),
        in_specs=[
            pl.BlockSpec((tm, k), lambda i, j: (i, 0)),
            pl.BlockSpec((None, k, tn), lambda i, j: (0, 0, j)),
        ],
        out_specs=pl.BlockSpec((tm, tn), lambda i, j: (i, j)),
        out_shape=jax.ShapeDtypeStruct((t, n), F32),
        scratch_shapes=[pltpu.VMEM((tm, k), BF16)],
        compiler_params=_params("parallel", "arbitrary"),
        name="in_proj",
    )(x, w)


def _proj_ln_kernel(*refs, n_in):
    a_refs = refs[:n_in]
    w_ref, x_ref, g_ref, b_ref, o_ref = refs[n_in:]
    acc = None
    row = 0
    for a_ref in a_refs:
        kc = a_ref.shape[1]
        part = _dot(a_ref[...], w_ref[row:row + kc, :])
        acc = part if acc is None else acc + part
        row += kc
    y = DEEPNORM_ALPHA * x_ref[...] + acc
    o_ref[...] = _layer_norm(y, g_ref[...], b_ref[...])


def _proj_ln(a_parts, w, x, ln_g, ln_b, tm=512):
    t, d = x.shape
    k = w.shape[1]
    in_specs = [pl.BlockSpec((tm, a.shape[1]), lambda i: (i, 0)) for a in a_parts]
    in_specs += [
        pl.BlockSpec((None, k, d), lambda i: (0, 0, 0), pipeline_mode=pl.Buffered(1)),
        pl.BlockSpec((tm, d), lambda i: (i, 0)),
        pl.BlockSpec((1, d), lambda i: (0, 0)),
        pl.BlockSpec((1, d), lambda i: (0, 0)),
    ]
    return pl.pallas_call(
        functools.partial(_proj_ln_kernel, n_in=len(a_parts)),
        grid=(t // tm,),
        in_specs=in_specs,
        out_specs=pl.BlockSpec((tm, d), lambda i: (i, 0)),
        out_shape=jax.ShapeDtypeStruct((t, d), F32),
        compiler_params=_params("parallel"),
        name="proj_ln",
    )(*a_parts, w, x, ln_g.reshape(1, d), ln_b.reshape(1, d))


def _rope(x, cos_full, sin_signed):
    return x * cos_full + pltpu.roll(x, HEAD_DIM // 2, 1) * sin_signed


def _moba_kernel(q_ref, k_ref, v_ref, cos_ref, sin_ref, o_ref):
    s = q_ref.shape[0]
    nb = s // MOBA_BLOCK
    blk = MOBA_BLOCK
    cos_full = cos_ref[...]
    sin_signed = sin_ref[...]
    qr = _rope(q_ref[...], cos_full, sin_signed)
    kr = _rope(k_ref[...], cos_full, sin_signed)

    row = lax.broadcasted_iota(jnp.int32, (HEAD_DIM, s), 0)
    col = lax.broadcasted_iota(jnp.int32, (HEAD_DIM, s), 1)
    avg = jnp.where(col // blk == row, 1.0 / blk, 0.0).astype(BF16)
    k_mean = _dot_exact_lhs(avg, kr)
    gate = _dot_hp(qr, k_mean, NT)

    qb = qr.astype(BF16)
    kb = kr.astype(BF16)
    vb = v_ref[...].astype(BF16)
    scale = HEAD_DIM ** -0.5

    lane = lax.broadcasted_iota(jnp.int32, (blk, HEAD_DIM), 1)
    qpos = lax.broadcasted_iota(jnp.int32, (blk, blk), 0)
    kpos = lax.broadcasted_iota(jnp.int32, (blk, blk), 1)
    causal = kpos <= qpos

    for i in range(nb):
        rows = slice(i * blk, (i + 1) * blk)
        qi = qb[rows]
        ranks = []
        if i > MOBA_TOPK:
            gi = gate[rows]
            valid = lane < i
            for n in range(i):
                gn = gi[:, n:n + 1]
                beats = valid & ((gi > gn) | ((gi == gn) & (lane < n)))
                ranks.append(jnp.sum(jnp.where(beats, 1.0, 0.0), axis=-1, keepdims=True))
        scores = []
        for n in range(i + 1):
            sn = _dot(qi, kb[n * blk:(n + 1) * blk], NT) * scale
            if n == i:
                sn = jnp.where(causal, sn, NEG_INF)
            elif ranks:
                selected = jnp.broadcast_to(ranks[n], sn.shape) < float(MOBA_TOPK)
                sn = jnp.where(selected, sn, NEG_INF)
            scores.append(sn)
        m = scores[0].max(axis=-1, keepdims=True)
        for sn in scores[1:]:
            m = jnp.maximum(m, sn.max(axis=-1, keepdims=True))
        l = jnp.zeros((blk, 1), F32)
        acc = jnp.zeros((blk, HEAD_DIM), F32)
        for n, sn in enumerate(scores):
            p = jnp.exp(sn - m)
            l = l + p.sum(axis=-1, keepdims=True)
            acc = acc + _dot(p.astype(BF16), vb[n * blk:(n + 1) * blk])
        o_ref[rows, :] = (acc / l).astype(o_ref.dtype)


def _moba(hcat, cos_full, sin_signed, batch):
    t = hcat.shape[0]
    s = t // batch
    spec = lambda off: pl.BlockSpec((s, HEAD_DIM), lambda b, h: (b, off + h))
    tab = pl.BlockSpec((s, HEAD_DIM), lambda b, h: (0, 0))
    return pl.pallas_call(
        _moba_kernel,
        grid=(batch, A_HEADS),
        in_specs=[spec(0), spec(A_HEADS), spec(2 * A_HEADS), tab, tab],
        out_specs=pl.BlockSpec((s, HEAD_DIM), lambda b, h: (b, h)),
        out_shape=jax.ShapeDtypeStruct((t, A_WIDTH), BF16),
        compiler_params=_params("parallel", "parallel"),
        name="moba",
    )(hcat, hcat, hcat, cos_full, sin_signed)


def _gmlp_kernel(u_ref, v_ref, lng_ref, lnb_ref, ws_ref, bs_ref, o_ref):
    rows = u_ref.shape[0]
    tpos = lax.broadcasted_iota(jnp.int32, (GMLP_CHUNK, GMLP_CHUNK), 0)
    spos = lax.broadcasted_iota(jnp.int32, (GMLP_CHUNK, GMLP_CHUNK), 1)
    causal = spos <= tpos
    for g in range(G_GROUPS):
        cols = slice(g * G_DIM, (g + 1) * G_DIM)
        w = jnp.where(causal, ws_ref[g], 0.0).astype(BF16)
        bias = bs_ref[:, g:g + 1]
        ln_g = lng_ref[:, cols]
        ln_b = lnb_ref[:, cols]
        for c in range(rows // GMLP_CHUNK):
            r = slice(c * GMLP_CHUNK, (c + 1) * GMLP_CHUNK)
            vn = _layer_norm(jax.nn.gelu(v_ref[r, cols]), ln_g, ln_b)
            mixed = _dot(w, vn.astype(BF16)) + bias
            o_ref[r, cols] = (jax.nn.gelu(u_ref[r, cols]) * mixed).astype(o_ref.dtype)


def _gmlp(hcat, ln_g, ln_b, w_s, b_s, rows=512):
    t = hcat.shape[0]
    u_blk = 3 * A_WIDTH // G_WIDTH
    return pl.pallas_call(
        _gmlp_kernel,
        grid=(t // rows,),
        in_specs=[
            pl.BlockSpec((rows, G_WIDTH), lambda i: (i, u_blk)),
            pl.BlockSpec((rows, G_WIDTH), lambda i: (i, u_blk + 1)),
            pl.BlockSpec((1, G_WIDTH), lambda i: (0, 0)),
            pl.BlockSpec((1, G_WIDTH), lambda i: (0, 0)),
            pl.BlockSpec((G_GROUPS, GMLP_CHUNK, GMLP_CHUNK), lambda i: (0, 0, 0)),
            pl.BlockSpec((GMLP_CHUNK, G_GROUPS), lambda i: (0, 0)),
        ],
        out_specs=pl.BlockSpec((rows, G_WIDTH), lambda i: (i, 0)),
        out_shape=jax.ShapeDtypeStruct((t, G_WIDTH), BF16),
        compiler_params=_params("parallel"),
        name="gmlp",
    )(hcat, hcat, ln_g.reshape(1, G_WIDTH), ln_b.reshape(1, G_WIDTH), w_s, b_s.T)


GATE_BLK = 128


def _softplus(x):
    return jnp.maximum(x, 0.0) + jnp.log1p(jnp.exp(-jnp.abs(x)))


def _dn_gate_kernel(x_ref, w_ref, wt_ref, ng_ref, dtb_ref, ngc_ref, dtbc_ref,
                    bg_ref, gc_ref, gct_ref):
    s = x_ref.shape[0]
    hv = DN_V_HEADS
    x = x_ref[...].astype(BF16)
    ba = _dot(x, w_ref[...])
    bat = _dot(wt_ref[...], x, NT)
    lane = lax.broadcasted_iota(jnp.int32, (s, 2 * hv), 1)
    g = ng_ref[...] * _softplus(ba + dtb_ref[...])
    bg = jnp.where(lane < hv, jax.nn.sigmoid(ba), g)
    bg_ref[...] = bg
    gt = ngc_ref[...] * _softplus(bat + dtbc_ref[...])

    r = lax.broadcasted_iota(jnp.int32, (GATE_BLK, GATE_BLK), 0)
    c = lax.broadcasted_iota(jnp.int32, (GATE_BLK, GATE_BLK), 1)
    same = (r // DN_CHUNK) == (c // DN_CHUNK)
    lower = jnp.where(same & (c <= r), 1.0, 0.0).astype(BF16)
    upper = jnp.where(same & (r <= c), 1.0, 0.0).astype(BF16)
    for i in range(s // GATE_BLK):
        sl = slice(i * GATE_BLK, (i + 1) * GATE_BLK)
        gc_ref[sl, :] = _dot_exact_lhs(lower, bg[sl, :])
        gct_ref[:, sl] = _dot_exact_rhs(gt[:, sl], upper)


def _dn_gates(x, w_ba, a_log, dt_bias, rows=512):
    t, d = x.shape
    s = rows
    hv = DN_V_HEADS
    neg_rate = -jnp.exp(a_log.astype(F32))
    zeros = jnp.zeros((hv,), F32)
    ng = jnp.concatenate([zeros, neg_rate]).reshape(1, 2 * hv)
    dtb = jnp.concatenate([zeros, dt_bias.astype(F32)]).reshape(1, 2 * hv)
    small = lambda shape: pl.BlockSpec(shape, lambda b: (0, 0))
    return pl.pallas_call(
        _dn_gate_kernel,
        grid=(t // rows,),
        in_specs=[
            pl.BlockSpec((s, d), lambda b: (b, 0)),
            small((d, 2 * hv)),
            small((2 * hv, d)),
            small((1, 2 * hv)),
            small((1, 2 * hv)),
            small((2 * hv, 1)),
            small((2 * hv, 1)),
        ],
        out_specs=[
            pl.BlockSpec((s, 2 * hv), lambda b: (b, 0)),
            pl.BlockSpec((s, 2 * hv), lambda b: (b, 0)),
            pl.BlockSpec((2 * hv, s), lambda b: (0, b)),
        ],
        out_shape=[
            jax.ShapeDtypeStruct((t, 2 * hv), F32),
            jax.ShapeDtypeStruct((t, 2 * hv), F32),
            jax.ShapeDtypeStruct((2 * hv, t), F32),
        ],
        compiler_params=_params("parallel"),
        name="dn_gates",
    )(x, w_ba, w_ba.T, ng, dtb, ng.reshape(2 * hv, 1), dtb.reshape(2 * hv, 1))


DN_TB = 128
DN_HB = 8
CONV_PAD = 8


def _conv_silu(x_ref, w_ref, buf_ref, first):
    tb = x_ref.shape[0]

    @pl.when(first)
    def _():
        buf_ref[0:CONV_PAD, :] = jnp.zeros((CONV_PAD, buf_ref.shape[1]), F32)

    buf_ref[CONV_PAD:CONV_PAD + tb, :] = x_ref[...]
    y = buf_ref[CONV_PAD:CONV_PAD + tb, :] * w_ref[DN_CONV - 1:DN_CONV, :]
    for j in range(1, DN_CONV):
        tap = DN_CONV - 1 - j
        y = y + buf_ref[CONV_PAD - j:CONV_PAD - j + tb, :] * w_ref[tap:tap + 1, :]
    buf_ref[0:CONV_PAD, :] = buf_ref[tb:tb + CONV_PAD, :]
    return _silu(y)


def _hi_lo(a):
    hi = a.astype(BF16).astype(F32)
    return hi, a - hi


def _pair_lhs(x):
    hi, lo = _hi_lo(x)
    return jnp.concatenate([hi, lo], axis=1).astype(BF16)


def _pair_blockdiag(y, first_half):
    top = jnp.where(first_half, y, 0.0)
    return jnp.concatenate([top, y - top], axis=0)


def _pair_product(x, y, first_half):
    width = x.shape[1]
    y_hi, y_lo = _hi_lo(y)
    bd = jnp.concatenate([_pair_blockdiag(y_hi, first_half),
                          _pair_blockdiag(y_lo, first_half)], axis=1).astype(BF16)
    out = _dot(_pair_lhs(x), jnp.concatenate([bd, bd], axis=0))
    return out[:, :width] + out[:, width:]


def _delta_kernel(q_ref, k_ref, v_ref, z_ref, wq_ref, wk_ref, wv_ref, bg_ref, gc_ref, gct_ref,
                  ng_ref, o_ref, state_ref, qbuf_ref, kbuf_ref, vbuf_ref):
    tb = q_ref.shape[0]
    hv = DN_V_HEADS
    c = DN_CHUNK
    dk = DN_HEAD_DIM
    rep = DN_V_HEADS // DN_QK_HEADS
    group = pl.program_id(1)
    first = pl.program_id(2) == 0

    @pl.when(first)
    def _():
        state_ref[...] = jnp.zeros_like(state_ref)

    q_all = _conv_silu(q_ref, wq_ref, qbuf_ref, first)
    k_all = _conv_silu(k_ref, wk_ref, kbuf_ref, first)
    v_all = _conv_silu(v_ref, wv_ref, vbuf_ref, first)

    assert tb == 2 * c
    ri = lax.broadcasted_iota(jnp.int32, (c, 2 * c), 0)
    ci2 = lax.broadcasted_iota(jnp.int32, (c, 2 * c), 1)
    first_half = ci2 < c
    ci = jnp.where(first_half, ci2, ci2 - c)
    incl = ci <= ri
    strict = ci < ri
    eye = jnp.where(ci == ri, 1.0, 0.0)
    norm_g = ng_ref[...]
    n_qk = DN_HB // rep
    heads = range(DN_HB)

    bg = bg_ref[...]
    gcs = gc_ref[...]
    gct = gct_ref[...]
    head_lane = lax.broadcasted_iota(jnp.int32, (tb, 2 * hv), 1)
    head_row = lax.broadcasted_iota(jnp.int32, (hv, tb), 0)

    q_n, k_n = [], []
    for jq in range(n_qk):
        qcols = slice(jq * dk, (jq + 1) * dk)
        q_h = q_all[:, qcols]
        k_h = k_all[:, qcols]
        q_n.append(q_h * lax.rsqrt(jnp.sum(q_h * q_h, axis=-1, keepdims=True) + RMS_EPS)
                   * (dk ** -0.5))
        k_n.append(k_h * lax.rsqrt(jnp.sum(k_h * k_h, axis=-1, keepdims=True) + RMS_EPS))

    qk_pair, kk_pair = [], []
    for jq in range(n_qk):
        q0, q1 = q_n[jq][:c], q_n[jq][c:]
        k0, k1 = k_n[jq][:c], k_n[jq][c:]
        zero = jnp.zeros_like(k0)
        lhs = jnp.concatenate([jnp.concatenate([q0, q1], axis=1),
                               jnp.concatenate([k0, k1], axis=1)], axis=0).astype(BF16)
        rhs = jnp.concatenate([jnp.concatenate([k0, zero], axis=1),
                               jnp.concatenate([zero, k1], axis=1)], axis=0).astype(BF16)
        g = _dot(lhs, rhs, NT)
        qk_pair.append(g[:c])
        kk_pair.append(g[c:])

    a_pair, attn_bf, rhs_bf, q_dec, k_dec, g_last = [], [], [], [], [], []
    for hl in heads:
        jq = hl // rep
        head = group * DN_HB + hl
        vcols = slice(hl * dk, (hl + 1) * dk)
        beta_col = jnp.sum(jnp.where(head_lane == head, bg, 0.0), axis=-1, keepdims=True)
        gc_col = jnp.sum(jnp.where(head_lane == head + hv, gcs, 0.0), axis=-1, keepdims=True)
        gc_row = jnp.sum(jnp.where(head_row == head, gct, 0.0), axis=0, keepdims=True)
        beta_sel = jnp.where(first_half, beta_col[:c], beta_col[c:])
        gc_sel = jnp.where(first_half, gc_col[:c], gc_col[c:])
        decay = jnp.exp(jnp.where(incl, gc_sel - gc_row, -jnp.inf))
        a_pair.append(jnp.where(strict, kk_pair[jq] * beta_sel * decay, 0.0))
        attn_bf.append((qk_pair[jq] * decay).astype(BF16))
        egc = jnp.exp(gc_col)
        kb = k_n[jq] * beta_col
        rhs_bf.append(jnp.concatenate([v_all[:, vcols] * beta_col, kb * egc],
                                      axis=1).astype(BF16))
        q_dec.append((q_n[jq] * egc).astype(BF16))
        gl = [gc_col[c - 1:c, :], gc_col[tb - 1:tb, :]]
        gl_rows = jnp.concatenate([jnp.broadcast_to(gl[0], (c, 1)),
                                   jnp.broadcast_to(gl[1], (c, 1))], axis=0)
        k_dec.append((k_n[jq] * jnp.exp(gl_rows - gc_col)).astype(BF16))
        g_last.append([jnp.exp(gl[0]), jnp.exp(gl[1])])

    inv = [eye - jnp.where((ri // 2) == (ci // 2), a, 0.0) for a in a_pair]
    s = 2
    while s < c:
        off = ((ri // (2 * s)) == (ci // (2 * s))) & ((ri // s) != (ci // s))
        m1 = [_pair_product(jnp.where(off, a_pair[hl], 0.0), inv[hl], first_half)
              for hl in heads]
        m2 = [_pair_product(inv[hl], m1[hl], first_half) for hl in heads]
        inv = [inv[hl] - m2[hl] for hl in heads]
        s *= 2

    u, w_bf = [], []
    zero_rows = jnp.zeros((c, 2 * dk), BF16)
    for hl in heads:
        lhs = _pair_lhs(inv[hl])
        r0, r1 = rhs_bf[hl][:c], rhs_bf[hl][c:]
        uw0 = _dot(lhs, jnp.concatenate([r0, zero_rows, r0, zero_rows], axis=0))
        uw1 = _dot(lhs, jnp.concatenate([zero_rows, r1, zero_rows, r1], axis=0))
        u.append([uw0[:, :dk], uw1[:, :dk]])
        w_bf.append([uw0[:, dk:].astype(BF16), uw1[:, dk:].astype(BF16)])

    states = [state_ref[hl] for hl in heads]
    zero_v = jnp.zeros((c, dk), BF16)
    for ch in range(2):
        rows = slice(ch * c, (ch + 1) * c)
        ws_qs = [_dot(jnp.concatenate([w_bf[hl][ch], q_dec[hl][rows]], axis=0),
                      states[hl].astype(BF16)) for hl in heads]
        v_new_bf = [(u[hl][ch] - ws_qs[hl][:c]).astype(BF16) for hl in heads]
        o_intra = []
        for hl in heads:
            v_rows = [v_new_bf[hl], zero_v] if ch == 0 else [zero_v, v_new_bf[hl]]
            o_intra.append(_dot(attn_bf[hl], jnp.concatenate(v_rows, axis=0)))
        states = [states[hl] * g_last[hl][ch] + _dot(k_dec[hl][rows], v_new_bf[hl], TN)
                  for hl in heads]
        for hl in heads:
            vcols = slice(hl * dk, (hl + 1) * dk)
            o_c = ws_qs[hl][c:] + o_intra[hl]
            z_c = z_ref[rows, vcols]
            o_n = (o_c * lax.rsqrt(jnp.mean(o_c * o_c, axis=-1, keepdims=True) + RMS_EPS)
                   * norm_g * _silu(z_c))
            o_ref[rows, vcols] = o_n.astype(o_ref.dtype)
    for hl in heads:
        state_ref[hl] = states[hl]


def _delta(hcat, conv_w, bg, gcs, gct, norm_g, batch):
    t = hcat.shape[0]
    s = t // batch
    nt = s // DN_TB
    rep = DN_V_HEADS // DN_QK_HEADS
    wq = DN_HB // rep * DN_HEAD_DIM
    wv = DN_HB * DN_HEAD_DIM
    k_off = DN_QK_WIDTH // wq
    v_off = 2 * DN_QK_WIDTH // wv
    z_off = DN_CONV_DIM // wv
    hv2 = 2 * DN_V_HEADS
    row = lambda b, g, i: b * nt + i
    return pl.pallas_call(
        _delta_kernel,
        grid=(batch, DN_V_HEADS // DN_HB, nt),
        in_specs=[
            pl.BlockSpec((DN_TB, wq), lambda b, g, i: (row(b, g, i), g)),
            pl.BlockSpec((DN_TB, wq), lambda b, g, i: (row(b, g, i), k_off + g)),
            pl.BlockSpec((DN_TB, wv), lambda b, g, i: (row(b, g, i), v_off + g)),
            pl.BlockSpec((DN_TB, wv), lambda b, g, i: (row(b, g, i), z_off + g)),
            pl.BlockSpec((DN_CONV, wq), lambda b, g, i: (0, g)),
            pl.BlockSpec((DN_CONV, wq), lambda b, g, i: (0, k_off + g)),
            pl.BlockSpec((DN_CONV, wv), lambda b, g, i: (0, v_off + g)),
            pl.BlockSpec((DN_TB, hv2), lambda b, g, i: (row(b, g, i), 0)),
            pl.BlockSpec((DN_TB, hv2), lambda b, g, i: (row(b, g, i), 0)),
            pl.BlockSpec((DN_V_HEADS, DN_TB), lambda b, g, i: (1, row(b, g, i))),
            pl.BlockSpec((1, DN_HEAD_DIM), lambda b, g, i: (0, 0)),
        ],
        out_specs=pl.BlockSpec((DN_TB, wv), lambda b, g, i: (row(b, g, i), g)),
        out_shape=jax.ShapeDtypeStruct((t, DN_V_WIDTH), BF16),
        scratch_shapes=[
            pltpu.VMEM((DN_HB, DN_HEAD_DIM, DN_HEAD_DIM), F32),
            pltpu.VMEM((CONV_PAD + DN_TB, wq), F32),
            pltpu.VMEM((CONV_PAD + DN_TB, wq), F32),
            pltpu.VMEM((CONV_PAD + DN_TB, wv), F32),
        ],
        compiler_params=_params("parallel", "parallel", "arbitrary"),
        name="delta_rule",
    )(hcat, hcat, hcat, hcat, conv_w, conv_w, conv_w, bg, gcs, gct,
      norm_g.reshape(1, DN_HEAD_DIM))


def _rope_tables(s):
    half = HEAD_DIM // 2
    inv = jnp.exp(-math.log(ROPE_THETA) * jnp.arange(half, dtype=F32) * (2.0 / HEAD_DIM))
    ang = jnp.arange(s, dtype=F32)[:, None] * inv[None, :]
    cos, sin = jnp.cos(ang), jnp.sin(ang)
    return jnp.concatenate([cos, cos], axis=-1), jnp.concatenate([-sin, sin], axis=-1)


def kernel(x, ffn1_w_gate, ffn1_w_up, ffn1_w_down, ffn2_w_gate, ffn2_w_up, ffn2_w_down, ln_g, ln_b, ab_w_in, ab_gmlp_ln_g, ab_gmlp_ln_b, ab_gmlp_w_s, ab_gmlp_b_s, ab_w_out, dn_w_in, dn_conv_w, dn_a_log, dn_dt_bias, dn_norm_g, dn_w_out):
    batch, s, d = x.shape
    t = batch * s
    bf = lambda w: w.astype(BF16)
    xf = x.reshape(t, d)
    cos_full, sin_signed = _rope_tables(s)
    ffn1 = (bf(ffn1_w_gate), bf(ffn1_w_up), bf(ffn1_w_down))
    ffn2 = (bf(ffn2_w_gate), bf(ffn2_w_up), bf(ffn2_w_down))
    ab_in, ab_out, dn_out = bf(ab_w_in), bf(ab_w_out), bf(dn_w_out)
    dn_in, dn_in_gates = bf(dn_w_in[:, :, :DN_MAIN]), bf(dn_w_in[:, :, DN_MAIN:])

    for i in range(DEPTH):
        xf = _ffn_ln(xf, *ffn1, i, ln_g[i, 0], ln_b[i, 0])
        j = i // 2
        if i % 2 == 0:
            hcat = _matmul(xf, ab_in[j:j + 1], AB_IN)
            a_out = _moba(hcat, cos_full, sin_signed, batch)
            g_out = _gmlp(hcat, ab_gmlp_ln_g[j], ab_gmlp_ln_b[j], ab_gmlp_w_s[j], ab_gmlp_b_s[j])
            mix, w_out = [a_out, g_out], ab_out[j:j + 1]
        else:
            hcat = _matmul(xf, dn_in[j:j + 1], DN_MAIN)
            bg, gcs, gct = _dn_gates(xf, dn_in_gates[j], dn_a_log[j], dn_dt_bias[j])
            mix = [_delta(hcat, dn_conv_w[j], bg, gcs, gct, dn_norm_g[j], batch)]
            w_out = dn_out[j:j + 1]
        xf = _proj_ln(mix, w_out, xf, ln_g[i, 1], ln_b[i, 1])
        xf = _ffn_ln(xf, *ffn2, i, ln_g[i, 2], ln_b[i, 2])
    return xf.reshape(batch, s, d)
```

```python
import functools
import math

import jax
import jax.numpy as jnp
from jax import lax
from jax.experimental import pallas as pl
from jax.experimental.pallas import tpu as pltpu

F32 = jnp.float32
BF16 = jnp.bfloat16

D_MODEL = 2048
SEQ = 2048
DEPTH = 2
HEAD_DIM = 128
A_HEADS = 8
A_WIDTH = A_HEADS * HEAD_DIM
MOBA_BLOCK = 256
MOBA_TOPK = 3
ROPE_THETA = 10000.0
G_GROUPS = 8
G_DIM = 128
G_WIDTH = G_GROUPS * G_DIM
GMLP_CHUNK = 128
AB_IN = 3 * A_WIDTH + 2 * G_WIDTH
DN_QK_HEADS = 16
DN_V_HEADS = 32
DN_HEAD_DIM = 128
DN_QK_WIDTH = DN_QK_HEADS * DN_HEAD_DIM
DN_V_WIDTH = DN_V_HEADS * DN_HEAD_DIM
DN_CONV_DIM = 2 * DN_QK_WIDTH + DN_V_WIDTH
DN_MAIN = DN_CONV_DIM + DN_V_WIDTH
DN_CONV = 4
DN_CHUNK = 64
DEEPNORM_ALPHA = (2 * DEPTH) ** 0.25
LN_EPS = 1e-5
RMS_EPS = 1e-6
NEG_INF = -1e30

SUBLANES = 8
VMEM_LIMIT_BYTES = 60 * 1024 * 1024

NN = (((1,), (0,)), ((), ()))
NT = (((1,), (1,)), ((), ()))
TN = (((0,), (0,)), ((), ()))


def _dot(a, b, dims=NN):
    return lax.dot_general(a, b, dims, preferred_element_type=F32)


def _split2(a):
    hi = a.astype(BF16)
    lo = (a - hi.astype(F32)).astype(BF16)
    return hi, lo


def _dot_hp(a, b, dims=NN):
    a_hi, a_lo = _split2(a)
    b_hi, b_lo = _split2(b)
    return _dot(a_hi, b_hi, dims) + (_dot(a_hi, b_lo, dims) + _dot(a_lo, b_hi, dims))


def _dot_exact_lhs(a_bf16, b, dims=NN):
    b0 = b.astype(BF16)
    r1 = b - b0.astype(F32)
    b1 = r1.astype(BF16)
    b2 = (r1 - b1.astype(F32)).astype(BF16)
    return _dot(a_bf16, b0, dims) + (_dot(a_bf16, b1, dims) + _dot(a_bf16, b2, dims))


def _dot_exact_rhs(a, b_bf16, dims=NN):
    a0 = a.astype(BF16)
    r1 = a - a0.astype(F32)
    a1 = r1.astype(BF16)
    a2 = (r1 - a1.astype(F32)).astype(BF16)
    return _dot(a0, b_bf16, dims) + (_dot(a1, b_bf16, dims) + _dot(a2, b_bf16, dims))


def _layer_norm(y, g, b):
    mu = jnp.mean(y, axis=-1, keepdims=True)
    yc = y - mu
    var = jnp.mean(yc * yc, axis=-1, keepdims=True)
    return yc * lax.rsqrt(var + LN_EPS) * g + b


def _silu(x):
    return x * jax.nn.sigmoid(x)


def _params(*sem):
    return pltpu.CompilerParams(dimension_semantics=sem, vmem_limit_bytes=VMEM_LIMIT_BYTES)


FFN_ROWS = 512


def _ffn_kernel(x_ref, wg_ref, wu_ref, wd_ref, g_ref, b_ref, o_ref, xb_ref):
    j = pl.program_id(1)

    @pl.when(j == 0)
    def _():
        xb_ref[...] = x_ref[...].astype(BF16)
        o_ref[...] = jnp.zeros_like(o_ref)

    for r in range(0, o_ref.shape[0], FFN_ROWS):
        rows = slice(r, r + FFN_ROWS)
        xb = xb_ref[rows, :]
        gate = _dot(xb, wg_ref[...])
        up = _dot(xb, wu_ref[...])
        h = (_silu(gate) * up).astype(BF16)
        o_ref[rows, :] += _dot(h, wd_ref[...])

    @pl.when(j == pl.num_programs(1) - 1)
    def _():
        for r in range(0, o_ref.shape[0], FFN_ROWS):
            rows = slice(r, r + FFN_ROWS)
            y = DEEPNORM_ALPHA * x_ref[rows, :] + 0.5 * o_ref[rows, :]
            o_ref[rows, :] = _layer_norm(y, g_ref[...], b_ref[...])


def _ffn_ln(x, wg, wu, wd, layer, ln_g, ln_b, tm=1024, tf=512):
    t, d = x.shape
    dff = wg.shape[2]
    return pl.pallas_call(
        _ffn_kernel,
        grid=(t // tm, dff // tf),
        in_specs=[
            pl.BlockSpec((tm, d), lambda i, j: (i, 0)),
            pl.BlockSpec((None, d, tf), lambda i, j: (layer, 0, j)),
            pl.BlockSpec((None, d, tf), lambda i, j: (layer, 0, j)),
            pl.BlockSpec((None, tf, d), lambda i, j: (layer, j, 0)),
            pl.BlockSpec((1, d), lambda i, j: (0, 0)),
            pl.BlockSpec((1, d), lambda i, j: (0, 0)),
        ],
        out_specs=pl.BlockSpec((tm, d), lambda i, j: (i, 0)),
        out_shape=jax.ShapeDtypeStruct((t, d), F32),
        scratch_shapes=[pltpu.VMEM((tm, d), BF16)],
        compiler_params=_params("parallel", "arbitrary"),
        name="ffn_ln",
    )(x, wg, wu, wd, ln_g.reshape(1, d), ln_b.reshape(1, d))


def _matmul_kernel(x_ref, w_ref, o_ref, xb_ref, *, w_is_transposed):
    @pl.when(pl.program_id(1) == 0)
    def _():
        xb_ref[...] = x_ref[...].astype(BF16)

    o_ref[...] = _dot(xb_ref[...], w_ref[...], NT if w_is_transposed else NN)


def _matmul(x, w, n, w_is_transposed=False, tm=1024, tn=1024):
    t, k = x.shape
    if w_is_transposed:
        w_spec = pl.BlockSpec((None, tn, k), lambda i, j: (0, j, 0))
    else:
        w_spec = pl.BlockSpec((None, k, tn), lambda i, j: (0, 0, j))
    return pl.pallas_call(
        functools.partial(_matmul_kernel, w_is_transposed=w_is_transposed),
        grid=(t // tm, n // tn),
        in_specs=[
            pl.BlockSpec((tm, k), lambda i, j: (i, 0)),
            w_spec,
        ],
        out_specs=pl.BlockSpec((tm, tn), lambda i, j: (i, j)),
        out_shape=jax.ShapeDtypeStruct((t, n), F32),
        scratch_shapes=[pltpu.VMEM((tm, k), BF16)],
        compiler_params=_params("parallel", "arbitrary"),
        name="in_proj",
    )(x, w)


def _proj_ln_kernel(*refs, n_in):
    a_refs = refs[:n_in]
    w_ref, x_ref, g_ref, b_ref, o_ref = refs[n_in:]
    acc = None
    row = 0
    for a_ref in a_refs:
        kc = a_ref.shape[1]
        part = _dot(a_ref[...], w_ref[row:row + kc, :])
        acc = part if acc is None else acc + part
        row += kc
    y = DEEPNORM_ALPHA * x_ref[...] + acc
    o_ref[...] = _layer_norm(y, g_ref[...], b_ref[...])


def _proj_ln(a_parts, w, x, ln_g, ln_b, tm=512):
    t, d = x.shape
    k = w.shape[1]
    in_specs = [pl.BlockSpec((tm, a.shape[1]), lambda i: (i, 0)) for a in a_parts]
    in_specs += [
        pl.BlockSpec((None, k, d), lambda i: (0, 0, 0), pipeline_mode=pl.Buffered(1)),
        pl.BlockSpec((tm, d), lambda i: (i, 0)),
        pl.BlockSpec((1, d), lambda i: (0, 0)),
        pl.BlockSpec((1, d), lambda i: (0, 0)),
    ]
    return pl.pallas_call(
        functools.partial(_proj_ln_kernel, n_in=len(a_parts)),
        grid=(t // tm,),
        in_specs=in_specs,
        out_specs=pl.BlockSpec((tm, d), lambda i: (i, 0)),
        out_shape=jax.ShapeDtypeStruct((t, d), F32),
        compiler_params=_params("parallel"),
        name="proj_ln",
    )(*a_parts, w, x, ln_g.reshape(1, d), ln_b.reshape(1, d))


def _rope(x, cos_full, sin_signed):
    return x * cos_full + pltpu.roll(x, HEAD_DIM // 2, 1) * sin_signed


def _moba_kernel(q_ref, k_ref, v_ref, cos_ref, sin_ref, o_ref):
    s = q_ref.shape[0]
    nb = s // MOBA_BLOCK
    blk = MOBA_BLOCK
    cos_full = cos_ref[...]
    sin_signed = sin_ref[...]
    qr = _rope(q_ref[...], cos_full, sin_signed)
    kr = _rope(k_ref[...], cos_full, sin_signed)

    row = lax.broadcasted_iota(jnp.int32, (HEAD_DIM, s), 0)
    col = lax.broadcasted_iota(jnp.int32, (HEAD_DIM, s), 1)
    avg = jnp.where(col // blk == row, 1.0 / blk, 0.0).astype(BF16)
    k_mean = _dot_exact_lhs(avg, kr)
    gate_t = _dot_hp(k_mean, qr, NT)

    qb = qr.astype(BF16)
    kb = kr.astype(BF16)
    vb = v_ref[...].astype(BF16)
    scale = HEAD_DIM ** -0.5

    assert nb <= SUBLANES and blk == 2 * HEAD_DIM
    blk_row = lax.broadcasted_iota(jnp.int32, (SUBLANES, blk), 0)
    qpos = lax.broadcasted_iota(jnp.int32, (blk, blk), 0)
    kpos = lax.broadcasted_iota(jnp.int32, (blk, blk), 1)
    causal = kpos <= qpos
    eye_q = jnp.where(kpos == qpos, 1.0, 0.0).astype(BF16)
    pad_rows = jnp.zeros((HEAD_DIM - SUBLANES, blk), F32)

    for i in range(nb):
        rows = slice(i * blk, (i + 1) * blk)
        qi = qb[rows]
        sel = None
        if i > MOBA_TOPK:
            g = gate_t[0:SUBLANES, rows]
            rank = jnp.zeros((SUBLANES, blk), F32)
            for m_blk in range(i):
                gm = g[m_blk:m_blk + 1, :]
                beats = (gm > g) | ((gm == g) & (m_blk < blk_row))
                rank = rank + jnp.where(beats, 1.0, 0.0)
            sel_t = jnp.where((rank < float(MOBA_TOPK)) & (blk_row < i), 1.0, 0.0)
            sel_t = jnp.concatenate([sel_t, pad_rows], axis=0).astype(BF16)
            sel = _dot(eye_q, sel_t, NT)
        scores = []
        for n in range(i + 1):
            sn = _dot(qi, kb[n * blk:(n + 1) * blk], NT) * scale
            if n == i:
                sn = jnp.where(causal, sn, NEG_INF)
            elif sel is not None:
                selected = jnp.broadcast_to(sel[:, n:n + 1], sn.shape) > 0.5
                sn = jnp.where(selected, sn, NEG_INF)
            scores.append(sn)
        m_acc = None
        for sn in scores:
            folded = jnp.maximum(sn[:, :HEAD_DIM], sn[:, HEAD_DIM:])
            m_acc = folded if m_acc is None else jnp.maximum(m_acc, folded)
        m = m_acc.max(axis=-1, keepdims=True)
        l_acc = jnp.zeros((blk, HEAD_DIM), F32)
        acc = jnp.zeros((blk, HEAD_DIM), F32)
        for n, sn in enumerate(scores):
            p = jnp.exp(sn - m)
            l_acc = l_acc + (p[:, :HEAD_DIM] + p[:, HEAD_DIM:])
            acc = acc + _dot(p.astype(BF16), vb[n * blk:(n + 1) * blk])
        l = l_acc.sum(axis=-1, keepdims=True)
        o_ref[rows, :] = (acc / l).astype(o_ref.dtype)


def _moba(hcat, cos_full, sin_signed, batch):
    t = hcat.shape[0]
    s = t // batch
    spec = lambda off: pl.BlockSpec((s, HEAD_DIM), lambda b, h: (b, off + h))
    tab = pl.BlockSpec((s, HEAD_DIM), lambda b, h: (0, 0))
    return pl.pallas_call(
        _moba_kernel,
        grid=(batch, A_HEADS),
        in_specs=[spec(0), spec(A_HEADS), spec(2 * A_HEADS), tab, tab],
        out_specs=pl.BlockSpec((s, HEAD_DIM), lambda b, h: (b, h)),
        out_shape=jax.ShapeDtypeStruct((t, A_WIDTH), BF16),
        compiler_params=_params("parallel", "parallel"),
        name="moba",
    )(hcat, hcat, hcat, cos_full, sin_signed)


def _gmlp_kernel(u_ref, v_ref, lng_ref, lnb_ref, ws_ref, bs_ref, o_ref):
    rows = u_ref.shape[0]
    tpos = lax.broadcasted_iota(jnp.int32, (GMLP_CHUNK, GMLP_CHUNK), 0)
    spos = lax.broadcasted_iota(jnp.int32, (GMLP_CHUNK, GMLP_CHUNK), 1)
    causal = spos <= tpos
    for g in range(G_GROUPS):
        cols = slice(g * G_DIM, (g + 1) * G_DIM)
        w = jnp.where(causal, ws_ref[g], 0.0).astype(BF16)
        bias = bs_ref[:, g:g + 1]
        ln_g = lng_ref[:, cols]
        ln_b = lnb_ref[:, cols]
        for c in range(rows // GMLP_CHUNK):
            r = slice(c * GMLP_CHUNK, (c + 1) * GMLP_CHUNK)
            vn = _layer_norm(jax.nn.gelu(v_ref[r, cols]), ln_g, ln_b)
            mixed = _dot(w, vn.astype(BF16)) + bias
            o_ref[r, cols] = (jax.nn.gelu(u_ref[r, cols]) * mixed).astype(o_ref.dtype)


def _gmlp(hcat, ln_g, ln_b, w_s, b_s, rows=512):
    t = hcat.shape[0]
    u_blk = 3 * A_WIDTH // G_WIDTH
    return pl.pallas_call(
        _gmlp_kernel,
        grid=(t // rows,),
        in_specs=[
            pl.BlockSpec((rows, G_WIDTH), lambda i: (i, u_blk)),
            pl.BlockSpec((rows, G_WIDTH), lambda i: (i, u_blk + 1)),
            pl.BlockSpec((1, G_WIDTH), lambda i: (0, 0)),
            pl.BlockSpec((1, G_WIDTH), lambda i: (0, 0)),
            pl.BlockSpec((G_GROUPS, GMLP_CHUNK, GMLP_CHUNK), lambda i: (0, 0, 0)),
            pl.BlockSpec((GMLP_CHUNK, G_GROUPS), lambda i: (0, 0)),
        ],
        out_specs=pl.BlockSpec((rows, G_WIDTH), lambda i: (i, 0)),
        out_shape=jax.ShapeDtypeStruct((t, G_WIDTH), BF16),
        compiler_params=_params("parallel"),
        name="gmlp",
    )(hcat, hcat, ln_g.reshape(1, G_WIDTH), ln_b.reshape(1, G_WIDTH), w_s, b_s.T)


GATE_BLK = 128


def _softplus(x):
    return jnp.maximum(x, 0.0) + jnp.log1p(jnp.exp(-jnp.abs(x)))


def _dn_gate_kernel(x_ref, w_ref, wt_ref, ng_ref, dtb_ref, ngc_ref, dtbc_ref,
                    bg_ref, gc_ref, gct_ref):
    s = x_ref.shape[0]
    hv = DN_V_HEADS
    x = x_ref[...].astype(BF16)
    ba = _dot(x, w_ref[...])
    bat = _dot(wt_ref[...], x, NT)
    lane = lax.broadcasted_iota(jnp.int32, (s, 2 * hv), 1)
    g = ng_ref[...] * _softplus(ba + dtb_ref[...])
    bg = jnp.where(lane < hv, jax.nn.sigmoid(ba), g)
    bg_ref[...] = bg
    gt = ngc_ref[...] * _softplus(bat + dtbc_ref[...])

    r = lax.broadcasted_iota(jnp.int32, (GATE_BLK, GATE_BLK), 0)
    c = lax.broadcasted_iota(jnp.int32, (GATE_BLK, GATE_BLK), 1)
    same = (r // DN_CHUNK) == (c // DN_CHUNK)
    lower = jnp.where(same & (c <= r), 1.0, 0.0).astype(BF16)
    upper = jnp.where(same & (r <= c), 1.0, 0.0).astype(BF16)
    for i in range(s // GATE_BLK):
        sl = slice(i * GATE_BLK, (i + 1) * GATE_BLK)
        gc_ref[sl, :] = _dot_exact_lhs(lower, bg[sl, :])
        gct_ref[:, sl] = _dot_exact_rhs(gt[:, sl], upper)


def _dn_gates(x, w_ba_t, a_log, dt_bias, rows=512):
    t, d = x.shape
    s = rows
    hv = DN_V_HEADS
    neg_rate = -jnp.exp(a_log.astype(F32))
    zeros = jnp.zeros((hv,), F32)
    ng = jnp.concatenate([zeros, neg_rate]).reshape(1, 2 * hv)
    dtb = jnp.concatenate([zeros, dt_bias.astype(F32)]).reshape(1, 2 * hv)
    small = lambda shape: pl.BlockSpec(shape, lambda b: (0, 0))
    return pl.pallas_call(
        _dn_gate_kernel,
        grid=(t // rows,),
        in_specs=[
            pl.BlockSpec((s, d), lambda b: (b, 0)),
            small((d, 2 * hv)),
            small((2 * hv, d)),
            small((1, 2 * hv)),
            small((1, 2 * hv)),
            small((2 * hv, 1)),
            small((2 * hv, 1)),
        ],
        out_specs=[
            pl.BlockSpec((s, 2 * hv), lambda b: (b, 0)),
            pl.BlockSpec((s, 2 * hv), lambda b: (b, 0)),
            pl.BlockSpec((2 * hv, s), lambda b: (0, b)),
        ],
        out_shape=[
            jax.ShapeDtypeStruct((t, 2 * hv), F32),
            jax.ShapeDtypeStruct((t, 2 * hv), F32),
            jax.ShapeDtypeStruct((2 * hv, t), F32),
        ],
        compiler_params=_params("parallel"),
        name="dn_gates",
    )(x, w_ba_t.T, w_ba_t, ng, dtb, ng.reshape(2 * hv, 1), dtb.reshape(2 * hv, 1))


DN_TB = 128
DN_HB = 16
CONV_PAD = 8


def _conv_stage(x_ref, buf_ref, first):
    tb = x_ref.shape[0]

    @pl.when(first)
    def _():
        buf_ref[0:CONV_PAD, :] = jnp.zeros((CONV_PAD, buf_ref.shape[1]), F32)

    buf_ref[CONV_PAD:CONV_PAD + tb, :] = x_ref[...]


def _conv_silu(buf_ref, w_ref, cols, tb):
    y = buf_ref[CONV_PAD:CONV_PAD + tb, cols] * w_ref[DN_CONV - 1:DN_CONV, cols]
    for j in range(1, DN_CONV):
        tap = DN_CONV - 1 - j
        y = y + buf_ref[CONV_PAD - j:CONV_PAD - j + tb, cols] * w_ref[tap:tap + 1, cols]
    return _silu(y)


def _conv_keep_tail(buf_ref, tb):
    buf_ref[0:CONV_PAD, :] = buf_ref[tb:tb + CONV_PAD, :]


def _hi_lo(a):
    hi = a.astype(BF16).astype(F32)
    return hi, a - hi


def _pair_lhs(x):
    hi, lo = _hi_lo(x)
    return jnp.concatenate([hi, lo], axis=1).astype(BF16)


def _pair_blockdiag(y, first_half):
    top = jnp.where(first_half, y, 0.0)
    return jnp.concatenate([top, y - top], axis=0)


def _pair_product(x, y, first_half):
    return _dot(x.astype(BF16), _pair_blockdiag(y, first_half).astype(BF16))


def _pair_product_hp(x, y, first_half):
    width = x.shape[1]
    y_hi, y_lo = _hi_lo(y)
    bd = jnp.concatenate([_pair_blockdiag(y_hi, first_half),
                          _pair_blockdiag(y_lo, first_half)], axis=1).astype(BF16)
    out = _dot(_pair_lhs(x), jnp.concatenate([bd, bd], axis=0))
    return out[:, :width] + out[:, width:]


def _delta_kernel(q_ref, k_ref, v_ref, z_ref, wq_ref, wk_ref, wv_ref, bg_ref, gc_ref, gct_ref,
                  ng_ref, o_ref, state_ref, qbuf_ref, kbuf_ref, vbuf_ref):
    tb = q_ref.shape[0]
    hv = DN_V_HEADS
    c = DN_CHUNK
    dk = DN_HEAD_DIM
    rep = DN_V_HEADS // DN_QK_HEADS
    group = pl.program_id(1)
    first = pl.program_id(2) == 0

    @pl.when(first)
    def _():
        state_ref[...] = jnp.zeros_like(state_ref)

    _conv_stage(q_ref, qbuf_ref, first)
    _conv_stage(k_ref, kbuf_ref, first)
    _conv_stage(v_ref, vbuf_ref, first)

    assert tb == 2 * c
    ri = lax.broadcasted_iota(jnp.int32, (c, 2 * c), 0)
    ci2 = lax.broadcasted_iota(jnp.int32, (c, 2 * c), 1)
    first_half = ci2 < c
    ci = jnp.where(first_half, ci2, ci2 - c)
    incl = ci <= ri
    strict = ci < ri
    eye = jnp.where(ci == ri, 1.0, 0.0)
    norm_g = ng_ref[...]
    n_qk = DN_HB // rep
    heads = range(DN_HB)

    bg = bg_ref[...]
    gcs = gc_ref[...]
    gct = gct_ref[...]
    head_lane = lax.broadcasted_iota(jnp.int32, (tb, 2 * hv), 1)
    head_row = lax.broadcasted_iota(jnp.int32, (hv, tb), 0)

    q_n, k_n = [], []
    for jq in range(n_qk):
        qcols = slice(jq * dk, (jq + 1) * dk)
        q_h = _conv_silu(qbuf_ref, wq_ref, qcols, tb)
        k_h = _conv_silu(kbuf_ref, wk_ref, qcols, tb)
        q_n.append(q_h * lax.rsqrt(jnp.sum(q_h * q_h, axis=-1, keepdims=True) + RMS_EPS)
                   * (dk ** -0.5))
        k_n.append(k_h * lax.rsqrt(jnp.sum(k_h * k_h, axis=-1, keepdims=True) + RMS_EPS))

    qk_pair, kk_pair = [], []
    for jq in range(n_qk):
        q0, q1 = q_n[jq][:c], q_n[jq][c:]
        k0, k1 = k_n[jq][:c], k_n[jq][c:]
        zero = jnp.zeros_like(k0)
        lhs = jnp.concatenate([jnp.concatenate([q0, q1], axis=1),
                               jnp.concatenate([k0, k1], axis=1)], axis=0).astype(BF16)
        rhs = jnp.concatenate([jnp.concatenate([k0, zero], axis=1),
                               jnp.concatenate([zero, k1], axis=1)], axis=0).astype(BF16)
        g = _dot(lhs, rhs, NT)
        qk_pair.append(g[:c])
        kk_pair.append(g[c:])

    a_pair, attn_bf, kbg_bf, q_dec, k_dec, g_last, betas = [], [], [], [], [], [], []
    for hl in heads:
        jq = hl // rep
        head = group * DN_HB + hl
        beta_col = jnp.sum(jnp.where(head_lane == head, bg, 0.0), axis=-1, keepdims=True)
        betas.append(beta_col)
        gc_col = jnp.sum(jnp.where(head_lane == head + hv, gcs, 0.0), axis=-1, keepdims=True)
        gc_row = jnp.sum(jnp.where(head_row == head, gct, 0.0), axis=0, keepdims=True)
        beta_sel = jnp.where(first_half, beta_col[:c], beta_col[c:])
        gc_sel = jnp.where(first_half, gc_col[:c], gc_col[c:])
        decay = jnp.exp(jnp.where(incl, gc_sel - gc_row, -jnp.inf))
        a_pair.append(jnp.where(strict, kk_pair[jq] * beta_sel * decay, 0.0))
        attn_bf.append((qk_pair[jq] * decay).astype(BF16))
        egc = jnp.exp(gc_col)
        kbg_bf.append((k_n[jq] * beta_col * egc).astype(BF16))
        q_dec.append((q_n[jq] * egc).astype(BF16))
        gl = [gc_col[c - 1:c, :], gc_col[tb - 1:tb, :]]
        gl_rows = jnp.concatenate([jnp.broadcast_to(gl[0], (c, 1)),
                                   jnp.broadcast_to(gl[1], (c, 1))], axis=0)
        k_dec.append((k_n[jq] * jnp.exp(gl_rows - gc_col)).astype(BF16))
        g_last.append([jnp.exp(gl[0]), jnp.exp(gl[1])])

    vb_bf, z_gate = [], []

    def value_conv(hl):
        vcols = slice(hl * dk, (hl + 1) * dk)
        vb_bf.append((_conv_silu(vbuf_ref, wv_ref, vcols, tb) * betas[hl]).astype(BF16))

    def output_gate(hl):
        z_gate.append(norm_g * _silu(z_ref[:, hl * dk:(hl + 1) * dk]))

    independent = ([functools.partial(value_conv, hl) for hl in heads]
                   + [functools.partial(output_gate, hl) for hl in heads])
    n_slots = 2 * (int(math.log2(c)) - 1) + 2
    per_slot = -(-len(independent) // n_slots)

    def fill():
        for work in independent[:per_slot]:
            work()
        del independent[:per_slot]

    inv = [eye - jnp.where((ri // 2) == (ci // 2), a, 0.0) for a in a_pair]
    s = 2
    while s < c:
        off = ((ri // (2 * s)) == (ci // (2 * s))) & ((ri // s) != (ci // s))
        m1 = [_pair_product(jnp.where(off, a_pair[hl], 0.0), inv[hl], first_half)
              for hl in heads]
        fill()
        m2 = [_pair_product(inv[hl], m1[hl], first_half) for hl in heads]
        fill()
        inv = [inv[hl] - m2[hl] for hl in heads]
        s *= 2
    ax = [_pair_product_hp(a_pair[hl], inv[hl], first_half) for hl in heads]
    fill()
    resid = [eye - inv[hl] - ax[hl] for hl in heads]
    corr = [_pair_product(inv[hl], resid[hl], first_half) for hl in heads]
    fill()
    inv = [inv[hl] + corr[hl] for hl in heads]
    assert not independent

    u, w_bf = [], []
    zero_rows = jnp.zeros((c, 2 * dk), BF16)
    for hl in heads:
        lhs = _pair_lhs(inv[hl])
        rhs = jnp.concatenate([vb_bf[hl], kbg_bf[hl]], axis=1)
        r0, r1 = rhs[:c], rhs[c:]
        uw0 = _dot(lhs, jnp.concatenate([r0, zero_rows, r0, zero_rows], axis=0))
        uw1 = _dot(lhs, jnp.concatenate([zero_rows, r1, zero_rows, r1], axis=0))
        u.append([uw0[:, :dk], uw1[:, :dk]])
        w_bf.append([uw0[:, dk:].astype(BF16), uw1[:, dk:].astype(BF16)])

    states = [state_ref[hl] for hl in heads]
    zero_v = jnp.zeros((c, dk), BF16)
    for ch in range(2):
        rows = slice(ch * c, (ch + 1) * c)
        ws_qs = [_dot(jnp.concatenate([w_bf[hl][ch], q_dec[hl][rows]], axis=0),
                      states[hl].astype(BF16)) for hl in heads]
        v_new_bf = [(u[hl][ch] - ws_qs[hl][:c]).astype(BF16) for hl in heads]
        o_intra = []
        for hl in heads:
            v_rows = [v_new_bf[hl], zero_v] if ch == 0 else [zero_v, v_new_bf[hl]]
            o_intra.append(_dot(attn_bf[hl], jnp.concatenate(v_rows, axis=0)))
        states = [states[hl] * g_last[hl][ch] + _dot(k_dec[hl][rows], v_new_bf[hl], TN)
                  for hl in heads]
        for hl in heads:
            vcols = slice(hl * dk, (hl + 1) * dk)
            o_c = ws_qs[hl][c:] + o_intra[hl]
            o_n = (o_c * lax.rsqrt(jnp.mean(o_c * o_c, axis=-1, keepdims=True) + RMS_EPS)
                   * z_gate[hl][rows])
            o_ref[rows, vcols] = o_n.astype(o_ref.dtype)
    for hl in heads:
        state_ref[hl] = states[hl]
    _conv_keep_tail(qbuf_ref, tb)
    _conv_keep_tail(kbuf_ref, tb)
    _conv_keep_tail(vbuf_ref, tb)


def _delta(hcat, conv_w, bg, gcs, gct, norm_g, batch):
    t = hcat.shape[0]
    s = t // batch
    nt = s // DN_TB
    rep = DN_V_HEADS // DN_QK_HEADS
    wq = DN_HB // rep * DN_HEAD_DIM
    wv = DN_HB * DN_HEAD_DIM
    k_off = DN_QK_WIDTH // wq
    v_off = 2 * DN_QK_WIDTH // wv
    z_off = DN_CONV_DIM // wv
    hv2 = 2 * DN_V_HEADS
    row = lambda b, g, i: b * nt + i
    return pl.pallas_call(
        _delta_kernel,
        grid=(batch, DN_V_HEADS // DN_HB, nt),
        in_specs=[
            pl.BlockSpec((DN_TB, wq), lambda b, g, i: (row(b, g, i), g)),
            pl.BlockSpec((DN_TB, wq), lambda b, g, i: (row(b, g, i), k_off + g)),
            pl.BlockSpec((DN_TB, wv), lambda b, g, i: (row(b, g, i), v_off + g)),
            pl.BlockSpec((DN_TB, wv), lambda b, g, i: (row(b, g, i), z_off + g)),
            pl.BlockSpec((DN_CONV, wq), lambda b, g, i: (0, g)),
            pl.BlockSpec((DN_CONV, wq), lambda b, g, i: (0, k_off + g)),
            pl.BlockSpec((DN_CONV, wv), lambda b, g, i: (0, v_off + g)),
            pl.BlockSpec((DN_TB, hv2), lambda b, g, i: (row(b, g, i), 0)),
            pl.BlockSpec((DN_TB, hv2), lambda b, g, i: (row(b, g, i), 0)),
            pl.BlockSpec((DN_V_HEADS, DN_TB), lambda b, g, i: (1, row(b, g, i))),
            pl.BlockSpec((1, DN_HEAD_DIM), lambda b, g, i: (0, 0)),
        ],
        out_specs=pl.BlockSpec((DN_TB, wv), lambda b, g, i: (row(b, g, i), g)),
        out_shape=jax.ShapeDtypeStruct((t, DN_V_WIDTH), BF16),
        scratch_shapes=[
            pltpu.VMEM((DN_HB, DN_HEAD_DIM, DN_HEAD_DIM), F32),
            pltpu.VMEM((CONV_PAD + DN_TB, wq), F32),
            pltpu.VMEM((CONV_PAD + DN_TB, wq), F32),
            pltpu.VMEM((CONV_PAD + DN_TB, wv), F32),
        ],
        compiler_params=_params("parallel", "parallel", "arbitrary"),
        name="delta_rule",
    )(hcat, hcat, hcat, hcat, conv_w, conv_w, conv_w, bg, gcs, gct,
      norm_g.reshape(1, DN_HEAD_DIM))


def _rope_tables(s):
    half = HEAD_DIM // 2
    inv = jnp.exp(-math.log(ROPE_THETA) * jnp.arange(half, dtype=F32) * (2.0 / HEAD_DIM))
    ang = jnp.arange(s, dtype=F32)[:, None] * inv[None, :]
    cos, sin = jnp.cos(ang), jnp.sin(ang)
    return jnp.concatenate([cos, cos], axis=-1), jnp.concatenate([-sin, sin], axis=-1)


def kernel(x, ffn1_w_gate, ffn1_w_up, ffn1_w_down, ffn2_w_gate, ffn2_w_up, ffn2_w_down, ln_g, ln_b, ab_w_in, ab_gmlp_ln_g, ab_gmlp_ln_b, ab_gmlp_w_s, ab_gmlp_b_s, ab_w_out, dn_w_in, dn_conv_w, dn_a_log, dn_dt_bias, dn_norm_g, dn_w_out):
    batch, s, d = x.shape
    t = batch * s
    bf = lambda w: w.astype(BF16)
    xf = x.reshape(t, d)
    cos_full, sin_signed = _rope_tables(s)
    ffn1 = (bf(ffn1_w_gate), bf(ffn1_w_up), bf(ffn1_w_down))
    ffn2 = (bf(ffn2_w_gate), bf(ffn2_w_up), bf(ffn2_w_down))
    ab_in, ab_out, dn_out = bf(ab_w_in), bf(ab_w_out), bf(dn_w_out)
    dn_in_t = bf(jnp.swapaxes(dn_w_in, 1, 2))

    for i in range(DEPTH):
        xf = _ffn_ln(xf, *ffn1, i, ln_g[i, 0], ln_b[i, 0])
        j = i // 2
        if i % 2 == 0:
            hcat = _matmul(xf, ab_in[j:j + 1], AB_IN)
            a_out = _moba(hcat, cos_full, sin_signed, batch)
            g_out = _gmlp(hcat, ab_gmlp_ln_g[j], ab_gmlp_ln_b[j], ab_gmlp_w_s[j], ab_gmlp_b_s[j])
            mix, w_out = [a_out, g_out], ab_out[j:j + 1]
        else:
            hcat = _matmul(xf, dn_in_t[j:j + 1], DN_MAIN, w_is_transposed=True)
            bg, gcs, gct = _dn_gates(xf, dn_in_t[j, DN_MAIN:], dn_a_log[j], dn_dt_bias[j])
            mix = [_delta(hcat, dn_conv_w[j], bg, gcs, gct, dn_norm_g[j], batch)]
            w_out = dn_out[j:j + 1]
        xf = _proj_ln(mix, w_out, xf, ln_g[i, 1], ln_b[i, 1])
        xf = _ffn_ln(xf, *ffn2, i, ln_g[i, 2], ln_b[i, 2])
    return xf.reshape(batch, s, d)
```

```python
import functools
import math

import jax
import jax.numpy as jnp
from jax import lax
from jax.experimental import pallas as pl
from jax.experimental.pallas import tpu as pltpu

F32 = jnp.float32
BF16 = jnp.bfloat16

D_MODEL = 2048
SEQ = 2048
DEPTH = 2
HEAD_DIM = 128
A_HEADS = 8
A_WIDTH = A_HEADS * HEAD_DIM
MOBA_BLOCK = 256
MOBA_TOPK = 3
ROPE_THETA = 10000.0
G_GROUPS = 8
G_DIM = 128
G_WIDTH = G_GROUPS * G_DIM
GMLP_CHUNK = 128
AB_IN = 3 * A_WIDTH + 2 * G_WIDTH
DN_QK_HEADS = 16
DN_V_HEADS = 32
DN_HEAD_DIM = 128
DN_QK_WIDTH = DN_QK_HEADS * DN_HEAD_DIM
DN_V_WIDTH = DN_V_HEADS * DN_HEAD_DIM
DN_CONV_DIM = 2 * DN_QK_WIDTH + DN_V_WIDTH
DN_MAIN = DN_CONV_DIM + DN_V_WIDTH
DN_CONV = 4
DN_CHUNK = 64
DEEPNORM_ALPHA = (2 * DEPTH) ** 0.25
LN_EPS = 1e-5
RMS_EPS = 1e-6
NEG_INF = -1e30

SUBLANES = 8
VMEM_LIMIT_BYTES = 60 * 1024 * 1024

NN = (((1,), (0,)), ((), ()))
NT = (((1,), (1,)), ((), ()))
TN = (((0,), (0,)), ((), ()))


def _dot(a, b, dims=NN):
    return lax.dot_general(a, b, dims, preferred_element_type=F32)


def _split2(a):
    hi = a.astype(BF16)
    lo = (a - hi.astype(F32)).astype(BF16)
    return hi, lo


def _dot_hp(a, b, dims=NN):
    a_hi, a_lo = _split2(a)
    b_hi, b_lo = _split2(b)
    return _dot(a_hi, b_hi, dims) + (_dot(a_hi, b_lo, dims) + _dot(a_lo, b_hi, dims))


def _dot_exact_lhs(a_bf16, b, dims=NN):
    b0 = b.astype(BF16)
    r1 = b - b0.astype(F32)
    b1 = r1.astype(BF16)
    b2 = (r1 - b1.astype(F32)).astype(BF16)
    return _dot(a_bf16, b0, dims) + (_dot(a_bf16, b1, dims) + _dot(a_bf16, b2, dims))


def _dot_exact_rhs(a, b_bf16, dims=NN):
    a0 = a.astype(BF16)
    r1 = a - a0.astype(F32)
    a1 = r1.astype(BF16)
    a2 = (r1 - a1.astype(F32)).astype(BF16)
    return _dot(a0, b_bf16, dims) + (_dot(a1, b_bf16, dims) + _dot(a2, b_bf16, dims))


def _layer_norm(y, g, b):
    mu = jnp.mean(y, axis=-1, keepdims=True)
    yc = y - mu
    var = jnp.mean(yc * yc, axis=-1, keepdims=True)
    return yc * lax.rsqrt(var + LN_EPS) * g + b


def _silu(x):
    return x * jax.nn.sigmoid(x)


def _params(*sem):
    return pltpu.CompilerParams(dimension_semantics=sem, vmem_limit_bytes=VMEM_LIMIT_BYTES)


FFN_ROWS = 512


def _ffn_kernel(x_ref, wg_ref, wu_ref, wd_ref, g_ref, b_ref, o_ref, xb_ref):
    j = pl.program_id(1)

    @pl.when(j == 0)
    def _():
        xb_ref[...] = x_ref[...].astype(BF16)
        o_ref[...] = jnp.zeros_like(o_ref)

    for r in range(0, o_ref.shape[0], FFN_ROWS):
        rows = slice(r, r + FFN_ROWS)
        xb = xb_ref[rows, :]
        gate = _dot(xb, wg_ref[...])
        up = _dot(xb, wu_ref[...])
        h = (_silu(gate) * up).astype(BF16)
        o_ref[rows, :] += _dot(h, wd_ref[...])

    @pl.when(j == pl.num_programs(1) - 1)
    def _():
        for r in range(0, o_ref.shape[0], FFN_ROWS):
            rows = slice(r, r + FFN_ROWS)
            y = DEEPNORM_ALPHA * x_ref[rows, :] + 0.5 * o_ref[rows, :]
            o_ref[rows, :] = _layer_norm(y, g_ref[...], b_ref[...])


def _ffn_ln(x, wg, wu, wd, ln_g, ln_b, tm=1024, tf=512):
    t, d = x.shape
    dff = wg.shape[1]
    return pl.pallas_call(
        _ffn_kernel,
        grid=(t // tm, dff // tf),
        in_specs=[
            pl.BlockSpec((tm, d), lambda i, j: (i, 0)),
            pl.BlockSpec((d, tf), lambda i, j: (0, j)),
            pl.BlockSpec((d, tf), lambda i, j: (0, j)),
            pl.BlockSpec((tf, d), lambda i, j: (j, 0)),
            pl.BlockSpec((1, d), lambda i, j: (0, 0)),
            pl.BlockSpec((1, d), lambda i, j: (0, 0)),
        ],
        out_specs=pl.BlockSpec((tm, d), lambda i, j: (i, 0)),
        out_shape=jax.ShapeDtypeStruct((t, d), F32),
        scratch_shapes=[pltpu.VMEM((tm, d), BF16)],
        compiler_params=_params("parallel", "arbitrary"),
        name="ffn_ln",
    )(x, wg, wu, wd, ln_g.reshape(1, d), ln_b.reshape(1, d))


def _cast_job_specs(jobs, grid):
    n_steps = math.prod(grid)
    strides = [math.prod(grid[a + 1:]) for a in range(len(grid))]
    step = lambda *ids: sum(i * s for i, s in zip(ids, strides))
    in_specs, out_specs, out_shapes = [], [], []
    for w, layer in jobs:
        _, r, c = w.shape
        slab = r // n_steps
        assert slab * n_steps == r and slab % (2 * SUBLANES) == 0
        in_specs.append(pl.BlockSpec((None, slab, c),
                                     lambda *ids, layer=layer: (layer, step(*ids), 0)))
        out_specs.append(pl.BlockSpec((slab, c), lambda *ids: (step(*ids), 0)))
        out_shapes.append(jax.ShapeDtypeStruct((r, c), BF16))
    return in_specs, out_specs, out_shapes


def _run_cast_jobs(src_refs, dst_refs):
    for src, dst in zip(src_refs, dst_refs):
        dst[...] = src[...].astype(BF16)


AB_TN = 1280
DN_TN = 1536


def _matmul_kernel(x_ref, w_ref, *rest, n_cast, w_is_transposed):
    cast_src = rest[:n_cast]
    o_ref = rest[n_cast]
    cast_dst = rest[n_cast + 1:2 * n_cast + 1]
    xb_ref = rest[2 * n_cast + 1]

    @pl.when(pl.program_id(1) == 0)
    def _():
        xb_ref[...] = x_ref[...].astype(BF16)

    o_ref[...] = _dot(xb_ref[...], w_ref[...], NT if w_is_transposed else NN)
    _run_cast_jobs(cast_src, cast_dst)


def _matmul(x, w, n, tn, w_is_transposed=False, cast_jobs=(), tm=1024):
    t, k = x.shape
    grid = (t // tm, n // tn)
    if w_is_transposed:
        w_spec = pl.BlockSpec((None, tn, k), lambda i, j: (0, j, 0))
    else:
        w_spec = pl.BlockSpec((None, k, tn), lambda i, j: (0, 0, j))
    cast_in, cast_out, cast_shapes = _cast_job_specs(cast_jobs, grid)
    outs = pl.pallas_call(
        functools.partial(_matmul_kernel, n_cast=len(cast_jobs),
                          w_is_transposed=w_is_transposed),
        grid=grid,
        in_specs=[pl.BlockSpec((tm, k), lambda i, j: (i, 0)), w_spec] + cast_in,
        out_specs=[pl.BlockSpec((tm, tn), lambda i, j: (i, j))] + cast_out,
        out_shape=[jax.ShapeDtypeStruct((t, n), F32)] + cast_shapes,
        scratch_shapes=[pltpu.VMEM((tm, k), BF16)],
        compiler_params=_params("parallel", "arbitrary"),
        name="in_proj",
    )(x, w, *[job[0] for job in cast_jobs])
    return outs[0], outs[1:]


PROJ_ROWS = 256


def _proj_ln_kernel(*refs, n_in):
    a_refs = refs[:n_in]
    w_ref, x_ref, g_ref, b_ref, o_ref = refs[n_in:]
    for r in range(0, o_ref.shape[0], PROJ_ROWS):
        rows = slice(r, r + PROJ_ROWS)
        acc = None
        row = 0
        for a_ref in a_refs:
            kc = a_ref.shape[1]
            part = _dot(a_ref[rows, :], w_ref[row:row + kc, :])
            acc = part if acc is None else acc + part
            row += kc
        y = DEEPNORM_ALPHA * x_ref[rows, :] + acc
        o_ref[rows, :] = _layer_norm(y, g_ref[...], b_ref[...])


def _proj_ln(a_parts, w, x, ln_g, ln_b, tm=512):
    t, d = x.shape
    k = w.shape[1]
    in_specs = [pl.BlockSpec((tm, a.shape[1]), lambda i: (i, 0)) for a in a_parts]
    in_specs += [
        pl.BlockSpec((None, k, d), lambda i: (0, 0, 0), pipeline_mode=pl.Buffered(1)),
        pl.BlockSpec((tm, d), lambda i: (i, 0)),
        pl.BlockSpec((1, d), lambda i: (0, 0)),
        pl.BlockSpec((1, d), lambda i: (0, 0)),
    ]
    return pl.pallas_call(
        functools.partial(_proj_ln_kernel, n_in=len(a_parts)),
        grid=(t // tm,),
        in_specs=in_specs,
        out_specs=pl.BlockSpec((tm, d), lambda i: (i, 0)),
        out_shape=jax.ShapeDtypeStruct((t, d), F32),
        compiler_params=_params("parallel"),
        name="proj_ln",
    )(*a_parts, w, x, ln_g.reshape(1, d), ln_b.reshape(1, d))


def _rope(x, cos_full, sin_signed):
    return x * cos_full + pltpu.roll(x, HEAD_DIM // 2, 1) * sin_signed


def _moba_kernel(q_ref, k_ref, v_ref, cos_ref, sin_ref, *rest, n_cast):
    o_ref = rest[n_cast]
    _run_cast_jobs(rest[:n_cast], rest[n_cast + 1:])
    s = q_ref.shape[0]
    nb = s // MOBA_BLOCK
    blk = MOBA_BLOCK
    cos_full = cos_ref[...]
    sin_signed = sin_ref[...]
    qr = _rope(q_ref[...], cos_full, sin_signed)
    kr = _rope(k_ref[...], cos_full, sin_signed)

    row = lax.broadcasted_iota(jnp.int32, (HEAD_DIM, s), 0)
    col = lax.broadcasted_iota(jnp.int32, (HEAD_DIM, s), 1)
    avg = jnp.where(col // blk == row, 1.0 / blk, 0.0).astype(BF16)
    k_mean = _dot_exact_lhs(avg, kr)
    gate_t = _dot_hp(k_mean, qr, NT)

    qb = qr.astype(BF16)
    kb = kr.astype(BF16)
    vb = v_ref[...].astype(BF16)
    scale = HEAD_DIM ** -0.5

    assert nb <= SUBLANES and blk == 2 * HEAD_DIM
    blk_row = lax.broadcasted_iota(jnp.int32, (SUBLANES, blk), 0)
    qpos = lax.broadcasted_iota(jnp.int32, (blk, blk), 0)
    kpos = lax.broadcasted_iota(jnp.int32, (blk, blk), 1)
    causal = kpos <= qpos
    eye_q = jnp.where(kpos == qpos, 1.0, 0.0).astype(BF16)
    pad_rows = jnp.zeros((HEAD_DIM - SUBLANES, blk), F32)

    for i in range(nb):
        rows = slice(i * blk, (i + 1) * blk)
        qi = qb[rows]
        sel = None
        if i > MOBA_TOPK:
            g = gate_t[0:SUBLANES, rows]
            rank = jnp.zeros((SUBLANES, blk), F32)
            for m_blk in range(i):
                gm = g[m_blk:m_blk + 1, :]
                beats = (gm > g) | ((gm == g) & (m_blk < blk_row))
                rank = rank + jnp.where(beats, 1.0, 0.0)
            sel_t = jnp.where((rank < float(MOBA_TOPK)) & (blk_row < i), 1.0, 0.0)
            sel_t = jnp.concatenate([sel_t, pad_rows], axis=0).astype(BF16)
            sel = _dot(eye_q, sel_t, NT)
        scores = []
        for n in range(i + 1):
            sn = _dot(qi, kb[n * blk:(n + 1) * blk], NT) * scale
            if n == i:
                sn = jnp.where(causal, sn, NEG_INF)
            elif sel is not None:
                selected = jnp.broadcast_to(sel[:, n:n + 1], sn.shape) > 0.5
                sn = jnp.where(selected, sn, NEG_INF)
            scores.append(sn)
        m_acc = None
        for sn in scores:
            folded = jnp.maximum(sn[:, :HEAD_DIM], sn[:, HEAD_DIM:])
            m_acc = folded if m_acc is None else jnp.maximum(m_acc, folded)
        m = m_acc.max(axis=-1, keepdims=True)
        l_acc = jnp.zeros((blk, HEAD_DIM), F32)
        acc = jnp.zeros((blk, HEAD_DIM), F32)
        for n, sn in enumerate(scores):
            p = jnp.exp(sn - m)
            l_acc = l_acc + (p[:, :HEAD_DIM] + p[:, HEAD_DIM:])
            acc = acc + _dot(p.astype(BF16), vb[n * blk:(n + 1) * blk])
        l = l_acc.sum(axis=-1, keepdims=True)
        o_ref[rows, :] = (acc / l).astype(o_ref.dtype)


def _moba(hcat, cos_full, sin_signed, batch, cast_jobs=()):
    t = hcat.shape[0]
    s = t // batch
    grid = (batch, A_HEADS)
    spec = lambda off: pl.BlockSpec((s, HEAD_DIM), lambda b, h: (b, off + h))
    tab = pl.BlockSpec((s, HEAD_DIM), lambda b, h: (0, 0))
    cast_in, cast_out, cast_shapes = _cast_job_specs(cast_jobs, grid)
    outs = pl.pallas_call(
        functools.partial(_moba_kernel, n_cast=len(cast_jobs)),
        grid=grid,
        in_specs=[spec(0), spec(A_HEADS), spec(2 * A_HEADS), tab, tab] + cast_in,
        out_specs=[pl.BlockSpec((s, HEAD_DIM), lambda b, h: (b, h))] + cast_out,
        out_shape=[jax.ShapeDtypeStruct((t, A_WIDTH), BF16)] + cast_shapes,
        compiler_params=_params("parallel", "parallel"),
        name="moba",
    )(hcat, hcat, hcat, cos_full, sin_signed, *[job[0] for job in cast_jobs])
    return outs[0], outs[1:]


def _gmlp_kernel(u_ref, v_ref, lng_ref, lnb_ref, ws_ref, bs_ref, o_ref):
    rows = u_ref.shape[0]
    tpos = lax.broadcasted_iota(jnp.int32, (GMLP_CHUNK, GMLP_CHUNK), 0)
    spos = lax.broadcasted_iota(jnp.int32, (GMLP_CHUNK, GMLP_CHUNK), 1)
    causal = spos <= tpos
    for g in range(G_GROUPS):
        cols = slice(g * G_DIM, (g + 1) * G_DIM)
        w = jnp.where(causal, ws_ref[g], 0.0).astype(BF16)
        bias = bs_ref[:, g:g + 1]
        ln_g = lng_ref[:, cols]
        ln_b = lnb_ref[:, cols]
        for c in range(rows // GMLP_CHUNK):
            r = slice(c * GMLP_CHUNK, (c + 1) * GMLP_CHUNK)
            vn = _layer_norm(jax.nn.gelu(v_ref[r, cols]), ln_g, ln_b)
            mixed = _dot(w, vn.astype(BF16)) + bias
            o_ref[r, cols] = (jax.nn.gelu(u_ref[r, cols]) * mixed).astype(o_ref.dtype)


def _gmlp(hcat, ln_g, ln_b, w_s, b_s, rows=512):
    t = hcat.shape[0]
    u_blk = 3 * A_WIDTH // G_WIDTH
    return pl.pallas_call(
        _gmlp_kernel,
        grid=(t // rows,),
        in_specs=[
            pl.BlockSpec((rows, G_WIDTH), lambda i: (i, u_blk)),
            pl.BlockSpec((rows, G_WIDTH), lambda i: (i, u_blk + 1)),
            pl.BlockSpec((1, G_WIDTH), lambda i: (0, 0)),
            pl.BlockSpec((1, G_WIDTH), lambda i: (0, 0)),
            pl.BlockSpec((G_GROUPS, GMLP_CHUNK, GMLP_CHUNK), lambda i: (0, 0, 0)),
            pl.BlockSpec((GMLP_CHUNK, G_GROUPS), lambda i: (0, 0)),
        ],
        out_specs=pl.BlockSpec((rows, G_WIDTH), lambda i: (i, 0)),
        out_shape=jax.ShapeDtypeStruct((t, G_WIDTH), BF16),
        compiler_params=_params("parallel"),
        name="gmlp",
    )(hcat, hcat, ln_g.reshape(1, G_WIDTH), ln_b.reshape(1, G_WIDTH), w_s, b_s.T)


GATE_BLK = 128


def _softplus(x):
    return jnp.maximum(x, 0.0) + jnp.log1p(jnp.exp(-jnp.abs(x)))


def _dn_gate_kernel(x_ref, w_ref, wt_ref, ng_ref, dtb_ref, ngc_ref, dtbc_ref,
                    bg_ref, gc_ref, gct_ref):
    s = x_ref.shape[0]
    hv = DN_V_HEADS
    x = x_ref[...].astype(BF16)
    ba = _dot(x, w_ref[...])
    bat = _dot(wt_ref[...], x, NT)
    lane = lax.broadcasted_iota(jnp.int32, (s, 2 * hv), 1)
    g = ng_ref[...] * _softplus(ba + dtb_ref[...])
    bg = jnp.where(lane < hv, jax.nn.sigmoid(ba), g)
    bg_ref[...] = bg
    gt = ngc_ref[...] * _softplus(bat + dtbc_ref[...])

    r = lax.broadcasted_iota(jnp.int32, (GATE_BLK, GATE_BLK), 0)
    c = lax.broadcasted_iota(jnp.int32, (GATE_BLK, GATE_BLK), 1)
    same = (r // DN_CHUNK) == (c // DN_CHUNK)
    lower = jnp.where(same & (c <= r), 1.0, 0.0).astype(BF16)
    upper = jnp.where(same & (r <= c), 1.0, 0.0).astype(BF16)
    for i in range(s // GATE_BLK):
        sl = slice(i * GATE_BLK, (i + 1) * GATE_BLK)
        gc_ref[sl, :] = _dot_exact_lhs(lower, bg[sl, :])
        gct_ref[:, sl] = _dot_exact_rhs(gt[:, sl], upper)


def _dn_gates(x, w_ba_t, a_log, dt_bias, rows=512):
    t, d = x.shape
    s = rows
    hv = DN_V_HEADS
    neg_rate = -jnp.exp(a_log.astype(F32))
    zeros = jnp.zeros((hv,), F32)
    ng = jnp.concatenate([zeros, neg_rate]).reshape(1, 2 * hv)
    dtb = jnp.concatenate([zeros, dt_bias.astype(F32)]).reshape(1, 2 * hv)
    small = lambda shape: pl.BlockSpec(shape, lambda b: (0, 0))
    return pl.pallas_call(
        _dn_gate_kernel,
        grid=(t // rows,),
        in_specs=[
            pl.BlockSpec((s, d), lambda b: (b, 0)),
            small((d, 2 * hv)),
            small((2 * hv, d)),
            small((1, 2 * hv)),
            small((1, 2 * hv)),
            small((2 * hv, 1)),
            small((2 * hv, 1)),
        ],
        out_specs=[
            pl.BlockSpec((s, 2 * hv), lambda b: (b, 0)),
            pl.BlockSpec((s, 2 * hv), lambda b: (b, 0)),
            pl.BlockSpec((2 * hv, s), lambda b: (0, b)),
        ],
        out_shape=[
            jax.ShapeDtypeStruct((t, 2 * hv), F32),
            jax.ShapeDtypeStruct((t, 2 * hv), F32),
            jax.ShapeDtypeStruct((2 * hv, t), F32),
        ],
        compiler_params=_params("parallel"),
        name="dn_gates",
    )(x, w_ba_t.T, w_ba_t, ng, dtb, ng.reshape(2 * hv, 1), dtb.reshape(2 * hv, 1))


DN_TB = 128
DN_HB = 16
CONV_PAD = 8


def _conv_stage(x_ref, buf_ref, first):
    tb = x_ref.shape[0]

    @pl.when(first)
    def _():
        buf_ref[0:CONV_PAD, :] = jnp.zeros((CONV_PAD, buf_ref.shape[1]), F32)

    buf_ref[CONV_PAD:CONV_PAD + tb, :] = x_ref[...]


def _conv_silu(buf_ref, w_ref, cols, tb):
    y = buf_ref[CONV_PAD:CONV_PAD + tb, cols] * w_ref[DN_CONV - 1:DN_CONV, cols]
    for j in range(1, DN_CONV):
        tap = DN_CONV - 1 - j
        y = y + buf_ref[CONV_PAD - j:CONV_PAD - j + tb, cols] * w_ref[tap:tap + 1, cols]
    return _silu(y)


def _conv_keep_tail(buf_ref, tb):
    buf_ref[0:CONV_PAD, :] = buf_ref[tb:tb + CONV_PAD, :]


def _hi_lo(a):
    hi = a.astype(BF16).astype(F32)
    return hi, a - hi


def _pair_lhs(x):
    hi, lo = _hi_lo(x)
    return jnp.concatenate([hi, lo], axis=1).astype(BF16)


def _pair_blockdiag(y, first_half):
    top = jnp.where(first_half, y, 0.0)
    return jnp.concatenate([top, y - top], axis=0)


def _pair_product(x, y, first_half):
    return _dot(x.astype(BF16), _pair_blockdiag(y, first_half).astype(BF16))


def _pair_product_hp(x, y, first_half):
    width = x.shape[1]
    y_hi, y_lo = _hi_lo(y)
    bd = jnp.concatenate([_pair_blockdiag(y_hi, first_half),
                          _pair_blockdiag(y_lo, first_half)], axis=1).astype(BF16)
    out = _dot(_pair_lhs(x), jnp.concatenate([bd, bd], axis=0))
    return out[:, :width] + out[:, width:]


def _delta_kernel(q_ref, k_ref, v_ref, z_ref, wq_ref, wk_ref, wv_ref, bg_ref, gc_ref, gct_ref,
                  ng_ref, o_ref, state_ref, qbuf_ref, kbuf_ref, vbuf_ref):
    tb = q_ref.shape[0]
    hv = DN_V_HEADS
    c = DN_CHUNK
    dk = DN_HEAD_DIM
    rep = DN_V_HEADS // DN_QK_HEADS
    group = pl.program_id(1)
    first = pl.program_id(2) == 0

    @pl.when(first)
    def _():
        state_ref[...] = jnp.zeros_like(state_ref)

    _conv_stage(q_ref, qbuf_ref, first)
    _conv_stage(k_ref, kbuf_ref, first)
    _conv_stage(v_ref, vbuf_ref, first)

    assert tb == 2 * c
    ri = lax.broadcasted_iota(jnp.int32, (c, 2 * c), 0)
    ci2 = lax.broadcasted_iota(jnp.int32, (c, 2 * c), 1)
    first_half = ci2 < c
    ci = jnp.where(first_half, ci2, ci2 - c)
    incl = ci <= ri
    strict = ci < ri
    eye = jnp.where(ci == ri, 1.0, 0.0)
    norm_g = ng_ref[...]
    n_qk = DN_HB // rep
    heads = range(DN_HB)

    bg = bg_ref[...]
    gcs = gc_ref[...]
    gct = gct_ref[...]
    head_lane = lax.broadcasted_iota(jnp.int32, (tb, 2 * hv), 1)
    head_row = lax.broadcasted_iota(jnp.int32, (hv, tb), 0)

    q_n, k_n = [], []
    for jq in range(n_qk):
        qcols = slice(jq * dk, (jq + 1) * dk)
        q_h = _conv_silu(qbuf_ref, wq_ref, qcols, tb)
        k_h = _conv_silu(kbuf_ref, wk_ref, qcols, tb)
        q_n.append(q_h * lax.rsqrt(jnp.sum(q_h * q_h, axis=-1, keepdims=True) + RMS_EPS)
                   * (dk ** -0.5))
        k_n.append(k_h * lax.rsqrt(jnp.sum(k_h * k_h, axis=-1, keepdims=True) + RMS_EPS))

    qk_pair, kk_pair = [], []
    for jq in range(n_qk):
        q0, q1 = q_n[jq][:c], q_n[jq][c:]
        k0, k1 = k_n[jq][:c], k_n[jq][c:]
        zero = jnp.zeros_like(k0)
        lhs = jnp.concatenate([jnp.concatenate([q0, q1], axis=1),
                               jnp.concatenate([k0, k1], axis=1)], axis=0).astype(BF16)
        rhs = jnp.concatenate([jnp.concatenate([k0, zero], axis=1),
                               jnp.concatenate([zero, k1], axis=1)], axis=0).astype(BF16)
        g = _dot(lhs, rhs, NT)
        qk_pair.append(g[:c])
        kk_pair.append(g[c:])

    a_pair, attn_bf, kbg_bf, q_dec, k_dec, g_last, betas = [], [], [], [], [], [], []
    for hl in heads:
        jq = hl // rep
        head = group * DN_HB + hl
        beta_col = jnp.sum(jnp.where(head_lane == head, bg, 0.0), axis=-1, keepdims=True)
        betas.append(beta_col)
        gc_col = jnp.sum(jnp.where(head_lane == head + hv, gcs, 0.0), axis=-1, keepdims=True)
        gc_row = jnp.sum(jnp.where(head_row == head, gct, 0.0), axis=0, keepdims=True)
        beta_sel = jnp.where(first_half, beta_col[:c], beta_col[c:])
        gc_sel = jnp.where(first_half, gc_col[:c], gc_col[c:])
        decay = jnp.exp(jnp.where(incl, gc_sel - gc_row, -jnp.inf))
        a_pair.append(jnp.where(strict, kk_pair[jq] * beta_sel * decay, 0.0))
        attn_bf.append((qk_pair[jq] * decay).astype(BF16))
        egc = jnp.exp(gc_col)
        kbg_bf.append((k_n[jq] * beta_col * egc).astype(BF16))
        q_dec.append((q_n[jq] * egc).astype(BF16))
        gl = [gc_col[c - 1:c, :], gc_col[tb - 1:tb, :]]
        gl_rows = jnp.concatenate([jnp.broadcast_to(gl[0], (c, 1)),
                                   jnp.broadcast_to(gl[1], (c, 1))], axis=0)
        k_dec.append((k_n[jq] * jnp.exp(gl_rows - gc_col)).astype(BF16))
        g_last.append([jnp.exp(gl[0]), jnp.exp(gl[1])])

    vb_bf, z_gate = [], []

    def value_conv(hl):
        vcols = slice(hl * dk, (hl + 1) * dk)
        vb_bf.append((_conv_silu(vbuf_ref, wv_ref, vcols, tb) * betas[hl]).astype(BF16))

    def output_gate(hl):
        z_gate.append(norm_g * _silu(z_ref[:, hl * dk:(hl + 1) * dk]))

    independent = ([functools.partial(value_conv, hl) for hl in heads]
                   + [functools.partial(output_gate, hl) for hl in heads])
    n_slots = 2 * (int(math.log2(c)) - 1) + 2
    per_slot = -(-len(independent) // n_slots)

    def fill():
        for work in independent[:per_slot]:
            work()
        del independent[:per_slot]

    inv = [eye - jnp.where((ri // 2) == (ci // 2), a, 0.0) for a in a_pair]
    s = 2
    while s < c:
        off = ((ri // (2 * s)) == (ci // (2 * s))) & ((ri // s) != (ci // s))
        m1 = [_pair_product(jnp.where(off, a_pair[hl], 0.0), inv[hl], first_half)
              for hl in heads]
        fill()
        m2 = [_pair_product(inv[hl], m1[hl], first_half) for hl in heads]
        fill()
        inv = [inv[hl] - m2[hl] for hl in heads]
        s *= 2
    ax = [_pair_product_hp(a_pair[hl], inv[hl], first_half) for hl in heads]
    fill()
    resid = [eye - inv[hl] - ax[hl] for hl in heads]
    corr = [_pair_product(inv[hl], resid[hl], first_half) for hl in heads]
    fill()
    inv = [inv[hl] + corr[hl] for hl in heads]
    assert not independent

    u, w_bf = [], []
    zero_rows = jnp.zeros((c, 2 * dk), BF16)
    for hl in heads:
        lhs = _pair_lhs(inv[hl])
        rhs = jnp.concatenate([vb_bf[hl], kbg_bf[hl]], axis=1)
        r0, r1 = rhs[:c], rhs[c:]
        uw0 = _dot(lhs, jnp.concatenate([r0, zero_rows, r0, zero_rows], axis=0))
        uw1 = _dot(lhs, jnp.concatenate([zero_rows, r1, zero_rows, r1], axis=0))
        u.append([uw0[:, :dk], uw1[:, :dk]])
        w_bf.append([uw0[:, dk:].astype(BF16), uw1[:, dk:].astype(BF16)])

    states = [state_ref[hl] for hl in heads]
    zero_v = jnp.zeros((c, dk), BF16)
    for ch in range(2):
        rows = slice(ch * c, (ch + 1) * c)
        ws_qs = [_dot(jnp.concatenate([w_bf[hl][ch], q_dec[hl][rows]], axis=0),
                      states[hl].astype(BF16)) for hl in heads]
        v_new_bf = [(u[hl][ch] - ws_qs[hl][:c]).astype(BF16) for hl in heads]
        o_intra = []
        for hl in heads:
            v_rows = [v_new_bf[hl], zero_v] if ch == 0 else [zero_v, v_new_bf[hl]]
            o_intra.append(_dot(attn_bf[hl], jnp.concatenate(v_rows, axis=0)))
        states = [states[hl] * g_last[hl][ch] + _dot(k_dec[hl][rows], v_new_bf[hl], TN)
                  for hl in heads]
        for hl in heads:
            vcols = slice(hl * dk, (hl + 1) * dk)
            o_c = ws_qs[hl][c:] + o_intra[hl]
            o_n = (o_c * lax.rsqrt(jnp.mean(o_c * o_c, axis=-1, keepdims=True) + RMS_EPS)
                   * z_gate[hl][rows])
            o_ref[rows, vcols] = o_n.astype(o_ref.dtype)
    for hl in heads:
        state_ref[hl] = states[hl]
    _conv_keep_tail(qbuf_ref, tb)
    _conv_keep_tail(kbuf_ref, tb)
    _conv_keep_tail(vbuf_ref, tb)


def _delta(hcat, conv_w, bg, gcs, gct, norm_g, batch):
    t = hcat.shape[0]
    s = t // batch
    nt = s // DN_TB
    rep = DN_V_HEADS // DN_QK_HEADS
    wq = DN_HB // rep * DN_HEAD_DIM
    wv = DN_HB * DN_HEAD_DIM
    k_off = DN_QK_WIDTH // wq
    v_off = 2 * DN_QK_WIDTH // wv
    z_off = DN_CONV_DIM // wv
    hv2 = 2 * DN_V_HEADS
    row = lambda b, g, i: b * nt + i
    return pl.pallas_call(
        _delta_kernel,
        grid=(batch, DN_V_HEADS // DN_HB, nt),
        in_specs=[
            pl.BlockSpec((DN_TB, wq), lambda b, g, i: (row(b, g, i), g)),
            pl.BlockSpec((DN_TB, wq), lambda b, g, i: (row(b, g, i), k_off + g)),
            pl.BlockSpec((DN_TB, wv), lambda b, g, i: (row(b, g, i), v_off + g)),
            pl.BlockSpec((DN_TB, wv), lambda b, g, i: (row(b, g, i), z_off + g)),
            pl.BlockSpec((DN_CONV, wq), lambda b, g, i: (0, g)),
            pl.BlockSpec((DN_CONV, wq), lambda b, g, i: (0, k_off + g)),
            pl.BlockSpec((DN_CONV, wv), lambda b, g, i: (0, v_off + g)),
            pl.BlockSpec((DN_TB, hv2), lambda b, g, i: (row(b, g, i), 0)),
            pl.BlockSpec((DN_TB, hv2), lambda b, g, i: (row(b, g, i), 0)),
            pl.BlockSpec((DN_V_HEADS, DN_TB), lambda b, g, i: (1, row(b, g, i))),
            pl.BlockSpec((1, DN_HEAD_DIM), lambda b, g, i: (0, 0)),
        ],
        out_specs=pl.BlockSpec((DN_TB, wv), lambda b, g, i: (row(b, g, i), g)),
        out_shape=jax.ShapeDtypeStruct((t, DN_V_WIDTH), BF16),
        scratch_shapes=[
            pltpu.VMEM((DN_HB, DN_HEAD_DIM, DN_HEAD_DIM), F32),
            pltpu.VMEM((CONV_PAD + DN_TB, wq), F32),
            pltpu.VMEM((CONV_PAD + DN_TB, wq), F32),
            pltpu.VMEM((CONV_PAD + DN_TB, wv), F32),
        ],
        compiler_params=_params("parallel", "parallel", "arbitrary"),
        name="delta_rule",
    )(hcat, hcat, hcat, hcat, conv_w, conv_w, conv_w, bg, gcs, gct,
      norm_g.reshape(1, DN_HEAD_DIM))


def _rope_tables(s):
    half = HEAD_DIM // 2
    inv = jnp.exp(-math.log(ROPE_THETA) * jnp.arange(half, dtype=F32) * (2.0 / HEAD_DIM))
    ang = jnp.arange(s, dtype=F32)[:, None] * inv[None, :]
    cos, sin = jnp.cos(ang), jnp.sin(ang)
    return jnp.concatenate([cos, cos], axis=-1), jnp.concatenate([-sin, sin], axis=-1)


def kernel(x, ffn1_w_gate, ffn1_w_up, ffn1_w_down, ffn2_w_gate, ffn2_w_up, ffn2_w_down, ln_g, ln_b, ab_w_in, ab_gmlp_ln_g, ab_gmlp_ln_b, ab_gmlp_w_s, ab_gmlp_b_s, ab_w_out, dn_w_in, dn_conv_w, dn_a_log, dn_dt_bias, dn_norm_g, dn_w_out):
    batch, s, d = x.shape
    t = batch * s
    bf = lambda w: w.astype(BF16)
    xf = x.reshape(t, d)
    cos_full, sin_signed = _rope_tables(s)
    ab_in, ab_out, dn_out = bf(ab_w_in), bf(ab_w_out), bf(dn_w_out)
    dn_in_t = bf(jnp.swapaxes(dn_w_in, 1, 2))

    ffn_f32 = {1: (ffn1_w_gate, ffn1_w_up, ffn1_w_down), 2: (ffn2_w_gate, ffn2_w_up, ffn2_w_down)}
    ffn_bf16 = {}
    pending = [(half, layer) for layer in range(DEPTH) for half in (1, 2)]

    def cast_now(key):
        half, layer = key
        ffn_bf16[key] = tuple(bf(w[layer]) for w in ffn_f32[half])

    def next_cast_jobs():
        if not pending:
            return None, ()
        half, layer = key = pending.pop(0)
        wg, wu, wd = ffn_f32[half]
        return key, ((wg, layer), (wu, layer), (wd.reshape(wg.shape), layer))

    def finish_cast_jobs(key, results):
        if key is not None:
            wg, wu, wd = results
            ffn_bf16[key] = (wg, wu, wd.reshape(wd.shape[1], wd.shape[0]))

    def half_step(xf, half, layer, slot):
        key = (half, layer)
        if key not in ffn_bf16:
            pending.remove(key)
            cast_now(key)
        return _ffn_ln(xf, *ffn_bf16[key], ln_g[layer, slot], ln_b[layer, slot])

    cast_now(pending.pop(0))
    for i in range(DEPTH):
        xf = half_step(xf, 1, i, 0)
        j = i // 2
        if i % 2 == 0:
            key, jobs = next_cast_jobs()
            hcat, done = _matmul(xf, ab_in[j:j + 1], AB_IN, AB_TN, cast_jobs=jobs)
            finish_cast_jobs(key, done)
            key, jobs = next_cast_jobs()
            a_out, done = _moba(hcat, cos_full, sin_signed, batch, cast_jobs=jobs)
            finish_cast_jobs(key, done)
            g_out = _gmlp(hcat, ab_gmlp_ln_g[j], ab_gmlp_ln_b[j], ab_gmlp_w_s[j], ab_gmlp_b_s[j])
            mix, w_out = [a_out, g_out], ab_out[j:j + 1]
        else:
            key, jobs = next_cast_jobs()
            hcat, done = _matmul(xf, dn_in_t[j:j + 1], DN_MAIN, DN_TN, w_is_transposed=True,
                                 cast_jobs=jobs)
            finish_cast_jobs(key, done)
            bg, gcs, gct = _dn_gates(xf, dn_in_t[j, DN_MAIN:], dn_a_log[j], dn_dt_bias[j])
            mix = [_delta(hcat, dn_conv_w[j], bg, gcs, gct, dn_norm_g[j], batch)]
            w_out = dn_out[j:j + 1]
        xf = _proj_ln(mix, w_out, xf, ln_g[i, 1], ln_b[i, 1])
        xf = half_step(xf, 2, i, 2)
    return xf.reshape(batch, s, d)
```

```python
import functools
import math

import jax
import jax.numpy as jnp
from jax import lax
from jax.experimental import pallas as pl
from jax.experimental.pallas import tpu as pltpu

F32 = jnp.float32
BF16 = jnp.bfloat16

D_MODEL = 2048
SEQ = 2048
DEPTH = 2
HEAD_DIM = 128
A_HEADS = 8
A_WIDTH = A_HEADS * HEAD_DIM
MOBA_BLOCK = 256
MOBA_TOPK = 3
ROPE_THETA = 10000.0
G_GROUPS = 8
G_DIM = 128
G_WIDTH = G_GROUPS * G_DIM
GMLP_CHUNK = 128
AB_IN = 3 * A_WIDTH + 2 * G_WIDTH
DN_QK_HEADS = 16
DN_V_HEADS = 32
DN_HEAD_DIM = 128
DN_QK_WIDTH = DN_QK_HEADS * DN_HEAD_DIM
DN_V_WIDTH = DN_V_HEADS * DN_HEAD_DIM
DN_CONV_DIM = 2 * DN_QK_WIDTH + DN_V_WIDTH
DN_MAIN = DN_CONV_DIM + DN_V_WIDTH
DN_CONV = 4
DN_CHUNK = 64
DEEPNORM_ALPHA = (2 * DEPTH) ** 0.25
LN_EPS = 1e-5
RMS_EPS = 1e-6
NEG_INF = -1e30

SUBLANES = 8
VMEM_LIMIT_BYTES = 60 * 1024 * 1024

NN = (((1,), (0,)), ((), ()))
NT = (((1,), (1,)), ((), ()))
TN = (((0,), (0,)), ((), ()))


def _dot(a, b, dims=NN):
    return lax.dot_general(a, b, dims, preferred_element_type=F32)


def _split2(a):
    hi = a.astype(BF16)
    lo = (a - hi.astype(F32)).astype(BF16)
    return hi, lo


def _dot_hp(a, b, dims=NN):
    a_hi, a_lo = _split2(a)
    b_hi, b_lo = _split2(b)
    return _dot(a_hi, b_hi, dims) + (_dot(a_hi, b_lo, dims) + _dot(a_lo, b_hi, dims))


def _dot_exact_lhs(a_bf16, b, dims=NN):
    b0 = b.astype(BF16)
    r1 = b - b0.astype(F32)
    b1 = r1.astype(BF16)
    b2 = (r1 - b1.astype(F32)).astype(BF16)
    return _dot(a_bf16, b0, dims) + (_dot(a_bf16, b1, dims) + _dot(a_bf16, b2, dims))


def _dot_exact_rhs(a, b_bf16, dims=NN):
    a0 = a.astype(BF16)
    r1 = a - a0.astype(F32)
    a1 = r1.astype(BF16)
    a2 = (r1 - a1.astype(F32)).astype(BF16)
    return _dot(a0, b_bf16, dims) + (_dot(a1, b_bf16, dims) + _dot(a2, b_bf16, dims))


def _layer_norm(y, g, b):
    mu = jnp.mean(y, axis=-1, keepdims=True)
    yc = y - mu
    var = jnp.mean(yc * yc, axis=-1, keepdims=True)
    return yc * lax.rsqrt(var + LN_EPS) * g + b


def _silu(x):
    return x * jax.nn.sigmoid(x)


def _params(*sem):
    return pltpu.CompilerParams(dimension_semantics=sem, vmem_limit_bytes=VMEM_LIMIT_BYTES)


FFN_ROWS = 512


def _ffn_kernel(x_ref, wg_ref, wu_ref, wd_ref, g_ref, b_ref, o_ref, xb_ref):
    j = pl.program_id(1)

    @pl.when(j == 0)
    def _():
        xb_ref[...] = x_ref[...].astype(BF16)
        o_ref[...] = jnp.zeros_like(o_ref)

    for r in range(0, o_ref.shape[0], FFN_ROWS):
        rows = slice(r, r + FFN_ROWS)
        xb = xb_ref[rows, :]
        gate = _dot(xb, wg_ref[...])
        up = _dot(xb, wu_ref[...])
        h = (_silu(gate) * up).astype(BF16)
        o_ref[rows, :] += _dot(h, wd_ref[...])

    @pl.when(j == pl.num_programs(1) - 1)
    def _():
        for r in range(0, o_ref.shape[0], FFN_ROWS):
            rows = slice(r, r + FFN_ROWS)
            y = DEEPNORM_ALPHA * x_ref[rows, :] + 0.5 * o_ref[rows, :]
            o_ref[rows, :] = _layer_norm(y, g_ref[...], b_ref[...])


def _ffn_ln(x, wg, wu, wd, ln_g, ln_b, tm=1024, tf=512):
    t, d = x.shape
    dff = wg.shape[1]
    return pl.pallas_call(
        _ffn_kernel,
        grid=(t // tm, dff // tf),
        in_specs=[
            pl.BlockSpec((tm, d), lambda i, j: (i, 0)),
            pl.BlockSpec((d, tf), lambda i, j: (0, j)),
            pl.BlockSpec((d, tf), lambda i, j: (0, j)),
            pl.BlockSpec((tf, d), lambda i, j: (j, 0)),
            pl.BlockSpec((1, d), lambda i, j: (0, 0)),
            pl.BlockSpec((1, d), lambda i, j: (0, 0)),
        ],
        out_specs=pl.BlockSpec((tm, d), lambda i, j: (i, 0)),
        out_shape=jax.ShapeDtypeStruct((t, d), F32),
        scratch_shapes=[pltpu.VMEM((tm, d), BF16)],
        compiler_params=_params("parallel", "arbitrary"),
        name="ffn_ln",
    )(x, wg, wu, wd, ln_g.reshape(1, d), ln_b.reshape(1, d))


def _cast_job_specs(jobs, grid):
    n_steps = math.prod(grid)
    strides = [math.prod(grid[a + 1:]) for a in range(len(grid))]
    step = lambda *ids: sum(i * s for i, s in zip(ids, strides))
    in_specs, out_specs, out_shapes = [], [], []
    for w, layer in jobs:
        _, r, c = w.shape
        slab = r // n_steps
        assert slab * n_steps == r and slab % (2 * SUBLANES) == 0
        in_specs.append(pl.BlockSpec((None, slab, c),
                                     lambda *ids, layer=layer: (layer, step(*ids), 0)))
        out_specs.append(pl.BlockSpec((slab, c), lambda *ids: (step(*ids), 0)))
        out_shapes.append(jax.ShapeDtypeStruct((r, c), BF16))
    return in_specs, out_specs, out_shapes


def _run_cast_jobs(src_refs, dst_refs):
    for src, dst in zip(src_refs, dst_refs):
        dst[...] = src[...].astype(BF16)


def _matmul_kernel(x_ref, w_ref, o_ref, xb_ref, *, w_is_transposed):
    @pl.when(pl.program_id(1) == 0)
    def _():
        xb_ref[...] = x_ref[...].astype(BF16)

    o_ref[...] = _dot(xb_ref[...], w_ref[...], NT if w_is_transposed else NN)


def _matmul(x, w, n, w_is_transposed=False, tm=1024, tn=1024):
    t, k = x.shape
    if w_is_transposed:
        w_spec = pl.BlockSpec((None, tn, k), lambda i, j: (0, j, 0))
    else:
        w_spec = pl.BlockSpec((None, k, tn), lambda i, j: (0, 0, j))
    return pl.pallas_call(
        functools.partial(_matmul_kernel, w_is_transposed=w_is_transposed),
        grid=(t // tm, n // tn),
        in_specs=[pl.BlockSpec((tm, k), lambda i, j: (i, 0)), w_spec],
        out_specs=pl.BlockSpec((tm, tn), lambda i, j: (i, j)),
        out_shape=jax.ShapeDtypeStruct((t, n), F32),
        scratch_shapes=[pltpu.VMEM((tm, k), BF16)],
        compiler_params=_params("parallel", "arbitrary"),
        name="in_proj",
    )(x, w)


PROJ_ROWS = 256


def _proj_ln_kernel(*refs, n_in):
    a_refs = refs[:n_in]
    w_ref, x_ref, g_ref, b_ref, o_ref = refs[n_in:]
    for r in range(0, o_ref.shape[0], PROJ_ROWS):
        rows = slice(r, r + PROJ_ROWS)
        acc = None
        row = 0
        for a_ref in a_refs:
            kc = a_ref.shape[1]
            part = _dot(a_ref[rows, :], w_ref[row:row + kc, :])
            acc = part if acc is None else acc + part
            row += kc
        y = DEEPNORM_ALPHA * x_ref[rows, :] + acc
        o_ref[rows, :] = _layer_norm(y, g_ref[...], b_ref[...])


def _proj_ln(a_parts, w, x, ln_g, ln_b, tm=512):
    t, d = x.shape
    k = w.shape[1]
    in_specs = [pl.BlockSpec((tm, a.shape[1]), lambda i: (i, 0)) for a in a_parts]
    in_specs += [
        pl.BlockSpec((None, k, d), lambda i: (0, 0, 0), pipeline_mode=pl.Buffered(1)),
        pl.BlockSpec((tm, d), lambda i: (i, 0)),
        pl.BlockSpec((1, d), lambda i: (0, 0)),
        pl.BlockSpec((1, d), lambda i: (0, 0)),
    ]
    return pl.pallas_call(
        functools.partial(_proj_ln_kernel, n_in=len(a_parts)),
        grid=(t // tm,),
        in_specs=in_specs,
        out_specs=pl.BlockSpec((tm, d), lambda i: (i, 0)),
        out_shape=jax.ShapeDtypeStruct((t, d), F32),
        compiler_params=_params("parallel"),
        name="proj_ln",
    )(*a_parts, w, x, ln_g.reshape(1, d), ln_b.reshape(1, d))


def _rope(x, cos_full, sin_signed):
    return x * cos_full + pltpu.roll(x, HEAD_DIM // 2, 1) * sin_signed


def _moba_kernel(q_ref, k_ref, v_ref, cos_ref, sin_ref, *rest, n_cast):
    o_ref = rest[n_cast]
    _run_cast_jobs(rest[:n_cast], rest[n_cast + 1:])
    s = q_ref.shape[0]
    nb = s // MOBA_BLOCK
    blk = MOBA_BLOCK
    cos_full = cos_ref[...]
    sin_signed = sin_ref[...]
    qr = _rope(q_ref[...], cos_full, sin_signed)
    kr = _rope(k_ref[...], cos_full, sin_signed)

    row = lax.broadcasted_iota(jnp.int32, (HEAD_DIM, s), 0)
    col = lax.broadcasted_iota(jnp.int32, (HEAD_DIM, s), 1)
    avg = jnp.where(col // blk == row, 1.0 / blk, 0.0).astype(BF16)
    k_mean = _dot_exact_lhs(avg, kr)
    gate_t = _dot_hp(k_mean, qr, NT)

    qb = qr.astype(BF16)
    kb = kr.astype(BF16)
    vb = v_ref[...].astype(BF16)
    scale = HEAD_DIM ** -0.5

    assert nb <= SUBLANES and blk == 2 * HEAD_DIM
    blk_row = lax.broadcasted_iota(jnp.int32, (SUBLANES, blk), 0)
    qpos = lax.broadcasted_iota(jnp.int32, (blk, blk), 0)
    kpos = lax.broadcasted_iota(jnp.int32, (blk, blk), 1)
    causal = kpos <= qpos
    eye_q = jnp.where(kpos == qpos, 1.0, 0.0).astype(BF16)
    pad_rows = jnp.zeros((HEAD_DIM - SUBLANES, blk), F32)

    for i in range(nb):
        rows = slice(i * blk, (i + 1) * blk)
        qi = qb[rows]
        sel = None
        if i > MOBA_TOPK:
            g = gate_t[0:SUBLANES, rows]
            rank = jnp.zeros((SUBLANES, blk), F32)
            for m_blk in range(i):
                gm = g[m_blk:m_blk + 1, :]
                beats = (gm > g) | ((gm == g) & (m_blk < blk_row))
                rank = rank + jnp.where(beats, 1.0, 0.0)
            sel_t = jnp.where((rank < float(MOBA_TOPK)) & (blk_row < i), 1.0, 0.0)
            sel_t = jnp.concatenate([sel_t, pad_rows], axis=0).astype(BF16)
            sel = _dot(eye_q, sel_t, NT)
        scores = []
        for n in range(i + 1):
            sn = _dot(qi, kb[n * blk:(n + 1) * blk], NT) * scale
            if n == i:
                sn = jnp.where(causal, sn, NEG_INF)
            elif sel is not None:
                selected = jnp.broadcast_to(sel[:, n:n + 1], sn.shape) > 0.5
                sn = jnp.where(selected, sn, NEG_INF)
            scores.append(sn)
        m_acc = None
        for sn in scores:
            folded = jnp.maximum(sn[:, :HEAD_DIM], sn[:, HEAD_DIM:])
            m_acc = folded if m_acc is None else jnp.maximum(m_acc, folded)
        m = m_acc.max(axis=-1, keepdims=True)
        l_acc = jnp.zeros((blk, HEAD_DIM), F32)
        acc = jnp.zeros((blk, HEAD_DIM), F32)
        for n, sn in enumerate(scores):
            p = jnp.exp(sn - m)
            l_acc = l_acc + (p[:, :HEAD_DIM] + p[:, HEAD_DIM:])
            acc = acc + _dot(p.astype(BF16), vb[n * blk:(n + 1) * blk])
        l = l_acc.sum(axis=-1, keepdims=True)
        o_ref[rows, :] = (acc / l).astype(o_ref.dtype)


def _moba(hcat, cos_full, sin_signed, batch, cast_jobs=()):
    t = hcat.shape[0]
    s = t // batch
    grid = (batch, A_HEADS)
    spec = lambda off: pl.BlockSpec((s, HEAD_DIM), lambda b, h: (b, off + h))
    tab = pl.BlockSpec((s, HEAD_DIM), lambda b, h: (0, 0), pipeline_mode=pl.Buffered(1))
    cast_in, cast_out, cast_shapes = _cast_job_specs(cast_jobs, grid)
    outs = pl.pallas_call(
        functools.partial(_moba_kernel, n_cast=len(cast_jobs)),
        grid=grid,
        in_specs=[spec(0), spec(A_HEADS), spec(2 * A_HEADS), tab, tab] + cast_in,
        out_specs=[pl.BlockSpec((s, HEAD_DIM), lambda b, h: (b, h))] + cast_out,
        out_shape=[jax.ShapeDtypeStruct((t, A_WIDTH), BF16)] + cast_shapes,
        compiler_params=_params("parallel", "parallel"),
        name="moba",
    )(hcat, hcat, hcat, cos_full, sin_signed, *[job[0] for job in cast_jobs])
    return outs[0], outs[1:]


def _gmlp_kernel(u_ref, v_ref, lng_ref, lnb_ref, ws_ref, bs_ref, o_ref):
    rows = u_ref.shape[0]
    tpos = lax.broadcasted_iota(jnp.int32, (GMLP_CHUNK, GMLP_CHUNK), 0)
    spos = lax.broadcasted_iota(jnp.int32, (GMLP_CHUNK, GMLP_CHUNK), 1)
    causal = spos <= tpos
    for g in range(G_GROUPS):
        cols = slice(g * G_DIM, (g + 1) * G_DIM)
        w = jnp.where(causal, ws_ref[g], 0.0).astype(BF16)
        bias = bs_ref[:, g:g + 1]
        ln_g = lng_ref[:, cols]
        ln_b = lnb_ref[:, cols]
        for c in range(rows // GMLP_CHUNK):
            r = slice(c * GMLP_CHUNK, (c + 1) * GMLP_CHUNK)
            vn = _layer_norm(jax.nn.gelu(v_ref[r, cols]), ln_g, ln_b)
            mixed = _dot(w, vn.astype(BF16)) + bias
            o_ref[r, cols] = (jax.nn.gelu(u_ref[r, cols]) * mixed).astype(o_ref.dtype)


def _gmlp(hcat, ln_g, ln_b, w_s, b_s, rows=512):
    t = hcat.shape[0]
    u_blk = 3 * A_WIDTH // G_WIDTH
    return pl.pallas_call(
        _gmlp_kernel,
        grid=(t // rows,),
        in_specs=[
            pl.BlockSpec((rows, G_WIDTH), lambda i: (i, u_blk)),
            pl.BlockSpec((rows, G_WIDTH), lambda i: (i, u_blk + 1)),
            pl.BlockSpec((1, G_WIDTH), lambda i: (0, 0)),
            pl.BlockSpec((1, G_WIDTH), lambda i: (0, 0)),
            pl.BlockSpec((G_GROUPS, GMLP_CHUNK, GMLP_CHUNK), lambda i: (0, 0, 0)),
            pl.BlockSpec((GMLP_CHUNK, G_GROUPS), lambda i: (0, 0)),
        ],
        out_specs=pl.BlockSpec((rows, G_WIDTH), lambda i: (i, 0)),
        out_shape=jax.ShapeDtypeStruct((t, G_WIDTH), BF16),
        compiler_params=_params("parallel"),
        name="gmlp",
    )(hcat, hcat, ln_g.reshape(1, G_WIDTH), ln_b.reshape(1, G_WIDTH), w_s, b_s.T)


GATE_BLK = 128


def _softplus(x):
    return jnp.maximum(x, 0.0) + jnp.log1p(jnp.exp(-jnp.abs(x)))


def _dn_gate_kernel(x_ref, w_ref, wt_ref, ng_ref, dtb_ref, ngc_ref, dtbc_ref,
                    bg_ref, gc_ref, gct_ref):
    s = x_ref.shape[0]
    hv = DN_V_HEADS
    x = x_ref[...].astype(BF16)
    ba = _dot(x, w_ref[...])
    bat = _dot(wt_ref[...], x, NT)
    lane = lax.broadcasted_iota(jnp.int32, (s, 2 * hv), 1)
    g = ng_ref[...] * _softplus(ba + dtb_ref[...])
    bg = jnp.where(lane < hv, jax.nn.sigmoid(ba), g)
    bg_ref[...] = bg
    gt = ngc_ref[...] * _softplus(bat + dtbc_ref[...])

    r = lax.broadcasted_iota(jnp.int32, (GATE_BLK, GATE_BLK), 0)
    c = lax.broadcasted_iota(jnp.int32, (GATE_BLK, GATE_BLK), 1)
    same = (r // DN_CHUNK) == (c // DN_CHUNK)
    lower = jnp.where(same & (c <= r), 1.0, 0.0).astype(BF16)
    upper = jnp.where(same & (r <= c), 1.0, 0.0).astype(BF16)
    for i in range(s // GATE_BLK):
        sl = slice(i * GATE_BLK, (i + 1) * GATE_BLK)
        gc_ref[sl, :] = _dot_exact_lhs(lower, bg[sl, :])
        gct_ref[:, sl] = _dot_exact_rhs(gt[:, sl], upper)


def _dn_gates(x, w_ba_t, a_log, dt_bias, rows=512):
    t, d = x.shape
    s = rows
    hv = DN_V_HEADS
    neg_rate = -jnp.exp(a_log.astype(F32))
    zeros = jnp.zeros((hv,), F32)
    ng = jnp.concatenate([zeros, neg_rate]).reshape(1, 2 * hv)
    dtb = jnp.concatenate([zeros, dt_bias.astype(F32)]).reshape(1, 2 * hv)
    small = lambda shape: pl.BlockSpec(shape, lambda b: (0, 0))
    return pl.pallas_call(
        _dn_gate_kernel,
        grid=(t // rows,),
        in_specs=[
            pl.BlockSpec((s, d), lambda b: (b, 0)),
            small((d, 2 * hv)),
            small((2 * hv, d)),
            small((1, 2 * hv)),
            small((1, 2 * hv)),
            small((2 * hv, 1)),
            small((2 * hv, 1)),
        ],
        out_specs=[
            pl.BlockSpec((s, 2 * hv), lambda b: (b, 0)),
            pl.BlockSpec((s, 2 * hv), lambda b: (b, 0)),
            pl.BlockSpec((2 * hv, s), lambda b: (0, b)),
        ],
        out_shape=[
            jax.ShapeDtypeStruct((t, 2 * hv), F32),
            jax.ShapeDtypeStruct((t, 2 * hv), F32),
            jax.ShapeDtypeStruct((2 * hv, t), F32),
        ],
        compiler_params=_params("parallel"),
        name="dn_gates",
    )(x, w_ba_t.T, w_ba_t, ng, dtb, ng.reshape(2 * hv, 1), dtb.reshape(2 * hv, 1))


DN_TB = 128
DN_HB = 16
CONV_PAD = 8


def _conv_stage(x_ref, buf_ref, first):
    tb = x_ref.shape[0]

    @pl.when(first)
    def _():
        buf_ref[0:CONV_PAD, :] = jnp.zeros((CONV_PAD, buf_ref.shape[1]), F32)

    buf_ref[CONV_PAD:CONV_PAD + tb, :] = x_ref[...]


def _conv_silu(buf_ref, w_ref, cols, tb):
    y = buf_ref[CONV_PAD:CONV_PAD + tb, cols] * w_ref[DN_CONV - 1:DN_CONV, cols]
    for j in range(1, DN_CONV):
        tap = DN_CONV - 1 - j
        y = y + buf_ref[CONV_PAD - j:CONV_PAD - j + tb, cols] * w_ref[tap:tap + 1, cols]
    return _silu(y)


def _conv_keep_tail(buf_ref, tb):
    buf_ref[0:CONV_PAD, :] = buf_ref[tb:tb + CONV_PAD, :]


def _hi_lo(a):
    hi = a.astype(BF16).astype(F32)
    return hi, a - hi


def _pair_lhs(x):
    hi, lo = _hi_lo(x)
    return jnp.concatenate([hi, lo], axis=1).astype(BF16)


def _pair_blockdiag(y, first_half):
    top = jnp.where(first_half, y, 0.0)
    return jnp.concatenate([top, y - top], axis=0)


def _pair_product(x, y, first_half):
    return _dot(x.astype(BF16), _pair_blockdiag(y, first_half).astype(BF16))


def _pair_product_hp(x, y, first_half):
    width = x.shape[1]
    y_hi, y_lo = _hi_lo(y)
    bd = jnp.concatenate([_pair_blockdiag(y_hi, first_half),
                          _pair_blockdiag(y_lo, first_half)], axis=1).astype(BF16)
    out = _dot(_pair_lhs(x), jnp.concatenate([bd, bd], axis=0))
    return out[:, :width] + out[:, width:]


def _delta_kernel(q_ref, k_ref, v_ref, z_ref, wq_ref, wk_ref, wv_ref, bg_ref, gc_ref, gct_ref,
                  ng_ref, o_ref, state_ref, qbuf_ref, kbuf_ref, vbuf_ref):
    tb = q_ref.shape[0]
    hv = DN_V_HEADS
    c = DN_CHUNK
    dk = DN_HEAD_DIM
    rep = DN_V_HEADS // DN_QK_HEADS
    group = pl.program_id(1)
    first = pl.program_id(2) == 0

    @pl.when(first)
    def _():
        state_ref[...] = jnp.zeros_like(state_ref)

    _conv_stage(q_ref, qbuf_ref, first)
    _conv_stage(k_ref, kbuf_ref, first)
    _conv_stage(v_ref, vbuf_ref, first)

    assert tb == 2 * c
    ri = lax.broadcasted_iota(jnp.int32, (c, 2 * c), 0)
    ci2 = lax.broadcasted_iota(jnp.int32, (c, 2 * c), 1)
    first_half = ci2 < c
    ci = jnp.where(first_half, ci2, ci2 - c)
    incl = ci <= ri
    strict = ci < ri
    eye = jnp.where(ci == ri, 1.0, 0.0)
    norm_g = ng_ref[...]
    n_qk = DN_HB // rep
    heads = range(DN_HB)

    bg = bg_ref[...]
    gcs = gc_ref[...]
    gct = gct_ref[...]
    head_lane = lax.broadcasted_iota(jnp.int32, (tb, 2 * hv), 1)
    head_row = lax.broadcasted_iota(jnp.int32, (hv, tb), 0)

    q_n, k_n = [], []
    for jq in range(n_qk):
        qcols = slice(jq * dk, (jq + 1) * dk)
        q_h = _conv_silu(qbuf_ref, wq_ref, qcols, tb)
        k_h = _conv_silu(kbuf_ref, wk_ref, qcols, tb)
        q_n.append(q_h * lax.rsqrt(jnp.sum(q_h * q_h, axis=-1, keepdims=True) + RMS_EPS)
                   * (dk ** -0.5))
        k_n.append(k_h * lax.rsqrt(jnp.sum(k_h * k_h, axis=-1, keepdims=True) + RMS_EPS))

    qk_pair, kk_pair = [], []
    for jq in range(n_qk):
        q0, q1 = q_n[jq][:c], q_n[jq][c:]
        k0, k1 = k_n[jq][:c], k_n[jq][c:]
        zero = jnp.zeros_like(k0)
        lhs = jnp.concatenate([jnp.concatenate([q0, q1], axis=1),
                               jnp.concatenate([k0, k1], axis=1)], axis=0).astype(BF16)
        rhs = jnp.concatenate([jnp.concatenate([k0, zero], axis=1),
                               jnp.concatenate([zero, k1], axis=1)], axis=0).astype(BF16)
        g = _dot(lhs, rhs, NT)
        qk_pair.append(g[:c])
        kk_pair.append(g[c:])

    a_pair, attn_bf, kbg_bf, q_dec, k_dec, g_last, betas = [], [], [], [], [], [], []
    for hl in heads:
        jq = hl // rep
        head = group * DN_HB + hl
        beta_col = jnp.sum(jnp.where(head_lane == head, bg, 0.0), axis=-1, keepdims=True)
        betas.append(beta_col)
        gc_col = jnp.sum(jnp.where(head_lane == head + hv, gcs, 0.0), axis=-1, keepdims=True)
        gc_row = jnp.sum(jnp.where(head_row == head, gct, 0.0), axis=0, keepdims=True)
        beta_sel = jnp.where(first_half, beta_col[:c], beta_col[c:])
        gc_sel = jnp.where(first_half, gc_col[:c], gc_col[c:])
        decay = jnp.exp(jnp.where(incl, gc_sel - gc_row, -jnp.inf))
        a_pair.append(jnp.where(strict, kk_pair[jq] * beta_sel * decay, 0.0))
        attn_bf.append((qk_pair[jq] * decay).astype(BF16))
        egc = jnp.exp(gc_col)
        kbg_bf.append((k_n[jq] * beta_col * egc).astype(BF16))
        q_dec.append((q_n[jq] * egc).astype(BF16))
        gl = [gc_col[c - 1:c, :], gc_col[tb - 1:tb, :]]
        gl_rows = jnp.concatenate([jnp.broadcast_to(gl[0], (c, 1)),
                                   jnp.broadcast_to(gl[1], (c, 1))], axis=0)
        k_dec.append((k_n[jq] * jnp.exp(gl_rows - gc_col)).astype(BF16))
        g_last.append([jnp.exp(gl[0]), jnp.exp(gl[1])])

    vb_bf, z_gate = [], []

    def value_conv(hl):
        vcols = slice(hl * dk, (hl + 1) * dk)
        vb_bf.append((_conv_silu(vbuf_ref, wv_ref, vcols, tb) * betas[hl]).astype(BF16))

    def output_gate(hl):
        z_gate.append(norm_g * _silu(z_ref[:, hl * dk:(hl + 1) * dk]))

    independent = ([functools.partial(value_conv, hl) for hl in heads]
                   + [functools.partial(output_gate, hl) for hl in heads])
    n_slots = 2 * (int(math.log2(c)) - 1) + 2
    per_slot = -(-len(independent) // n_slots)

    def fill():
        for work in independent[:per_slot]:
            work()
        del independent[:per_slot]

    inv = [eye - jnp.where((ri // 2) == (ci // 2), a, 0.0) for a in a_pair]
    s = 2
    while s < c:
        off = ((ri // (2 * s)) == (ci // (2 * s))) & ((ri // s) != (ci // s))
        m1 = [_pair_product(jnp.where(off, a_pair[hl], 0.0), inv[hl], first_half)
              for hl in heads]
        fill()
        m2 = [_pair_product(inv[hl], m1[hl], first_half) for hl in heads]
        fill()
        inv = [inv[hl] - m2[hl] for hl in heads]
        s *= 2
    ax = [_pair_product_hp(a_pair[hl], inv[hl], first_half) for hl in heads]
    fill()
    resid = [eye - inv[hl] - ax[hl] for hl in heads]
    corr = [_pair_product(inv[hl], resid[hl], first_half) for hl in heads]
    fill()
    inv = [inv[hl] + corr[hl] for hl in heads]
    assert not independent

    u, w_bf = [], []
    zero_rows = jnp.zeros((c, 2 * dk), BF16)
    for hl in heads:
        lhs = _pair_lhs(inv[hl])
        rhs = jnp.concatenate([vb_bf[hl], kbg_bf[hl]], axis=1)
        r0, r1 = rhs[:c], rhs[c:]
        uw0 = _dot(lhs, jnp.concatenate([r0, zero_rows, r0, zero_rows], axis=0))
        uw1 = _dot(lhs, jnp.concatenate([zero_rows, r1, zero_rows, r1], axis=0))
        u.append([uw0[:, :dk], uw1[:, :dk]])
        w_bf.append([uw0[:, dk:].astype(BF16), uw1[:, dk:].astype(BF16)])

    states = [state_ref[hl] for hl in heads]
    zero_v = jnp.zeros((c, dk), BF16)
    for ch in range(2):
        rows = slice(ch * c, (ch + 1) * c)
        ws_qs = [_dot(jnp.concatenate([w_bf[hl][ch], q_dec[hl][rows]], axis=0),
                      states[hl].astype(BF16)) for hl in heads]
        v_new_bf = [(u[hl][ch] - ws_qs[hl][:c]).astype(BF16) for hl in heads]
        o_intra = []
        for hl in heads:
            v_rows = [v_new_bf[hl], zero_v] if ch == 0 else [zero_v, v_new_bf[hl]]
            o_intra.append(_dot(attn_bf[hl], jnp.concatenate(v_rows, axis=0)))
        states = [states[hl] * g_last[hl][ch] + _dot(k_dec[hl][rows], v_new_bf[hl], TN)
                  for hl in heads]
        for hl in heads:
            vcols = slice(hl * dk, (hl + 1) * dk)
            o_c = ws_qs[hl][c:] + o_intra[hl]
            o_n = (o_c * lax.rsqrt(jnp.mean(o_c * o_c, axis=-1, keepdims=True) + RMS_EPS)
                   * z_gate[hl][rows])
            o_ref[rows, vcols] = o_n.astype(o_ref.dtype)
    for hl in heads:
        state_ref[hl] = states[hl]
    _conv_keep_tail(qbuf_ref, tb)
    _conv_keep_tail(kbuf_ref, tb)
    _conv_keep_tail(vbuf_ref, tb)


def _delta(hcat, conv_w, bg, gcs, gct, norm_g, batch):
    t = hcat.shape[0]
    s = t // batch
    nt = s // DN_TB
    rep = DN_V_HEADS // DN_QK_HEADS
    wq = DN_HB // rep * DN_HEAD_DIM
    wv = DN_HB * DN_HEAD_DIM
    k_off = DN_QK_WIDTH // wq
    v_off = 2 * DN_QK_WIDTH // wv
    z_off = DN_CONV_DIM // wv
    hv2 = 2 * DN_V_HEADS
    row = lambda b, g, i: b * nt + i
    return pl.pallas_call(
        _delta_kernel,
        grid=(batch, DN_V_HEADS // DN_HB, nt),
        in_specs=[
            pl.BlockSpec((DN_TB, wq), lambda b, g, i: (row(b, g, i), g)),
            pl.BlockSpec((DN_TB, wq), lambda b, g, i: (row(b, g, i), k_off + g)),
            pl.BlockSpec((DN_TB, wv), lambda b, g, i: (row(b, g, i), v_off + g)),
            pl.BlockSpec((DN_TB, wv), lambda b, g, i: (row(b, g, i), z_off + g)),
            pl.BlockSpec((DN_CONV, wq), lambda b, g, i: (0, g)),
            pl.BlockSpec((DN_CONV, wq), lambda b, g, i: (0, k_off + g)),
            pl.BlockSpec((DN_CONV, wv), lambda b, g, i: (0, v_off + g)),
            pl.BlockSpec((DN_TB, hv2), lambda b, g, i: (row(b, g, i), 0)),
            pl.BlockSpec((DN_TB, hv2), lambda b, g, i: (row(b, g, i), 0)),
            pl.BlockSpec((DN_V_HEADS, DN_TB), lambda b, g, i: (1, row(b, g, i))),
            pl.BlockSpec((1, DN_HEAD_DIM), lambda b, g, i: (0, 0)),
        ],
        out_specs=pl.BlockSpec((DN_TB, wv), lambda b, g, i: (row(b, g, i), g)),
        out_shape=jax.ShapeDtypeStruct((t, DN_V_WIDTH), BF16),
        scratch_shapes=[
            pltpu.VMEM((DN_HB, DN_HEAD_DIM, DN_HEAD_DIM), F32),
            pltpu.VMEM((CONV_PAD + DN_TB, wq), F32),
            pltpu.VMEM((CONV_PAD + DN_TB, wq), F32),
            pltpu.VMEM((CONV_PAD + DN_TB, wv), F32),
        ],
        compiler_params=_params("parallel", "parallel", "arbitrary"),
        name="delta_rule",
    )(hcat, hcat, hcat, hcat, conv_w, conv_w, conv_w, bg, gcs, gct,
      norm_g.reshape(1, DN_HEAD_DIM))


def _rope_tables(s):
    half = HEAD_DIM // 2
    inv = jnp.exp(-math.log(ROPE_THETA) * jnp.arange(half, dtype=F32) * (2.0 / HEAD_DIM))
    ang = jnp.arange(s, dtype=F32)[:, None] * inv[None, :]
    cos, sin = jnp.cos(ang), jnp.sin(ang)
    return jnp.concatenate([cos, cos], axis=-1), jnp.concatenate([-sin, sin], axis=-1)


def kernel(x, ffn1_w_gate, ffn1_w_up, ffn1_w_down, ffn2_w_gate, ffn2_w_up, ffn2_w_down, ln_g, ln_b, ab_w_in, ab_gmlp_ln_g, ab_gmlp_ln_b, ab_gmlp_w_s, ab_gmlp_b_s, ab_w_out, dn_w_in, dn_conv_w, dn_a_log, dn_dt_bias, dn_norm_g, dn_w_out):
    batch, s, d = x.shape
    t = batch * s
    bf = lambda w: w.astype(BF16)
    xf = x.reshape(t, d)
    cos_full, sin_signed = _rope_tables(s)
    ab_in, ab_out, dn_out = bf(ab_w_in), bf(ab_w_out), bf(dn_w_out)
    dn_in_t = bf(jnp.swapaxes(dn_w_in, 1, 2))

    ffn_f32 = {1: (ffn1_w_gate, ffn1_w_up, ffn1_w_down), 2: (ffn2_w_gate, ffn2_w_up, ffn2_w_down)}
    ffn_bf16 = {}
    pending = [(half, layer) for layer in range(DEPTH) for half in (1, 2)]

    def cast_now(key):
        half, layer = key
        ffn_bf16[key] = tuple(bf(w[layer]) for w in ffn_f32[half])

    def take_cast_jobs():
        keys = list(pending)
        del pending[:]
        return keys, tuple((w, layer) for half, layer in keys for w in ffn_f32[half])

    def finish_cast_jobs(keys, results):
        for n, key in enumerate(keys):
            ffn_bf16[key] = tuple(results[3 * n:3 * n + 3])

    def half_step(xf, half, layer, slot):
        key = (half, layer)
        if key not in ffn_bf16:
            pending.remove(key)
            cast_now(key)
        return _ffn_ln(xf, *ffn_bf16[key], ln_g[layer, slot], ln_b[layer, slot])

    cast_now(pending.pop(0))
    for i in range(DEPTH):
        xf = half_step(xf, 1, i, 0)
        j = i // 2
        if i % 2 == 0:
            hcat = _matmul(xf, ab_in[j:j + 1], AB_IN)
            keys, jobs = take_cast_jobs()
            a_out, done = _moba(hcat, cos_full, sin_signed, batch, cast_jobs=jobs)
            finish_cast_jobs(keys, done)
            g_out = _gmlp(hcat, ab_gmlp_ln_g[j], ab_gmlp_ln_b[j], ab_gmlp_w_s[j], ab_gmlp_b_s[j])
            mix, w_out = [a_out, g_out], ab_out[j:j + 1]
        else:
            hcat = _matmul(xf, dn_in_t[j:j + 1], DN_MAIN, w_is_transposed=True)
            bg, gcs, gct = _dn_gates(xf, dn_in_t[j, DN_MAIN:], dn_a_log[j], dn_dt_bias[j])
            mix = [_delta(hcat, dn_conv_w[j], bg, gcs, gct, dn_norm_g[j], batch)]
            w_out = dn_out[j:j + 1]
        xf = _proj_ln(mix, w_out, xf, ln_g[i, 1], ln_b[i, 1])
        xf = half_step(xf, 2, i, 2)
    return xf.reshape(batch, s, d)
```

```python
import functools
import math

import jax
import jax.numpy as jnp
from jax import lax
from jax.experimental import pallas as pl
from jax.experimental.pallas import tpu as pltpu

F32 = jnp.float32
BF16 = jnp.bfloat16

D_MODEL = 2048
SEQ = 2048
DEPTH = 2
HEAD_DIM = 128
A_HEADS = 8
A_WIDTH = A_HEADS * HEAD_DIM
MOBA_BLOCK = 256
MOBA_TOPK = 3
ROPE_THETA = 10000.0
G_GROUPS = 8
G_DIM = 128
G_WIDTH = G_GROUPS * G_DIM
GMLP_CHUNK = 128
AB_IN = 3 * A_WIDTH + 2 * G_WIDTH
DN_QK_HEADS = 16
DN_V_HEADS = 32
DN_HEAD_DIM = 128
DN_QK_WIDTH = DN_QK_HEADS * DN_HEAD_DIM
DN_V_WIDTH = DN_V_HEADS * DN_HEAD_DIM
DN_CONV_DIM = 2 * DN_QK_WIDTH + DN_V_WIDTH
DN_MAIN = DN_CONV_DIM + DN_V_WIDTH
DN_CONV = 4
DN_CHUNK = 64
DEEPNORM_ALPHA = (2 * DEPTH) ** 0.25
LN_EPS = 1e-5
RMS_EPS = 1e-6
NEG_INF = -1e30

SUBLANES = 8
VMEM_LIMIT_BYTES = 60 * 1024 * 1024

NN = (((1,), (0,)), ((), ()))
NT = (((1,), (1,)), ((), ()))
TN = (((0,), (0,)), ((), ()))


def _dot(a, b, dims=NN):
    return lax.dot_general(a, b, dims, preferred_element_type=F32)


def _split2(a):
    hi = a.astype(BF16)
    lo = (a - hi.astype(F32)).astype(BF16)
    return hi, lo


def _dot_hp(a, b, dims=NN):
    a_hi, a_lo = _split2(a)
    b_hi, b_lo = _split2(b)
    return _dot(a_hi, b_hi, dims) + (_dot(a_hi, b_lo, dims) + _dot(a_lo, b_hi, dims))


def _dot_exact_lhs(a_bf16, b, dims=NN):
    b0 = b.astype(BF16)
    r1 = b - b0.astype(F32)
    b1 = r1.astype(BF16)
    b2 = (r1 - b1.astype(F32)).astype(BF16)
    return _dot(a_bf16, b0, dims) + (_dot(a_bf16, b1, dims) + _dot(a_bf16, b2, dims))


def _dot_exact_rhs(a, b_bf16, dims=NN):
    a0 = a.astype(BF16)
    r1 = a - a0.astype(F32)
    a1 = r1.astype(BF16)
    a2 = (r1 - a1.astype(F32)).astype(BF16)
    return _dot(a0, b_bf16, dims) + (_dot(a1, b_bf16, dims) + _dot(a2, b_bf16, dims))


def _layer_norm(y, g, b):
    mu = jnp.mean(y, axis=-1, keepdims=True)
    yc = y - mu
    var = jnp.mean(yc * yc, axis=-1, keepdims=True)
    return yc * lax.rsqrt(var + LN_EPS) * g + b


def _silu(x):
    return x * jax.nn.sigmoid(x)


def _params(*sem):
    return pltpu.CompilerParams(dimension_semantics=sem, vmem_limit_bytes=VMEM_LIMIT_BYTES)


FFN_ROWS = 512
FFN_ROWS_LAST = 256


def _ffn_kernel(x_ref, wg_ref, wu_ref, wd_ref, g_ref, b_ref, o_ref, xb_ref):
    j = pl.program_id(1)
    last_j = pl.num_programs(1) - 1

    def step(first, last):
        sub = FFN_ROWS_LAST if last else FFN_ROWS
        for r in range(0, o_ref.shape[0], sub):
            rows = slice(r, r + sub)
            if first:
                xb_ref[rows, :] = x_ref[rows, :].astype(BF16)
            xb = xb_ref[rows, :]
            gate = _dot(xb, wg_ref[...])
            up = _dot(xb, wu_ref[...])
            h = (_silu(gate) * up).astype(BF16)
            part = _dot(h, wd_ref[...])
            acc = part if first else o_ref[rows, :] + part
            if last:
                y = DEEPNORM_ALPHA * x_ref[rows, :] + 0.5 * acc
                acc = _layer_norm(y, g_ref[...], b_ref[...])
            o_ref[rows, :] = acc

    pl.when(j == 0)(functools.partial(step, True, False))
    pl.when((j > 0) & (j < last_j))(functools.partial(step, False, False))
    pl.when(j == last_j)(functools.partial(step, False, True))


def _ffn_ln(x, wg, wu, wd, ln_g, ln_b, tm=1024, tf=512):
    t, d = x.shape
    dff = wg.shape[1]
    assert dff // tf >= 2
    return pl.pallas_call(
        _ffn_kernel,
        grid=(t // tm, dff // tf),
        in_specs=[
            pl.BlockSpec((tm, d), lambda i, j: (i, 0)),
            pl.BlockSpec((d, tf), lambda i, j: (0, j)),
            pl.BlockSpec((d, tf), lambda i, j: (0, j)),
            pl.BlockSpec((tf, d), lambda i, j: (j, 0)),
            pl.BlockSpec((1, d), lambda i, j: (0, 0)),
            pl.BlockSpec((1, d), lambda i, j: (0, 0)),
        ],
        out_specs=pl.BlockSpec((tm, d), lambda i, j: (i, 0)),
        out_shape=jax.ShapeDtypeStruct((t, d), F32),
        scratch_shapes=[pltpu.VMEM((tm, d), BF16)],
        compiler_params=_params("parallel", "arbitrary"),
        name="ffn_ln",
    )(x, wg, wu, wd, ln_g.reshape(1, d), ln_b.reshape(1, d))


def _cast_job_specs(jobs, grid):
    n_steps = math.prod(grid)
    strides = [math.prod(grid[a + 1:]) for a in range(len(grid))]
    step = lambda *ids: sum(i * s for i, s in zip(ids, strides))
    in_specs, out_specs, out_shapes = [], [], []
    for w, layer in jobs:
        _, r, c = w.shape
        slab = r // n_steps
        assert slab * n_steps == r and slab % (2 * SUBLANES) == 0
        in_specs.append(pl.BlockSpec((None, slab, c),
                                     lambda *ids, layer=layer: (layer, step(*ids), 0)))
        out_specs.append(pl.BlockSpec((slab, c), lambda *ids: (step(*ids), 0)))
        out_shapes.append(jax.ShapeDtypeStruct((r, c), BF16))
    return in_specs, out_specs, out_shapes


def _run_cast_jobs(src_refs, dst_refs):
    for src, dst in zip(src_refs, dst_refs):
        dst[...] = src[...].astype(BF16)


AB_TN = 1280
DN_TN = 2048


def _matmul_kernel(x_ref, w_ref, o_ref, xb_ref, *, w_is_transposed):
    @pl.when(pl.program_id(1) == 0)
    def _():
        xb_ref[...] = x_ref[...].astype(BF16)

    o_ref[...] = _dot(xb_ref[...], w_ref[...], NT if w_is_transposed else NN)


def _matmul(x, w, n, w_is_transposed=False, tm=1024, tn=1024):
    t, k = x.shape
    if w_is_transposed:
        w_spec = pl.BlockSpec((None, tn, k), lambda i, j: (0, j, 0))
    else:
        w_spec = pl.BlockSpec((None, k, tn), lambda i, j: (0, 0, j))
    return pl.pallas_call(
        functools.partial(_matmul_kernel, w_is_transposed=w_is_transposed),
        grid=(t // tm, n // tn),
        in_specs=[pl.BlockSpec((tm, k), lambda i, j: (i, 0)), w_spec],
        out_specs=pl.BlockSpec((tm, tn), lambda i, j: (i, j)),
        out_shape=jax.ShapeDtypeStruct((t, n), F32),
        scratch_shapes=[pltpu.VMEM((tm, k), BF16)],
        compiler_params=_params("parallel", "arbitrary"),
        name="in_proj",
    )(x, w)


PROJ_ROWS = 256


def _proj_ln_kernel(*refs, n_in):
    a_refs = refs[:n_in]
    w_ref, x_ref, g_ref, b_ref, o_ref = refs[n_in:]
    for r in range(0, o_ref.shape[0], PROJ_ROWS):
        rows = slice(r, r + PROJ_ROWS)
        acc = None
        row = 0
        for a_ref in a_refs:
            kc = a_ref.shape[1]
            part = _dot(a_ref[rows, :], w_ref[row:row + kc, :])
            acc = part if acc is None else acc + part
            row += kc
        y = DEEPNORM_ALPHA * x_ref[rows, :] + acc
        o_ref[rows, :] = _layer_norm(y, g_ref[...], b_ref[...])


def _proj_ln(a_parts, w, x, ln_g, ln_b, tm=512):
    t, d = x.shape
    k = w.shape[1]
    in_specs = [pl.BlockSpec((tm, a.shape[1]), lambda i: (i, 0)) for a in a_parts]
    in_specs += [
        pl.BlockSpec((None, k, d), lambda i: (0, 0, 0), pipeline_mode=pl.Buffered(1)),
        pl.BlockSpec((tm, d), lambda i: (i, 0)),
        pl.BlockSpec((1, d), lambda i: (0, 0)),
        pl.BlockSpec((1, d), lambda i: (0, 0)),
    ]
    return pl.pallas_call(
        functools.partial(_proj_ln_kernel, n_in=len(a_parts)),
        grid=(t // tm,),
        in_specs=in_specs,
        out_specs=pl.BlockSpec((tm, d), lambda i: (i, 0)),
        out_shape=jax.ShapeDtypeStruct((t, d), F32),
        compiler_params=_params("parallel"),
        name="proj_ln",
    )(*a_parts, w, x, ln_g.reshape(1, d), ln_b.reshape(1, d))


def _rope(x, cos_full, sin_signed):
    return x * cos_full + pltpu.roll(x, HEAD_DIM // 2, 1) * sin_signed


def _moba_kernel(q_ref, k_ref, v_ref, cos_ref, sin_ref, *rest, n_cast):
    o_ref = rest[n_cast]
    _run_cast_jobs(rest[:n_cast], rest[n_cast + 1:])
    s = q_ref.shape[0]
    nb = s // MOBA_BLOCK
    blk = MOBA_BLOCK
    cos_full = cos_ref[...]
    sin_signed = sin_ref[...]
    qr = _rope(q_ref[...], cos_full, sin_signed)
    kr = _rope(k_ref[...], cos_full, sin_signed)

    row = lax.broadcasted_iota(jnp.int32, (HEAD_DIM, s), 0)
    col = lax.broadcasted_iota(jnp.int32, (HEAD_DIM, s), 1)
    avg = jnp.where(col // blk == row, 1.0 / blk, 0.0).astype(BF16)
    k_mean = _dot_exact_lhs(avg, kr)
    gate_t = _dot_hp(k_mean, qr, NT)

    qb = qr.astype(BF16)
    kb = kr.astype(BF16)
    vb = v_ref[...].astype(BF16)
    scale = HEAD_DIM ** -0.5

    assert nb <= SUBLANES and blk == 2 * HEAD_DIM
    blk_row = lax.broadcasted_iota(jnp.int32, (SUBLANES, blk), 0)
    qpos = lax.broadcasted_iota(jnp.int32, (blk, blk), 0)
    kpos = lax.broadcasted_iota(jnp.int32, (blk, blk), 1)
    causal = kpos <= qpos
    eye_q = jnp.where(kpos == qpos, 1.0, 0.0).astype(BF16)
    pad_rows = jnp.zeros((HEAD_DIM - SUBLANES, blk), F32)

    for i in range(nb):
        rows = slice(i * blk, (i + 1) * blk)
        qi = qb[rows]
        sel = None
        if i > MOBA_TOPK:
            g = gate_t[0:SUBLANES, rows]
            rank = jnp.zeros((SUBLANES, blk), F32)
            for m_blk in range(i):
                gm = g[m_blk:m_blk + 1, :]
                beats = (gm > g) | ((gm == g) & (m_blk < blk_row))
                rank = rank + jnp.where(beats, 1.0, 0.0)
            sel_t = jnp.where((rank < float(MOBA_TOPK)) & (blk_row < i), 1.0, 0.0)
            sel_t = jnp.concatenate([sel_t, pad_rows], axis=0).astype(BF16)
            sel = _dot(eye_q, sel_t, NT)
        scores = []
        for n in range(i + 1):
            sn = _dot(qi, kb[n * blk:(n + 1) * blk], NT) * scale
            if n == i:
                sn = jnp.where(causal, sn, NEG_INF)
            elif sel is not None:
                selected = jnp.broadcast_to(sel[:, n:n + 1], sn.shape) > 0.5
                sn = jnp.where(selected, sn, NEG_INF)
            scores.append(sn)
        m_acc = None
        for sn in scores:
            folded = jnp.maximum(sn[:, :HEAD_DIM], sn[:, HEAD_DIM:])
            m_acc = folded if m_acc is None else jnp.maximum(m_acc, folded)
        m = m_acc.max(axis=-1, keepdims=True)
        l_acc = jnp.zeros((blk, HEAD_DIM), F32)
        acc = jnp.zeros((blk, HEAD_DIM), F32)
        for n, sn in enumerate(scores):
            p = jnp.exp(sn - m)
            l_acc = l_acc + (p[:, :HEAD_DIM] + p[:, HEAD_DIM:])
            acc = acc + _dot(p.astype(BF16), vb[n * blk:(n + 1) * blk])
        l = l_acc.sum(axis=-1, keepdims=True)
        o_ref[rows, :] = (acc / l).astype(o_ref.dtype)


def _moba(hcat, cos_full, sin_signed, batch, cast_jobs=()):
    t = hcat.shape[0]
    s = t // batch
    grid = (batch, A_HEADS)
    spec = lambda off: pl.BlockSpec((s, HEAD_DIM), lambda b, h: (b, off + h))
    tab = pl.BlockSpec((s, HEAD_DIM), lambda b, h: (0, 0), pipeline_mode=pl.Buffered(1))
    cast_in, cast_out, cast_shapes = _cast_job_specs(cast_jobs, grid)
    outs = pl.pallas_call(
        functools.partial(_moba_kernel, n_cast=len(cast_jobs)),
        grid=grid,
        in_specs=[spec(0), spec(A_HEADS), spec(2 * A_HEADS), tab, tab] + cast_in,
        out_specs=[pl.BlockSpec((s, HEAD_DIM), lambda b, h: (b, h))] + cast_out,
        out_shape=[jax.ShapeDtypeStruct((t, A_WIDTH), BF16)] + cast_shapes,
        compiler_params=_params("parallel", "parallel"),
        name="moba",
    )(hcat, hcat, hcat, cos_full, sin_signed, *[job[0] for job in cast_jobs])
    return outs[0], outs[1:]


def _gmlp_kernel(u_ref, v_ref, lng_ref, lnb_ref, ws_ref, bs_ref, o_ref):
    rows = u_ref.shape[0]
    tpos = lax.broadcasted_iota(jnp.int32, (GMLP_CHUNK, GMLP_CHUNK), 0)
    spos = lax.broadcasted_iota(jnp.int32, (GMLP_CHUNK, GMLP_CHUNK), 1)
    causal = spos <= tpos
    for g in range(G_GROUPS):
        cols = slice(g * G_DIM, (g + 1) * G_DIM)
        w = jnp.where(causal, ws_ref[g], 0.0).astype(BF16)
        bias = bs_ref[:, g:g + 1]
        ln_g = lng_ref[:, cols]
        ln_b = lnb_ref[:, cols]
        for c in range(rows // GMLP_CHUNK):
            r = slice(c * GMLP_CHUNK, (c + 1) * GMLP_CHUNK)
            vn = _layer_norm(jax.nn.gelu(v_ref[r, cols]), ln_g, ln_b)
            mixed = _dot(w, vn.astype(BF16)) + bias
            o_ref[r, cols] = (jax.nn.gelu(u_ref[r, cols]) * mixed).astype(o_ref.dtype)


def _gmlp(hcat, ln_g, ln_b, w_s, b_s, rows=512):
    t = hcat.shape[0]
    u_blk = 3 * A_WIDTH // G_WIDTH
    return pl.pallas_call(
        _gmlp_kernel,
        grid=(t // rows,),
        in_specs=[
            pl.BlockSpec((rows, G_WIDTH), lambda i: (i, u_blk)),
            pl.BlockSpec((rows, G_WIDTH), lambda i: (i, u_blk + 1)),
            pl.BlockSpec((1, G_WIDTH), lambda i: (0, 0)),
            pl.BlockSpec((1, G_WIDTH), lambda i: (0, 0)),
            pl.BlockSpec((G_GROUPS, GMLP_CHUNK, GMLP_CHUNK), lambda i: (0, 0, 0)),
            pl.BlockSpec((GMLP_CHUNK, G_GROUPS), lambda i: (0, 0)),
        ],
        out_specs=pl.BlockSpec((rows, G_WIDTH), lambda i: (i, 0)),
        out_shape=jax.ShapeDtypeStruct((t, G_WIDTH), BF16),
        compiler_params=_params("parallel"),
        name="gmlp",
    )(hcat, hcat, ln_g.reshape(1, G_WIDTH), ln_b.reshape(1, G_WIDTH), w_s, b_s.T)


GATE_BLK = 128


def _softplus(x):
    return jnp.maximum(x, 0.0) + jnp.log1p(jnp.exp(-jnp.abs(x)))


def _dn_gate_kernel(x_ref, w_ref, wt_ref, ng_ref, dtb_ref, ngc_ref, dtbc_ref,
                    bg_ref, gc_ref, gct_ref):
    s = x_ref.shape[0]
    hv = DN_V_HEADS
    x = x_ref[...].astype(BF16)
    ba = _dot(x, w_ref[...])
    bat = _dot(wt_ref[...], x, NT)
    lane = lax.broadcasted_iota(jnp.int32, (s, 2 * hv), 1)
    g = ng_ref[...] * _softplus(ba + dtb_ref[...])
    bg = jnp.where(lane < hv, jax.nn.sigmoid(ba), g)
    bg_ref[...] = bg
    gt = ngc_ref[...] * _softplus(bat + dtbc_ref[...])

    r = lax.broadcasted_iota(jnp.int32, (GATE_BLK, GATE_BLK), 0)
    c = lax.broadcasted_iota(jnp.int32, (GATE_BLK, GATE_BLK), 1)
    same = (r // DN_CHUNK) == (c // DN_CHUNK)
    lower = jnp.where(same & (c <= r), 1.0, 0.0).astype(BF16)
    upper = jnp.where(same & (r <= c), 1.0, 0.0).astype(BF16)
    for i in range(s // GATE_BLK):
        sl = slice(i * GATE_BLK, (i + 1) * GATE_BLK)
        gc_ref[sl, :] = _dot_exact_lhs(lower, bg[sl, :])
        gct_ref[:, sl] = _dot_exact_rhs(gt[:, sl], upper)


def _dn_gates(x, w_ba_t, a_log, dt_bias, rows=512):
    t, d = x.shape
    s = rows
    hv = DN_V_HEADS
    neg_rate = -jnp.exp(a_log.astype(F32))
    zeros = jnp.zeros((hv,), F32)
    ng = jnp.concatenate([zeros, neg_rate]).reshape(1, 2 * hv)
    dtb = jnp.concatenate([zeros, dt_bias.astype(F32)]).reshape(1, 2 * hv)
    small = lambda shape: pl.BlockSpec(shape, lambda b: (0, 0))
    return pl.pallas_call(
        _dn_gate_kernel,
        grid=(t // rows,),
        in_specs=[
            pl.BlockSpec((s, d), lambda b: (b, 0)),
            small((d, 2 * hv)),
            small((2 * hv, d)),
            small((1, 2 * hv)),
            small((1, 2 * hv)),
            small((2 * hv, 1)),
            small((2 * hv, 1)),
        ],
        out_specs=[
            pl.BlockSpec((s, 2 * hv), lambda b: (b, 0)),
            pl.BlockSpec((s, 2 * hv), lambda b: (b, 0)),
            pl.BlockSpec((2 * hv, s), lambda b: (0, b)),
        ],
        out_shape=[
            jax.ShapeDtypeStruct((t, 2 * hv), F32),
            jax.ShapeDtypeStruct((t, 2 * hv), F32),
            jax.ShapeDtypeStruct((2 * hv, t), F32),
        ],
        compiler_params=_params("parallel"),
        name="dn_gates",
    )(x, w_ba_t.T, w_ba_t, ng, dtb, ng.reshape(2 * hv, 1), dtb.reshape(2 * hv, 1))


DN_TB = 128
DN_HB = 16
CONV_PAD = 8


def _conv_stage(x_ref, buf_ref, first):
    tb = x_ref.shape[0]

    @pl.when(first)
    def _():
        buf_ref[0:CONV_PAD, :] = jnp.zeros((CONV_PAD, buf_ref.shape[1]), F32)

    buf_ref[CONV_PAD:CONV_PAD + tb, :] = x_ref[...]


def _conv_silu(buf_ref, w_ref, cols, tb):
    y = buf_ref[CONV_PAD:CONV_PAD + tb, cols] * w_ref[DN_CONV - 1:DN_CONV, cols]
    for j in range(1, DN_CONV):
        tap = DN_CONV - 1 - j
        y = y + buf_ref[CONV_PAD - j:CONV_PAD - j + tb, cols] * w_ref[tap:tap + 1, cols]
    return _silu(y)


def _conv_keep_tail(buf_ref, tb):
    buf_ref[0:CONV_PAD, :] = buf_ref[tb:tb + CONV_PAD, :]


def _hi_lo(a):
    hi = a.astype(BF16).astype(F32)
    return hi, a - hi


def _pair_lhs(x):
    hi, lo = _hi_lo(x)
    return jnp.concatenate([hi, lo], axis=1).astype(BF16)


def _pair_blockdiag(y, first_half):
    top = jnp.where(first_half, y, 0.0)
    return jnp.concatenate([top, y - top], axis=0)


def _pair_product(x, y, first_half):
    return _dot(x.astype(BF16), _pair_blockdiag(y, first_half).astype(BF16))


def _pair_product_hp(x, y, first_half):
    width = x.shape[1]
    y_hi, y_lo = _hi_lo(y)
    bd = jnp.concatenate([_pair_blockdiag(y_hi, first_half),
                          _pair_blockdiag(y_lo, first_half)], axis=1).astype(BF16)
    out = _dot(_pair_lhs(x), jnp.concatenate([bd, bd], axis=0))
    return out[:, :width] + out[:, width:]


def _delta_kernel(*refs):
    for group in range(DN_V_HEADS // DN_HB):
        pl.when(pl.program_id(1) == group)(functools.partial(_delta_group, group, *refs))


def _delta_group(group, q_ref, k_ref, v_ref, z_ref, wq_ref, wk_ref, wv_ref, bg_ref, gc_ref,
                 gct_ref, ng_ref, o_ref, state_ref, qbuf_ref, kbuf_ref, vbuf_ref):
    tb = q_ref.shape[0]
    hv = DN_V_HEADS
    c = DN_CHUNK
    dk = DN_HEAD_DIM
    rep = DN_V_HEADS // DN_QK_HEADS
    first = pl.program_id(2) == 0

    @pl.when(first)
    def _():
        state_ref[...] = jnp.zeros_like(state_ref)

    _conv_stage(q_ref, qbuf_ref, first)
    _conv_stage(k_ref, kbuf_ref, first)
    _conv_stage(v_ref, vbuf_ref, first)

    assert tb == 2 * c
    ri = lax.broadcasted_iota(jnp.int32, (c, 2 * c), 0)
    ci2 = lax.broadcasted_iota(jnp.int32, (c, 2 * c), 1)
    first_half = ci2 < c
    ci = jnp.where(first_half, ci2, ci2 - c)
    incl = ci <= ri
    strict = ci < ri
    eye = jnp.where(ci == ri, 1.0, 0.0)
    norm_g = ng_ref[...]
    n_qk = DN_HB // rep
    heads = range(DN_HB)


    q_n, k_n = [], []
    for jq in range(n_qk):
        qcols = slice(jq * dk, (jq + 1) * dk)
        q_h = _conv_silu(qbuf_ref, wq_ref, qcols, tb)
        k_h = _conv_silu(kbuf_ref, wk_ref, qcols, tb)
        q_n.append(q_h * lax.rsqrt(jnp.sum(q_h * q_h, axis=-1, keepdims=True) + RMS_EPS)
                   * (dk ** -0.5))
        k_n.append(k_h * lax.rsqrt(jnp.sum(k_h * k_h, axis=-1, keepdims=True) + RMS_EPS))

    qk_pair, kk_pair = [], []
    for jq in range(n_qk):
        q0, q1 = q_n[jq][:c], q_n[jq][c:]
        k0, k1 = k_n[jq][:c], k_n[jq][c:]
        zero = jnp.zeros_like(k0)
        lhs = jnp.concatenate([jnp.concatenate([q0, q1], axis=1),
                               jnp.concatenate([k0, k1], axis=1)], axis=0).astype(BF16)
        rhs = jnp.concatenate([jnp.concatenate([k0, zero], axis=1),
                               jnp.concatenate([zero, k1], axis=1)], axis=0).astype(BF16)
        g = _dot(lhs, rhs, NT)
        qk_pair.append(g[:c])
        kk_pair.append(g[c:])

    a_pair, attn_bf, kbg_bf, q_dec, k_dec, g_last, betas = [], [], [], [], [], [], []
    for hl in heads:
        jq = hl // rep
        head = group * DN_HB + hl
        beta_col = bg_ref[:, head:head + 1]
        betas.append(beta_col)
        gc_col = gc_ref[:, hv + head:hv + head + 1]
        gc_row = gct_ref[head:head + 1, :]
        beta_sel = jnp.where(first_half, beta_col[:c], beta_col[c:])
        gc_sel = jnp.where(first_half, gc_col[:c], gc_col[c:])
        decay = jnp.exp(jnp.where(incl, gc_sel - gc_row, -jnp.inf))
        a_pair.append(jnp.where(strict, kk_pair[jq] * beta_sel * decay, 0.0))
        attn_bf.append((qk_pair[jq] * decay).astype(BF16))
        egc = jnp.exp(gc_col)
        kbg_bf.append((k_n[jq] * beta_col * egc).astype(BF16))
        q_dec.append((q_n[jq] * egc).astype(BF16))
        gl = [gc_col[c - 1:c, :], gc_col[tb - 1:tb, :]]
        gl_rows = jnp.concatenate([jnp.broadcast_to(gl[0], (c, 1)),
                                   jnp.broadcast_to(gl[1], (c, 1))], axis=0)
        k_dec.append((k_n[jq] * jnp.exp(gl_rows - gc_col)).astype(BF16))
        g_last.append([jnp.exp(gl[0]), jnp.exp(gl[1])])

    vb_bf, z_gate = [], []

    def value_conv(hl):
        vcols = slice(hl * dk, (hl + 1) * dk)
        vb_bf.append((_conv_silu(vbuf_ref, wv_ref, vcols, tb) * betas[hl]).astype(BF16))

    def output_gate(hl):
        z_gate.append(norm_g * _silu(z_ref[:, hl * dk:(hl + 1) * dk]))

    independent = ([functools.partial(value_conv, hl) for hl in heads]
                   + [functools.partial(output_gate, hl) for hl in heads])
    n_slots = 2 * (int(math.log2(c)) - 1) + 2
    per_slot = -(-len(independent) // n_slots)

    def fill():
        for work in independent[:per_slot]:
            work()
        del independent[:per_slot]

    inv = [eye - jnp.where((ri // 2) == (ci // 2), a, 0.0) for a in a_pair]
    s = 2
    while s < c:
        off = ((ri // (2 * s)) == (ci // (2 * s))) & ((ri // s) != (ci // s))
        m1 = [_pair_product(jnp.where(off, a_pair[hl], 0.0), inv[hl], first_half)
              for hl in heads]
        fill()
        m2 = [_pair_product(inv[hl], m1[hl], first_half) for hl in heads]
        fill()
        inv = [inv[hl] - m2[hl] for hl in heads]
        s *= 2
    ax = [_pair_product_hp(a_pair[hl], inv[hl], first_half) for hl in heads]
    fill()
    resid = [eye - inv[hl] - ax[hl] for hl in heads]
    corr = [_pair_product(inv[hl], resid[hl], first_half) for hl in heads]
    fill()
    inv = [inv[hl] + corr[hl] for hl in heads]
    assert not independent

    u, w_bf = [], []
    zero_rows = jnp.zeros((c, 2 * dk), BF16)
    for hl in heads:
        lhs = _pair_lhs(inv[hl])
        rhs = jnp.concatenate([vb_bf[hl], kbg_bf[hl]], axis=1)
        r0, r1 = rhs[:c], rhs[c:]
        uw0 = _dot(lhs, jnp.concatenate([r0, zero_rows, r0, zero_rows], axis=0))
        uw1 = _dot(lhs, jnp.concatenate([zero_rows, r1, zero_rows, r1], axis=0))
        u.append([uw0[:, :dk], uw1[:, :dk]])
        w_bf.append([uw0[:, dk:].astype(BF16), uw1[:, dk:].astype(BF16)])

    states = [state_ref[hl] for hl in heads]
    zero_v = jnp.zeros((c, dk), BF16)
    for ch in range(2):
        rows = slice(ch * c, (ch + 1) * c)
        ws_qs = [_dot(jnp.concatenate([w_bf[hl][ch], q_dec[hl][rows]], axis=0),
                      states[hl].astype(BF16)) for hl in heads]
        v_new_bf = [(u[hl][ch] - ws_qs[hl][:c]).astype(BF16) for hl in heads]
        o_intra = []
        for hl in heads:
            v_rows = [v_new_bf[hl], zero_v] if ch == 0 else [zero_v, v_new_bf[hl]]
            o_intra.append(_dot(attn_bf[hl], jnp.concatenate(v_rows, axis=0)))
        states = [states[hl] * g_last[hl][ch] + _dot(k_dec[hl][rows], v_new_bf[hl], TN)
                  for hl in heads]
        for hl in heads:
            vcols = slice(hl * dk, (hl + 1) * dk)
            o_c = ws_qs[hl][c:] + o_intra[hl]
            o_n = (o_c * lax.rsqrt(jnp.mean(o_c * o_c, axis=-1, keepdims=True) + RMS_EPS)
                   * z_gate[hl][rows])
            o_ref[rows, vcols] = o_n.astype(o_ref.dtype)
    for hl in heads:
        state_ref[hl] = states[hl]
    _conv_keep_tail(qbuf_ref, tb)
    _conv_keep_tail(kbuf_ref, tb)
    _conv_keep_tail(vbuf_ref, tb)


def _delta(hcat, conv_w, bg, gcs, gct, norm_g, batch):
    t = hcat.shape[0]
    s = t // batch
    nt = s // DN_TB
    rep = DN_V_HEADS // DN_QK_HEADS
    wq = DN_HB // rep * DN_HEAD_DIM
    wv = DN_HB * DN_HEAD_DIM
    k_off = DN_QK_WIDTH // wq
    v_off = 2 * DN_QK_WIDTH // wv
    z_off = DN_CONV_DIM // wv
    hv2 = 2 * DN_V_HEADS
    row = lambda b, g, i: b * nt + i
    return pl.pallas_call(
        _delta_kernel,
        grid=(batch, DN_V_HEADS // DN_HB, nt),
        in_specs=[
            pl.BlockSpec((DN_TB, wq), lambda b, g, i: (row(b, g, i), g)),
            pl.BlockSpec((DN_TB, wq), lambda b, g, i: (row(b, g, i), k_off + g)),
            pl.BlockSpec((DN_TB, wv), lambda b, g, i: (row(b, g, i), v_off + g)),
            pl.BlockSpec((DN_TB, wv), lambda b, g, i: (row(b, g, i), z_off + g)),
            pl.BlockSpec((DN_CONV, wq), lambda b, g, i: (0, g)),
            pl.BlockSpec((DN_CONV, wq), lambda b, g, i: (0, k_off + g)),
            pl.BlockSpec((DN_CONV, wv), lambda b, g, i: (0, v_off + g)),
            pl.BlockSpec((DN_TB, hv2), lambda b, g, i: (row(b, g, i), 0)),
            pl.BlockSpec((DN_TB, hv2), lambda b, g, i: (row(b, g, i), 0)),
            pl.BlockSpec((DN_V_HEADS, DN_TB), lambda b, g, i: (1, row(b, g, i))),
            pl.BlockSpec((1, DN_HEAD_DIM), lambda b, g, i: (0, 0)),
        ],
        out_specs=pl.BlockSpec((DN_TB, wv), lambda b, g, i: (row(b, g, i), g)),
        out_shape=jax.ShapeDtypeStruct((t, DN_V_WIDTH), BF16),
        scratch_shapes=[
            pltpu.VMEM((DN_HB, DN_HEAD_DIM, DN_HEAD_DIM), F32),
            pltpu.VMEM((CONV_PAD + DN_TB, wq), F32),
            pltpu.VMEM((CONV_PAD + DN_TB, wq), F32),
            pltpu.VMEM((CONV_PAD + DN_TB, wv), F32),
        ],
        compiler_params=_params("parallel", "parallel", "arbitrary"),
        name="delta_rule",
    )(hcat, hcat, hcat, hcat, conv_w, conv_w, conv_w, bg, gcs, gct,
      norm_g.reshape(1, DN_HEAD_DIM))


def _rope_tables(s):
    half = HEAD_DIM // 2
    inv = jnp.exp(-math.log(ROPE_THETA) * jnp.arange(half, dtype=F32) * (2.0 / HEAD_DIM))
    ang = jnp.arange(s, dtype=F32)[:, None] * inv[None, :]
    cos, sin = jnp.cos(ang), jnp.sin(ang)
    return jnp.concatenate([cos, cos], axis=-1), jnp.concatenate([-sin, sin], axis=-1)


def kernel(x, ffn1_w_gate, ffn1_w_up, ffn1_w_down, ffn2_w_gate, ffn2_w_up, ffn2_w_down, ln_g, ln_b, ab_w_in, ab_gmlp_ln_g, ab_gmlp_ln_b, ab_gmlp_w_s, ab_gmlp_b_s, ab_w_out, dn_w_in, dn_conv_w, dn_a_log, dn_dt_bias, dn_norm_g, dn_w_out):
    batch, s, d = x.shape
    t = batch * s
    bf = lambda w: w.astype(BF16)
    xf = x.reshape(t, d)
    cos_full, sin_signed = _rope_tables(s)
    ab_in, ab_out, dn_out = bf(ab_w_in), bf(ab_w_out), bf(dn_w_out)
    dn_in_t = bf(jnp.swapaxes(dn_w_in, 1, 2))

    ffn_f32 = {1: (ffn1_w_gate, ffn1_w_up, ffn1_w_down), 2: (ffn2_w_gate, ffn2_w_up, ffn2_w_down)}
    ffn_bf16 = {}
    pending = [(half, layer) for layer in range(DEPTH) for half in (1, 2)]

    def cast_now(key):
        half, layer = key
        ffn_bf16[key] = tuple(bf(w[layer]) for w in ffn_f32[half])

    def take_cast_jobs():
        keys = list(pending)
        del pending[:]
        return keys, tuple((w, layer) for half, layer in keys for w in ffn_f32[half])

    def finish_cast_jobs(keys, results):
        for n, key in enumerate(keys):
            ffn_bf16[key] = tuple(results[3 * n:3 * n + 3])

    def half_step(xf, half, layer, slot):
        key = (half, layer)
        if key not in ffn_bf16:
            pending.remove(key)
            cast_now(key)
        return _ffn_ln(xf, *ffn_bf16[key], ln_g[layer, slot], ln_b[layer, slot])

    cast_now(pending.pop(0))
    for i in range(DEPTH):
        xf = half_step(xf, 1, i, 0)
        j = i // 2
        if i % 2 == 0:
            hcat = _matmul(xf, ab_in[j:j + 1], AB_IN, tn=AB_TN)
            keys, jobs = take_cast_jobs()
            a_out, done = _moba(hcat, cos_full, sin_signed, batch, cast_jobs=jobs)
            finish_cast_jobs(keys, done)
            g_out = _gmlp(hcat, ab_gmlp_ln_g[j], ab_gmlp_ln_b[j], ab_gmlp_w_s[j], ab_gmlp_b_s[j])
            mix, w_out = [a_out, g_out], ab_out[j:j + 1]
        else:
            hcat = _matmul(xf, dn_in_t[j:j + 1], DN_MAIN, w_is_transposed=True, tn=DN_TN)
            bg, gcs, gct = _dn_gates(xf, dn_in_t[j, DN_MAIN:], dn_a_log[j], dn_dt_bias[j])
            mix = [_delta(hcat, dn_conv_w[j], bg, gcs, gct, dn_norm_g[j], batch)]
            w_out = dn_out[j:j + 1]
        xf = _proj_ln(mix, w_out, xf, ln_g[i, 1], ln_b[i, 1])
        xf = half_step(xf, 2, i, 2)
    return xf.reshape(batch, s, d)
```

```python
import functools
import math

import jax
import jax.numpy as jnp
from jax import lax
from jax.experimental import pallas as pl
from jax.experimental.pallas import tpu as pltpu

F32 = jnp.float32
BF16 = jnp.bfloat16

D_MODEL = 2048
SEQ = 2048
DEPTH = 2
HEAD_DIM = 128
A_HEADS = 8
A_WIDTH = A_HEADS * HEAD_DIM
MOBA_BLOCK = 256
MOBA_TOPK = 3
ROPE_THETA = 10000.0
G_GROUPS = 8
G_DIM = 128
G_WIDTH = G_GROUPS * G_DIM
GMLP_CHUNK = 128
AB_IN = 3 * A_WIDTH + 2 * G_WIDTH
DN_QK_HEADS = 16
DN_V_HEADS = 32
DN_HEAD_DIM = 128
DN_QK_WIDTH = DN_QK_HEADS * DN_HEAD_DIM
DN_V_WIDTH = DN_V_HEADS * DN_HEAD_DIM
DN_CONV_DIM = 2 * DN_QK_WIDTH + DN_V_WIDTH
DN_MAIN = DN_CONV_DIM + DN_V_WIDTH
DN_CONV = 4
DN_CHUNK = 64
DEEPNORM_ALPHA = (2 * DEPTH) ** 0.25
LN_EPS = 1e-5
RMS_EPS = 1e-6
NEG_INF = -1e30

SUBLANES = 8
VMEM_LIMIT_BYTES = 60 * 1024 * 1024

NN = (((1,), (0,)), ((), ()))
NT = (((1,), (1,)), ((), ()))
TN = (((0,), (0,)), ((), ()))


def _dot(a, b, dims=NN):
    return lax.dot_general(a, b, dims, preferred_element_type=F32)


def _split2(a):
    hi = a.astype(BF16)
    lo = (a - hi.astype(F32)).astype(BF16)
    return hi, lo


def _dot_hp(a, b, dims=NN):
    a_hi, a_lo = _split2(a)
    b_hi, b_lo = _split2(b)
    return _dot(a_hi, b_hi, dims) + (_dot(a_hi, b_lo, dims) + _dot(a_lo, b_hi, dims))


def _dot_exact_lhs(a_bf16, b, dims=NN):
    b0 = b.astype(BF16)
    r1 = b - b0.astype(F32)
    b1 = r1.astype(BF16)
    b2 = (r1 - b1.astype(F32)).astype(BF16)
    return _dot(a_bf16, b0, dims) + (_dot(a_bf16, b1, dims) + _dot(a_bf16, b2, dims))


def _dot_exact_rhs(a, b_bf16, dims=NN):
    a0 = a.astype(BF16)
    r1 = a - a0.astype(F32)
    a1 = r1.astype(BF16)
    a2 = (r1 - a1.astype(F32)).astype(BF16)
    return _dot(a0, b_bf16, dims) + (_dot(a1, b_bf16, dims) + _dot(a2, b_bf16, dims))


def _layer_norm(y, g, b):
    mu = jnp.mean(y, axis=-1, keepdims=True)
    yc = y - mu
    var = jnp.mean(yc * yc, axis=-1, keepdims=True)
    return yc * lax.rsqrt(var + LN_EPS) * g + b


def _silu(x):
    return x * jax.nn.sigmoid(x)


def _params(*sem):
    return pltpu.CompilerParams(dimension_semantics=sem, vmem_limit_bytes=VMEM_LIMIT_BYTES)


FFN_ROWS = 512
FFN_ROWS_LAST = 256


def _ffn_kernel(x_ref, wg_ref, wu_ref, wd_ref, g_ref, b_ref, o_ref, xb_ref):
    j = pl.program_id(1)
    last_j = pl.num_programs(1) - 1

    def step(first, last):
        sub = FFN_ROWS_LAST if last else FFN_ROWS
        for r in range(0, o_ref.shape[0], sub):
            rows = slice(r, r + sub)
            if first:
                xb_ref[rows, :] = x_ref[rows, :].astype(BF16)
            xb = xb_ref[rows, :]
            gate = _dot(xb, wg_ref[...])
            up = _dot(xb, wu_ref[...])
            h = (_silu(gate) * up).astype(BF16)
            part = _dot(h, wd_ref[...])
            acc = part if first else o_ref[rows, :] + part
            if last:
                y = DEEPNORM_ALPHA * x_ref[rows, :] + 0.5 * acc
                acc = _layer_norm(y, g_ref[...], b_ref[...])
            o_ref[rows, :] = acc

    pl.when(j == 0)(functools.partial(step, True, False))
    pl.when((j > 0) & (j < last_j))(functools.partial(step, False, False))
    pl.when(j == last_j)(functools.partial(step, False, True))


def _ffn_ln(x, wg, wu, wd, ln_g, ln_b, tm=1024, tf=512):
    t, d = x.shape
    dff = wg.shape[1]
    assert dff // tf >= 2
    return pl.pallas_call(
        _ffn_kernel,
        grid=(t // tm, dff // tf),
        in_specs=[
            pl.BlockSpec((tm, d), lambda i, j: (i, 0)),
            pl.BlockSpec((d, tf), lambda i, j: (0, j)),
            pl.BlockSpec((d, tf), lambda i, j: (0, j)),
            pl.BlockSpec((tf, d), lambda i, j: (j, 0)),
            pl.BlockSpec((1, d), lambda i, j: (0, 0)),
            pl.BlockSpec((1, d), lambda i, j: (0, 0)),
        ],
        out_specs=pl.BlockSpec((tm, d), lambda i, j: (i, 0)),
        out_shape=jax.ShapeDtypeStruct((t, d), F32),
        scratch_shapes=[pltpu.VMEM((tm, d), BF16)],
        compiler_params=_params("parallel", "arbitrary"),
        name="ffn_ln",
    )(x, wg, wu, wd, ln_g.reshape(1, d), ln_b.reshape(1, d))


def _cast_job_specs(jobs, grid):
    n_steps = math.prod(grid)
    strides = [math.prod(grid[a + 1:]) for a in range(len(grid))]
    step = lambda *ids: sum(i * s for i, s in zip(ids, strides))
    in_specs, out_specs, out_shapes = [], [], []
    for w, layer in jobs:
        _, r, c = w.shape
        slab = r // n_steps
        assert slab * n_steps == r and slab % (2 * SUBLANES) == 0
        in_specs.append(pl.BlockSpec((None, slab, c),
                                     lambda *ids, layer=layer: (layer, step(*ids), 0)))
        out_specs.append(pl.BlockSpec((slab, c), lambda *ids: (step(*ids), 0)))
        out_shapes.append(jax.ShapeDtypeStruct((r, c), BF16))
    return in_specs, out_specs, out_shapes


def _run_cast_jobs(src_refs, dst_refs):
    for src, dst in zip(src_refs, dst_refs):
        dst[...] = src[...].astype(BF16)


AB_TN = 1280
DN_TN = 2048


def _matmul_kernel(x_ref, w_ref, o_ref, xb_ref, *, w_is_transposed):
    @pl.when(pl.program_id(1) == 0)
    def _():
        xb_ref[...] = x_ref[...].astype(BF16)

    o_ref[...] = _dot(xb_ref[...], w_ref[...],
                      NT if w_is_transposed else NN).astype(o_ref.dtype)


def _matmul(x, w, n, w_is_transposed=False, tm=1024, tn=1024):
    t, k = x.shape
    if w_is_transposed:
        w_spec = pl.BlockSpec((None, tn, k), lambda i, j: (0, j, 0))
    else:
        w_spec = pl.BlockSpec((None, k, tn), lambda i, j: (0, 0, j))
    return pl.pallas_call(
        functools.partial(_matmul_kernel, w_is_transposed=w_is_transposed),
        grid=(t // tm, n // tn),
        in_specs=[pl.BlockSpec((tm, k), lambda i, j: (i, 0)), w_spec],
        out_specs=pl.BlockSpec((tm, tn), lambda i, j: (i, j)),
        out_shape=jax.ShapeDtypeStruct((t, n), BF16),
        scratch_shapes=[pltpu.VMEM((tm, k), BF16)],
        compiler_params=_params("parallel", "arbitrary"),
        name="in_proj",
    )(x, w)


PROJ_ROWS = 256


def _proj_ln_kernel(*refs, n_in):
    a_refs = refs[:n_in]
    w_ref, x_ref, g_ref, b_ref, o_ref = refs[n_in:]
    for r in range(0, o_ref.shape[0], PROJ_ROWS):
        rows = slice(r, r + PROJ_ROWS)
        acc = None
        row = 0
        for a_ref in a_refs:
            kc = a_ref.shape[1]
            part = _dot(a_ref[rows, :], w_ref[row:row + kc, :])
            acc = part if acc is None else acc + part
            row += kc
        y = DEEPNORM_ALPHA * x_ref[rows, :] + acc
        o_ref[rows, :] = _layer_norm(y, g_ref[...], b_ref[...])


def _proj_ln(a_parts, w, x, ln_g, ln_b, tm=512):
    t, d = x.shape
    k = w.shape[1]
    in_specs = [pl.BlockSpec((tm, a.shape[1]), lambda i: (i, 0)) for a in a_parts]
    in_specs += [
        pl.BlockSpec((None, k, d), lambda i: (0, 0, 0), pipeline_mode=pl.Buffered(1)),
        pl.BlockSpec((tm, d), lambda i: (i, 0)),
        pl.BlockSpec((1, d), lambda i: (0, 0)),
        pl.BlockSpec((1, d), lambda i: (0, 0)),
    ]
    return pl.pallas_call(
        functools.partial(_proj_ln_kernel, n_in=len(a_parts)),
        grid=(t // tm,),
        in_specs=in_specs,
        out_specs=pl.BlockSpec((tm, d), lambda i: (i, 0)),
        out_shape=jax.ShapeDtypeStruct((t, d), F32),
        compiler_params=_params("parallel"),
        name="proj_ln",
    )(*a_parts, w, x, ln_g.reshape(1, d), ln_b.reshape(1, d))


def _rope(x, cos_full, sin_signed):
    return x * cos_full + pltpu.roll(x, HEAD_DIM // 2, 1) * sin_signed


def _moba_kernel(q_ref, k_ref, v_ref, cos_ref, sin_ref, *rest, n_cast):
    o_ref = rest[n_cast]
    _run_cast_jobs(rest[:n_cast], rest[n_cast + 1:])
    s = q_ref.shape[0]
    nb = s // MOBA_BLOCK
    blk = MOBA_BLOCK
    cos_full = cos_ref[...]
    sin_signed = sin_ref[...]
    qr = _rope(q_ref[...].astype(F32), cos_full, sin_signed)
    kr = _rope(k_ref[...].astype(F32), cos_full, sin_signed)

    row = lax.broadcasted_iota(jnp.int32, (HEAD_DIM, s), 0)
    col = lax.broadcasted_iota(jnp.int32, (HEAD_DIM, s), 1)
    avg = jnp.where(col // blk == row, 1.0 / blk, 0.0).astype(BF16)
    k_mean = _dot_exact_lhs(avg, kr)
    gate_t = _dot_hp(k_mean, qr, NT)

    qb = qr.astype(BF16)
    kb = kr.astype(BF16)
    vb = v_ref[...]
    scale = HEAD_DIM ** -0.5

    assert nb <= SUBLANES and blk == 2 * HEAD_DIM
    blk_row = lax.broadcasted_iota(jnp.int32, (SUBLANES, blk), 0)
    qpos = lax.broadcasted_iota(jnp.int32, (blk, blk), 0)
    kpos = lax.broadcasted_iota(jnp.int32, (blk, blk), 1)
    causal = kpos <= qpos
    eye_q = jnp.where(kpos == qpos, 1.0, 0.0).astype(BF16)
    pad_rows = jnp.zeros((HEAD_DIM - SUBLANES, blk), F32)

    for i in range(nb):
        rows = slice(i * blk, (i + 1) * blk)
        qi = qb[rows]
        sel = None
        if i > MOBA_TOPK:
            g = gate_t[0:SUBLANES, rows]
            rank = jnp.zeros((SUBLANES, blk), F32)
            for m_blk in range(i):
                gm = g[m_blk:m_blk + 1, :]
                beats = (gm > g) | ((gm == g) & (m_blk < blk_row))
                rank = rank + jnp.where(beats, 1.0, 0.0)
            sel_t = jnp.where((rank < float(MOBA_TOPK)) & (blk_row < i), 1.0, 0.0)
            sel_t = jnp.concatenate([sel_t, pad_rows], axis=0).astype(BF16)
            sel = _dot(eye_q, sel_t, NT)
        scores = []
        for n in range(i + 1):
            sn = _dot(qi, kb[n * blk:(n + 1) * blk], NT) * scale
            if n == i:
                sn = jnp.where(causal, sn, NEG_INF)
            elif sel is not None:
                selected = jnp.broadcast_to(sel[:, n:n + 1], sn.shape) > 0.5
                sn = jnp.where(selected, sn, NEG_INF)
            scores.append(sn)
        m_acc = None
        for sn in scores:
            folded = jnp.maximum(sn[:, :HEAD_DIM], sn[:, HEAD_DIM:])
            m_acc = folded if m_acc is None else jnp.maximum(m_acc, folded)
        m = m_acc.max(axis=-1, keepdims=True)
        l_acc = jnp.zeros((blk, HEAD_DIM), F32)
        acc = jnp.zeros((blk, HEAD_DIM), F32)
        for n, sn in enumerate(scores):
            p = jnp.exp(sn - m)
            l_acc = l_acc + (p[:, :HEAD_DIM] + p[:, HEAD_DIM:])
            acc = acc + _dot(p.astype(BF16), vb[n * blk:(n + 1) * blk])
        l = l_acc.sum(axis=-1, keepdims=True)
        o_ref[rows, :] = (acc / l).astype(o_ref.dtype)


def _moba(hcat, cos_full, sin_signed, batch, cast_jobs=()):
    t = hcat.shape[0]
    s = t // batch
    grid = (batch, A_HEADS)
    spec = lambda off: pl.BlockSpec((s, HEAD_DIM), lambda b, h: (b, off + h))
    tab = pl.BlockSpec((s, HEAD_DIM), lambda b, h: (0, 0), pipeline_mode=pl.Buffered(1))
    cast_in, cast_out, cast_shapes = _cast_job_specs(cast_jobs, grid)
    outs = pl.pallas_call(
        functools.partial(_moba_kernel, n_cast=len(cast_jobs)),
        grid=grid,
        in_specs=[spec(0), spec(A_HEADS), spec(2 * A_HEADS), tab, tab] + cast_in,
        out_specs=[pl.BlockSpec((s, HEAD_DIM), lambda b, h: (b, h))] + cast_out,
        out_shape=[jax.ShapeDtypeStruct((t, A_WIDTH), BF16)] + cast_shapes,
        compiler_params=_params("parallel", "parallel"),
        name="moba",
    )(hcat, hcat, hcat, cos_full, sin_signed, *[job[0] for job in cast_jobs])
    return outs[0], outs[1:]


def _gmlp_kernel(u_ref, v_ref, lng_ref, lnb_ref, ws_ref, bs_ref, o_ref):
    rows = u_ref.shape[0]
    tpos = lax.broadcasted_iota(jnp.int32, (GMLP_CHUNK, GMLP_CHUNK), 0)
    spos = lax.broadcasted_iota(jnp.int32, (GMLP_CHUNK, GMLP_CHUNK), 1)
    causal = spos <= tpos
    for g in range(G_GROUPS):
        cols = slice(g * G_DIM, (g + 1) * G_DIM)
        w = jnp.where(causal, ws_ref[g], 0.0).astype(BF16)
        bias = bs_ref[:, g:g + 1]
        ln_g = lng_ref[:, cols]
        ln_b = lnb_ref[:, cols]
        for c in range(rows // GMLP_CHUNK):
            r = slice(c * GMLP_CHUNK, (c + 1) * GMLP_CHUNK)
            vn = _layer_norm(jax.nn.gelu(v_ref[r, cols].astype(F32)), ln_g, ln_b)
            mixed = _dot(w, vn.astype(BF16)) + bias
            gate = jax.nn.gelu(u_ref[r, cols].astype(F32))
            o_ref[r, cols] = (gate * mixed).astype(o_ref.dtype)


def _gmlp(hcat, ln_g, ln_b, w_s, b_s, rows=512):
    t = hcat.shape[0]
    u_blk = 3 * A_WIDTH // G_WIDTH
    return pl.pallas_call(
        _gmlp_kernel,
        grid=(t // rows,),
        in_specs=[
            pl.BlockSpec((rows, G_WIDTH), lambda i: (i, u_blk)),
            pl.BlockSpec((rows, G_WIDTH), lambda i: (i, u_blk + 1)),
            pl.BlockSpec((1, G_WIDTH), lambda i: (0, 0)),
            pl.BlockSpec((1, G_WIDTH), lambda i: (0, 0)),
            pl.BlockSpec((G_GROUPS, GMLP_CHUNK, GMLP_CHUNK), lambda i: (0, 0, 0)),
            pl.BlockSpec((GMLP_CHUNK, G_GROUPS), lambda i: (0, 0)),
        ],
        out_specs=pl.BlockSpec((rows, G_WIDTH), lambda i: (i, 0)),
        out_shape=jax.ShapeDtypeStruct((t, G_WIDTH), BF16),
        compiler_params=_params("parallel"),
        name="gmlp",
    )(hcat, hcat, ln_g.reshape(1, G_WIDTH), ln_b.reshape(1, G_WIDTH), w_s, b_s.T)


GATE_BLK = 128


def _softplus(x):
    return jnp.maximum(x, 0.0) + jnp.log1p(jnp.exp(-jnp.abs(x)))


def _dn_gate_kernel(x_ref, w_ref, wt_ref, ng_ref, dtb_ref, ngc_ref, dtbc_ref,
                    bg_ref, gc_ref, gct_ref):
    s = x_ref.shape[0]
    hv = DN_V_HEADS
    x = x_ref[...].astype(BF16)
    ba = _dot(x, w_ref[...])
    bat = _dot(wt_ref[...], x, NT)
    lane = lax.broadcasted_iota(jnp.int32, (s, 2 * hv), 1)
    g = ng_ref[...] * _softplus(ba + dtb_ref[...])
    bg = jnp.where(lane < hv, jax.nn.sigmoid(ba), g)
    bg_ref[...] = bg
    gt = ngc_ref[...] * _softplus(bat + dtbc_ref[...])

    r = lax.broadcasted_iota(jnp.int32, (GATE_BLK, GATE_BLK), 0)
    c = lax.broadcasted_iota(jnp.int32, (GATE_BLK, GATE_BLK), 1)
    same = (r // DN_CHUNK) == (c // DN_CHUNK)
    lower = jnp.where(same & (c <= r), 1.0, 0.0).astype(BF16)
    upper = jnp.where(same & (r <= c), 1.0, 0.0).astype(BF16)
    for i in range(s // GATE_BLK):
        sl = slice(i * GATE_BLK, (i + 1) * GATE_BLK)
        gc_ref[sl, :] = _dot_exact_lhs(lower, bg[sl, :])
        gct_ref[:, sl] = _dot_exact_rhs(gt[:, sl], upper)


def _dn_gates(x, w_ba_t, a_log, dt_bias, rows=512):
    t, d = x.shape
    s = rows
    hv = DN_V_HEADS
    neg_rate = -jnp.exp(a_log.astype(F32))
    zeros = jnp.zeros((hv,), F32)
    ng = jnp.concatenate([zeros, neg_rate]).reshape(1, 2 * hv)
    dtb = jnp.concatenate([zeros, dt_bias.astype(F32)]).reshape(1, 2 * hv)
    small = lambda shape: pl.BlockSpec(shape, lambda b: (0, 0))
    return pl.pallas_call(
        _dn_gate_kernel,
        grid=(t // rows,),
        in_specs=[
            pl.BlockSpec((s, d), lambda b: (b, 0)),
            small((d, 2 * hv)),
            small((2 * hv, d)),
            small((1, 2 * hv)),
            small((1, 2 * hv)),
            small((2 * hv, 1)),
            small((2 * hv, 1)),
        ],
        out_specs=[
            pl.BlockSpec((s, 2 * hv), lambda b: (b, 0)),
            pl.BlockSpec((s, 2 * hv), lambda b: (b, 0)),
            pl.BlockSpec((2 * hv, s), lambda b: (0, b)),
        ],
        out_shape=[
            jax.ShapeDtypeStruct((t, 2 * hv), F32),
            jax.ShapeDtypeStruct((t, 2 * hv), F32),
            jax.ShapeDtypeStruct((2 * hv, t), F32),
        ],
        compiler_params=_params("parallel"),
        name="dn_gates",
    )(x, w_ba_t.T, w_ba_t, ng, dtb, ng.reshape(2 * hv, 1), dtb.reshape(2 * hv, 1))


DN_TB = 128
DN_HB = 16
CONV_PAD = 8


def _conv_stage(x_ref, buf_ref, first):
    tb = x_ref.shape[0]

    @pl.when(first)
    def _():
        buf_ref[0:CONV_PAD, :] = jnp.zeros((CONV_PAD, buf_ref.shape[1]), F32)

    buf_ref[CONV_PAD:CONV_PAD + tb, :] = x_ref[...].astype(F32)


def _conv_silu(buf_ref, w_ref, cols, tb):
    y = buf_ref[CONV_PAD:CONV_PAD + tb, cols] * w_ref[DN_CONV - 1:DN_CONV, cols]
    for j in range(1, DN_CONV):
        tap = DN_CONV - 1 - j
        y = y + buf_ref[CONV_PAD - j:CONV_PAD - j + tb, cols] * w_ref[tap:tap + 1, cols]
    return _silu(y)


def _conv_keep_tail(buf_ref, tb):
    buf_ref[0:CONV_PAD, :] = buf_ref[tb:tb + CONV_PAD, :]


def _hi_lo(a):
    hi = a.astype(BF16).astype(F32)
    return hi, a - hi


def _pair_lhs(x):
    hi, lo = _hi_lo(x)
    return jnp.concatenate([hi, lo], axis=1).astype(BF16)


def _pair_blockdiag(y, first_half):
    top = jnp.where(first_half, y, 0.0)
    return jnp.concatenate([top, y - top], axis=0)


def _pair_product(x, y, first_half):
    return _dot(x.astype(BF16), _pair_blockdiag(y, first_half).astype(BF16))


def _pair_product_hp(x, y, first_half):
    width = x.shape[1]
    y_hi, y_lo = _hi_lo(y)
    bd = jnp.concatenate([_pair_blockdiag(y_hi, first_half),
                          _pair_blockdiag(y_lo, first_half)], axis=1).astype(BF16)
    out = _dot(_pair_lhs(x), jnp.concatenate([bd, bd], axis=0))
    return out[:, :width] + out[:, width:]


def _delta_kernel(*refs):
    for group in range(DN_V_HEADS // DN_HB):
        pl.when(pl.program_id(1) == group)(functools.partial(_delta_group, group, *refs))


def _delta_group(group, q_ref, k_ref, v_ref, z_ref, wq_ref, wk_ref, wv_ref, bg_ref, gc_ref,
                 gct_ref, ng_ref, o_ref, state_ref, qbuf_ref, kbuf_ref, vbuf_ref):
    tb = q_ref.shape[0]
    hv = DN_V_HEADS
    c = DN_CHUNK
    dk = DN_HEAD_DIM
    rep = DN_V_HEADS // DN_QK_HEADS
    first = pl.program_id(2) == 0

    @pl.when(first)
    def _():
        state_ref[...] = jnp.zeros_like(state_ref)

    _conv_stage(q_ref, qbuf_ref, first)
    _conv_stage(k_ref, kbuf_ref, first)
    _conv_stage(v_ref, vbuf_ref, first)

    assert tb == 2 * c
    ri = lax.broadcasted_iota(jnp.int32, (c, 2 * c), 0)
    ci2 = lax.broadcasted_iota(jnp.int32, (c, 2 * c), 1)
    first_half = ci2 < c
    ci = jnp.where(first_half, ci2, ci2 - c)
    incl = ci <= ri
    strict = ci < ri
    eye = jnp.where(ci == ri, 1.0, 0.0)
    norm_g = ng_ref[...]
    n_qk = DN_HB // rep
    heads = range(DN_HB)


    q_n, k_n = [], []
    for jq in range(n_qk):
        qcols = slice(jq * dk, (jq + 1) * dk)
        q_h = _conv_silu(qbuf_ref, wq_ref, qcols, tb)
        k_h = _conv_silu(kbuf_ref, wk_ref, qcols, tb)
        q_n.append(q_h * lax.rsqrt(jnp.sum(q_h * q_h, axis=-1, keepdims=True) + RMS_EPS)
                   * (dk ** -0.5))
        k_n.append(k_h * lax.rsqrt(jnp.sum(k_h * k_h, axis=-1, keepdims=True) + RMS_EPS))

    qk_pair, kk_pair = [], []
    for jq in range(n_qk):
        q0, q1 = q_n[jq][:c], q_n[jq][c:]
        k0, k1 = k_n[jq][:c], k_n[jq][c:]
        zero = jnp.zeros_like(k0)
        lhs = jnp.concatenate([jnp.concatenate([q0, q1], axis=1),
                               jnp.concatenate([k0, k1], axis=1)], axis=0).astype(BF16)
        rhs = jnp.concatenate([jnp.concatenate([k0, zero], axis=1),
                               jnp.concatenate([zero, k1], axis=1)], axis=0).astype(BF16)
        g = _dot(lhs, rhs, NT)
        qk_pair.append(g[:c])
        kk_pair.append(g[c:])

    a_pair, attn_bf, kbg_bf, q_dec, k_dec, g_last, betas = [], [], [], [], [], [], []
    for hl in heads:
        jq = hl // rep
        head = group * DN_HB + hl
        beta_col = bg_ref[:, head:head + 1]
        betas.append(beta_col)
        gc_col = gc_ref[:, hv + head:hv + head + 1]
        gc_row = gct_ref[head:head + 1, :]
        beta_sel = jnp.where(first_half, beta_col[:c], beta_col[c:])
        gc_sel = jnp.where(first_half, gc_col[:c], gc_col[c:])
        decay = jnp.exp(jnp.where(incl, gc_sel - gc_row, -jnp.inf))
        a_pair.append(jnp.where(strict, kk_pair[jq] * beta_sel * decay, 0.0))
        attn_bf.append((qk_pair[jq] * decay).astype(BF16))
        egc = jnp.exp(gc_col)
        kbg_bf.append((k_n[jq] * beta_col * egc).astype(BF16))
        q_dec.append((q_n[jq] * egc).astype(BF16))
        gl = [gc_col[c - 1:c, :], gc_col[tb - 1:tb, :]]
        gl_rows = jnp.concatenate([jnp.broadcast_to(gl[0], (c, 1)),
                                   jnp.broadcast_to(gl[1], (c, 1))], axis=0)
        k_dec.append((k_n[jq] * jnp.exp(gl_rows - gc_col)).astype(BF16))
        g_last.append([jnp.exp(gl[0]), jnp.exp(gl[1])])

    vb_bf, z_gate = [], []

    def value_conv(hl):
        vcols = slice(hl * dk, (hl + 1) * dk)
        vb_bf.append((_conv_silu(vbuf_ref, wv_ref, vcols, tb) * betas[hl]).astype(BF16))

    def output_gate(hl):
        z_gate.append(norm_g * _silu(z_ref[:, hl * dk:(hl + 1) * dk].astype(F32)))

    independent = ([functools.partial(value_conv, hl) for hl in heads]
                   + [functools.partial(output_gate, hl) for hl in heads])
    n_slots = 2 * (int(math.log2(c)) - 1) + 2
    per_slot = -(-len(independent) // n_slots)

    def fill():
        for work in independent[:per_slot]:
            work()
        del independent[:per_slot]

    inv = [eye - jnp.where((ri // 2) == (ci // 2), a, 0.0) for a in a_pair]
    s = 2
    while s < c:
        off = ((ri // (2 * s)) == (ci // (2 * s))) & ((ri // s) != (ci // s))
        m1 = [_pair_product(jnp.where(off, a_pair[hl], 0.0), inv[hl], first_half)
              for hl in heads]
        fill()
        m2 = [_pair_product(inv[hl], m1[hl], first_half) for hl in heads]
        fill()
        inv = [inv[hl] - m2[hl] for hl in heads]
        s *= 2
    ax = [_pair_product_hp(a_pair[hl], inv[hl], first_half) for hl in heads]
    fill()
    resid = [eye - inv[hl] - ax[hl] for hl in heads]
    corr = [_pair_product(inv[hl], resid[hl], first_half) for hl in heads]
    fill()
    inv = [inv[hl] + corr[hl] for hl in heads]
    assert not independent

    u, w_bf = [], []
    zero_rows = jnp.zeros((c, 2 * dk), BF16)
    for hl in heads:
        lhs = _pair_lhs(inv[hl])
        rhs = jnp.concatenate([vb_bf[hl], kbg_bf[hl]], axis=1)
        r0, r1 = rhs[:c], rhs[c:]
        uw0 = _dot(lhs, jnp.concatenate([r0, zero_rows, r0, zero_rows], axis=0))
        uw1 = _dot(lhs, jnp.concatenate([zero_rows, r1, zero_rows, r1], axis=0))
        u.append([uw0[:, :dk], uw1[:, :dk]])
        w_bf.append([uw0[:, dk:].astype(BF16), uw1[:, dk:].astype(BF16)])

    states = [state_ref[hl] for hl in heads]
    zero_v = jnp.zeros((c, dk), BF16)
    for ch in range(2):
        rows = slice(ch * c, (ch + 1) * c)
        ws_qs = [_dot(jnp.concatenate([w_bf[hl][ch], q_dec[hl][rows]], axis=0),
                      states[hl].astype(BF16)) for hl in heads]
        v_new_bf = [(u[hl][ch] - ws_qs[hl][:c]).astype(BF16) for hl in heads]
        o_intra = []
        for hl in heads:
            v_rows = [v_new_bf[hl], zero_v] if ch == 0 else [zero_v, v_new_bf[hl]]
            o_intra.append(_dot(attn_bf[hl], jnp.concatenate(v_rows, axis=0)))
        states = [states[hl] * g_last[hl][ch] + _dot(k_dec[hl][rows], v_new_bf[hl], TN)
                  for hl in heads]
        for hl in heads:
            vcols = slice(hl * dk, (hl + 1) * dk)
            o_c = ws_qs[hl][c:] + o_intra[hl]
            o_n = (o_c * lax.rsqrt(jnp.mean(o_c * o_c, axis=-1, keepdims=True) + RMS_EPS)
                   * z_gate[hl][rows])
            o_ref[rows, vcols] = o_n.astype(o_ref.dtype)
    for hl in heads:
        state_ref[hl] = states[hl]
    _conv_keep_tail(qbuf_ref, tb)
    _conv_keep_tail(kbuf_ref, tb)
    _conv_keep_tail(vbuf_ref, tb)


def _delta(hcat, conv_w, bg, gcs, gct, norm_g, batch):
    t = hcat.shape[0]
    s = t // batch
    nt = s // DN_TB
    rep = DN_V_HEADS // DN_QK_HEADS
    wq = DN_HB // rep * DN_HEAD_DIM
    wv = DN_HB * DN_HEAD_DIM
    k_off = DN_QK_WIDTH // wq
    v_off = 2 * DN_QK_WIDTH // wv
    z_off = DN_CONV_DIM // wv
    hv2 = 2 * DN_V_HEADS
    row = lambda b, g, i: b * nt + i
    return pl.pallas_call(
        _delta_kernel,
        grid=(batch, DN_V_HEADS // DN_HB, nt),
        in_specs=[
            pl.BlockSpec((DN_TB, wq), lambda b, g, i: (row(b, g, i), g)),
            pl.BlockSpec((DN_TB, wq), lambda b, g, i: (row(b, g, i), k_off + g)),
            pl.BlockSpec((DN_TB, wv), lambda b, g, i: (row(b, g, i), v_off + g)),
            pl.BlockSpec((DN_TB, wv), lambda b, g, i: (row(b, g, i), z_off + g)),
            pl.BlockSpec((DN_CONV, wq), lambda b, g, i: (0, g)),
            pl.BlockSpec((DN_CONV, wq), lambda b, g, i: (0, k_off + g)),
            pl.BlockSpec((DN_CONV, wv), lambda b, g, i: (0, v_off + g)),
            pl.BlockSpec((DN_TB, hv2), lambda b, g, i: (row(b, g, i), 0)),
            pl.BlockSpec((DN_TB, hv2), lambda b, g, i: (row(b, g, i), 0)),
            pl.BlockSpec((DN_V_HEADS, DN_TB), lambda b, g, i: (1, row(b, g, i))),
            pl.BlockSpec((1, DN_HEAD_DIM), lambda b, g, i: (0, 0)),
        ],
        out_specs=pl.BlockSpec((DN_TB, wv), lambda b, g, i: (row(b, g, i), g)),
        out_shape=jax.ShapeDtypeStruct((t, DN_V_WIDTH), BF16),
        scratch_shapes=[
            pltpu.VMEM((DN_HB, DN_HEAD_DIM, DN_HEAD_DIM), F32),
            pltpu.VMEM((CONV_PAD + DN_TB, wq), F32),
            pltpu.VMEM((CONV_PAD + DN_TB, wq), F32),
            pltpu.VMEM((CONV_PAD + DN_TB, wv), F32),
        ],
        compiler_params=_params("parallel", "parallel", "arbitrary"),
        name="delta_rule",
    )(hcat, hcat, hcat, hcat, conv_w, conv_w, conv_w, bg, gcs, gct,
      norm_g.reshape(1, DN_HEAD_DIM))


def _rope_tables(s):
    half = HEAD_DIM // 2
    inv = jnp.exp(-math.log(ROPE_THETA) * jnp.arange(half, dtype=F32) * (2.0 / HEAD_DIM))
    ang = jnp.arange(s, dtype=F32)[:, None] * inv[None, :]
    cos, sin = jnp.cos(ang), jnp.sin(ang)
    return jnp.concatenate([cos, cos], axis=-1), jnp.concatenate([-sin, sin], axis=-1)


def kernel(x, ffn1_w_gate, ffn1_w_up, ffn1_w_down, ffn2_w_gate, ffn2_w_up, ffn2_w_down, ln_g, ln_b, ab_w_in, ab_gmlp_ln_g, ab_gmlp_ln_b, ab_gmlp_w_s, ab_gmlp_b_s, ab_w_out, dn_w_in, dn_conv_w, dn_a_log, dn_dt_bias, dn_norm_g, dn_w_out):
    batch, s, d = x.shape
    t = batch * s
    bf = lambda w: w.astype(BF16)
    xf = x.reshape(t, d)
    cos_full, sin_signed = _rope_tables(s)
    ab_in, ab_out, dn_out = bf(ab_w_in), bf(ab_w_out), bf(dn_w_out)
    dn_in_t = bf(jnp.swapaxes(dn_w_in, 1, 2))

    ffn_f32 = {1: (ffn1_w_gate, ffn1_w_up, ffn1_w_down), 2: (ffn2_w_gate, ffn2_w_up, ffn2_w_down)}
    ffn_bf16 = {}
    pending = [(half, layer) for layer in range(DEPTH) for half in (1, 2)]

    def cast_now(key):
        half, layer = key
        ffn_bf16[key] = tuple(bf(w[layer]) for w in ffn_f32[half])

    def take_cast_jobs():
        keys = list(pending)
        del pending[:]
        return keys, tuple((w, layer) for half, layer in keys for w in ffn_f32[half])

    def finish_cast_jobs(keys, results):
        for n, key in enumerate(keys):
            ffn_bf16[key] = tuple(results[3 * n:3 * n + 3])

    def half_step(xf, half, layer, slot):
        key = (half, layer)
        if key not in ffn_bf16:
            pending.remove(key)
            cast_now(key)
        return _ffn_ln(xf, *ffn_bf16[key], ln_g[layer, slot], ln_b[layer, slot])

    cast_now(pending.pop(0))
    for i in range(DEPTH):
        xf = half_step(xf, 1, i, 0)
        j = i // 2
        if i % 2 == 0:
            hcat = _matmul(xf, ab_in[j:j + 1], AB_IN, tn=AB_TN)
            keys, jobs = take_cast_jobs()
            a_out, done = _moba(hcat, cos_full, sin_signed, batch, cast_jobs=jobs)
            finish_cast_jobs(keys, done)
            g_out = _gmlp(hcat, ab_gmlp_ln_g[j], ab_gmlp_ln_b[j], ab_gmlp_w_s[j], ab_gmlp_b_s[j])
            mix, w_out = [a_out, g_out], ab_out[j:j + 1]
        else:
            hcat = _matmul(xf, dn_in_t[j:j + 1], DN_MAIN, w_is_transposed=True, tn=DN_TN)
            bg, gcs, gct = _dn_gates(xf, dn_in_t[j, DN_MAIN:], dn_a_log[j], dn_dt_bias[j])
            mix = [_delta(hcat, dn_conv_w[j], bg, gcs, gct, dn_norm_g[j], batch)]
            w_out = dn_out[j:j + 1]
        xf = _proj_ln(mix, w_out, xf, ln_g[i, 1], ln_b[i, 1])
        xf = half_step(xf, 2, i, 2)
    return xf.reshape(batch, s, d)
```

```python
import functools
import math

import jax
import jax.numpy as jnp
from jax import lax
from jax.experimental import pallas as pl
from jax.experimental.pallas import tpu as pltpu

F32 = jnp.float32
BF16 = jnp.bfloat16

D_MODEL = 2048
SEQ = 2048
DEPTH = 2
HEAD_DIM = 128
A_HEADS = 8
A_WIDTH = A_HEADS * HEAD_DIM
MOBA_BLOCK = 256
MOBA_TOPK = 3
ROPE_THETA = 10000.0
G_GROUPS = 8
G_DIM = 128
G_WIDTH = G_GROUPS * G_DIM
GMLP_CHUNK = 128
AB_IN = 3 * A_WIDTH + 2 * G_WIDTH
DN_QK_HEADS = 16
DN_V_HEADS = 32
DN_HEAD_DIM = 128
DN_QK_WIDTH = DN_QK_HEADS * DN_HEAD_DIM
DN_V_WIDTH = DN_V_HEADS * DN_HEAD_DIM
DN_CONV_DIM = 2 * DN_QK_WIDTH + DN_V_WIDTH
DN_MAIN = DN_CONV_DIM + DN_V_WIDTH
DN_CONV = 4
DN_CHUNK = 64
DEEPNORM_ALPHA = (2 * DEPTH) ** 0.25
LN_EPS = 1e-5
RMS_EPS = 1e-6
NEG_INF = -1e30

SUBLANES = 8
VMEM_LIMIT_BYTES = 60 * 1024 * 1024

NN = (((1,), (0,)), ((), ()))
NT = (((1,), (1,)), ((), ()))
TN = (((0,), (0,)), ((), ()))


def _dot(a, b, dims=NN):
    return lax.dot_general(a, b, dims, preferred_element_type=F32)


def _split2(a):
    hi = a.astype(BF16)
    lo = (a - hi.astype(F32)).astype(BF16)
    return hi, lo


def _dot_hp(a, b, dims=NN):
    a_hi, a_lo = _split2(a)
    b_hi, b_lo = _split2(b)
    return _dot(a_hi, b_hi, dims) + (_dot(a_hi, b_lo, dims) + _dot(a_lo, b_hi, dims))


def _dot_exact_lhs(a_bf16, b, dims=NN):
    b0 = b.astype(BF16)
    r1 = b - b0.astype(F32)
    b1 = r1.astype(BF16)
    b2 = (r1 - b1.astype(F32)).astype(BF16)
    return _dot(a_bf16, b0, dims) + (_dot(a_bf16, b1, dims) + _dot(a_bf16, b2, dims))


def _dot_exact_rhs(a, b_bf16, dims=NN):
    a0 = a.astype(BF16)
    r1 = a - a0.astype(F32)
    a1 = r1.astype(BF16)
    a2 = (r1 - a1.astype(F32)).astype(BF16)
    return _dot(a0, b_bf16, dims) + (_dot(a1, b_bf16, dims) + _dot(a2, b_bf16, dims))


def _layer_norm(y, g, b):
    mu = jnp.mean(y, axis=-1, keepdims=True)
    yc = y - mu
    var = jnp.mean(yc * yc, axis=-1, keepdims=True)
    return yc * lax.rsqrt(var + LN_EPS) * g + b


def _silu(x):
    return x * jax.nn.sigmoid(x)


def _params(*sem):
    return pltpu.CompilerParams(dimension_semantics=sem, vmem_limit_bytes=VMEM_LIMIT_BYTES)


FFN_ROWS = 512
FFN_ROWS_LAST = 256


def _ffn_kernel(x_ref, wg_ref, wu_ref, wd_ref, g_ref, b_ref, o_ref, xb_ref):
    j = pl.program_id(1)
    last_j = pl.num_programs(1) - 1

    def step(first, last):
        sub = FFN_ROWS_LAST if last else FFN_ROWS
        for r in range(0, o_ref.shape[0], sub):
            rows = slice(r, r + sub)
            if first:
                xb_ref[rows, :] = x_ref[rows, :].astype(BF16)
            xb = xb_ref[rows, :]
            gate = _dot(xb, wg_ref[...])
            up = _dot(xb, wu_ref[...])
            h = (_silu(gate) * up).astype(BF16)
            part = _dot(h, wd_ref[...])
            acc = part if first else o_ref[rows, :] + part
            if last:
                y = DEEPNORM_ALPHA * x_ref[rows, :] + 0.5 * acc
                acc = _layer_norm(y, g_ref[...], b_ref[...])
            o_ref[rows, :] = acc

    pl.when(j == 0)(functools.partial(step, True, False))
    pl.when((j > 0) & (j < last_j))(functools.partial(step, False, False))
    pl.when(j == last_j)(functools.partial(step, False, True))


def _ffn_ln(x, wg, wu, wd, ln_g, ln_b, tm=1024, tf=512):
    t, d = x.shape
    dff = wg.shape[1]
    assert dff // tf >= 2
    return pl.pallas_call(
        _ffn_kernel,
        grid=(t // tm, dff // tf),
        in_specs=[
            pl.BlockSpec((tm, d), lambda i, j: (i, 0)),
            pl.BlockSpec((d, tf), lambda i, j: (0, j)),
            pl.BlockSpec((d, tf), lambda i, j: (0, j)),
            pl.BlockSpec((tf, d), lambda i, j: (j, 0)),
            pl.BlockSpec((1, d), lambda i, j: (0, 0)),
            pl.BlockSpec((1, d), lambda i, j: (0, 0)),
        ],
        out_specs=pl.BlockSpec((tm, d), lambda i, j: (i, 0)),
        out_shape=jax.ShapeDtypeStruct((t, d), F32),
        scratch_shapes=[pltpu.VMEM((tm, d), BF16)],
        compiler_params=_params("parallel", "arbitrary"),
        name="ffn_ln",
    )(x, wg, wu, wd, ln_g.reshape(1, d), ln_b.reshape(1, d))


def _cast_job_specs(jobs, grid):
    n_steps = math.prod(grid)
    strides = [math.prod(grid[a + 1:]) for a in range(len(grid))]
    step = lambda *ids: sum(i * s for i, s in zip(ids, strides))
    in_specs, out_specs, out_shapes = [], [], []
    for w, layer in jobs:
        _, r, c = w.shape
        slab = r // n_steps
        assert slab * n_steps == r and slab % (2 * SUBLANES) == 0
        in_specs.append(pl.BlockSpec((None, slab, c),
                                     lambda *ids, layer=layer: (layer, step(*ids), 0)))
        out_specs.append(pl.BlockSpec((slab, c), lambda *ids: (step(*ids), 0)))
        out_shapes.append(jax.ShapeDtypeStruct((r, c), BF16))
    return in_specs, out_specs, out_shapes


def _run_cast_jobs(src_refs, dst_refs):
    for src, dst in zip(src_refs, dst_refs):
        dst[...] = src[...].astype(BF16)


AB_TN = 1280
DN_TN = 2048


def _matmul_kernel(x_ref, w_ref, o_ref, xb_ref, *, w_is_transposed):
    @pl.when(pl.program_id(1) == 0)
    def _():
        xb_ref[...] = x_ref[...].astype(BF16)

    o_ref[...] = _dot(xb_ref[...], w_ref[...], NT if w_is_transposed else NN)


def _matmul(x, w, n, w_is_transposed=False, tm=1024, tn=1024):
    t, k = x.shape
    if w_is_transposed:
        w_spec = pl.BlockSpec((None, tn, k), lambda i, j: (0, j, 0))
    else:
        w_spec = pl.BlockSpec((None, k, tn), lambda i, j: (0, 0, j))
    return pl.pallas_call(
        functools.partial(_matmul_kernel, w_is_transposed=w_is_transposed),
        grid=(t // tm, n // tn),
        in_specs=[pl.BlockSpec((tm, k), lambda i, j: (i, 0)), w_spec],
        out_specs=pl.BlockSpec((tm, tn), lambda i, j: (i, j)),
        out_shape=jax.ShapeDtypeStruct((t, n), F32),
        scratch_shapes=[pltpu.VMEM((tm, k), BF16)],
        compiler_params=_params("parallel", "arbitrary"),
        name="in_proj",
    )(x, w)


PROJ_ROWS = 256


def _proj_ln_kernel(*refs, n_in):
    a_refs = refs[:n_in]
    w_ref, x_ref, g_ref, b_ref, o_ref = refs[n_in:]
    for r in range(0, o_ref.shape[0], PROJ_ROWS):
        rows = slice(r, r + PROJ_ROWS)
        acc = None
        row = 0
        for a_ref in a_refs:
            kc = a_ref.shape[1]
            part = _dot(a_ref[rows, :], w_ref[row:row + kc, :])
            acc = part if acc is None else acc + part
            row += kc
        y = DEEPNORM_ALPHA * x_ref[rows, :] + acc
        o_ref[rows, :] = _layer_norm(y, g_ref[...], b_ref[...])


def _proj_ln(a_parts, w, x, ln_g, ln_b, tm=512):
    t, d = x.shape
    k = w.shape[1]
    in_specs = [pl.BlockSpec((tm, a.shape[1]), lambda i: (i, 0)) for a in a_parts]
    in_specs += [
        pl.BlockSpec((None, k, d), lambda i: (0, 0, 0), pipeline_mode=pl.Buffered(1)),
        pl.BlockSpec((tm, d), lambda i: (i, 0)),
        pl.BlockSpec((1, d), lambda i: (0, 0)),
        pl.BlockSpec((1, d), lambda i: (0, 0)),
    ]
    return pl.pallas_call(
        functools.partial(_proj_ln_kernel, n_in=len(a_parts)),
        grid=(t // tm,),
        in_specs=in_specs,
        out_specs=pl.BlockSpec((tm, d), lambda i: (i, 0)),
        out_shape=jax.ShapeDtypeStruct((t, d), F32),
        compiler_params=_params("parallel"),
        name="proj_ln",
    )(*a_parts, w, x, ln_g.reshape(1, d), ln_b.reshape(1, d))


def _rope(x, cos_full, sin_signed):
    return x * cos_full + pltpu.roll(x, HEAD_DIM // 2, 1) * sin_signed


def _moba_kernel(q_ref, k_ref, v_ref, cos_ref, sin_ref, *rest, n_cast):
    o_ref = rest[n_cast]
    _run_cast_jobs(rest[:n_cast], rest[n_cast + 1:])
    s = q_ref.shape[0]
    nb = s // MOBA_BLOCK
    blk = MOBA_BLOCK
    cos_full = cos_ref[...]
    sin_signed = sin_ref[...]
    qr = _rope(q_ref[...], cos_full, sin_signed)
    kr = _rope(k_ref[...], cos_full, sin_signed)

    row = lax.broadcasted_iota(jnp.int32, (HEAD_DIM, s), 0)
    col = lax.broadcasted_iota(jnp.int32, (HEAD_DIM, s), 1)
    avg = jnp.where(col // blk == row, 1.0 / blk, 0.0).astype(BF16)
    k_mean = _dot_exact_lhs(avg, kr)
    gate_t = _dot_hp(k_mean, qr, NT)

    qb = qr.astype(BF16)
    kb = kr.astype(BF16)
    vb = v_ref[...].astype(BF16)
    scale = HEAD_DIM ** -0.5

    assert nb <= SUBLANES and blk == 2 * HEAD_DIM
    blk_row = lax.broadcasted_iota(jnp.int32, (SUBLANES, blk), 0)
    qpos = lax.broadcasted_iota(jnp.int32, (blk, blk), 0)
    kpos = lax.broadcasted_iota(jnp.int32, (blk, blk), 1)
    causal = kpos <= qpos
    eye_q = jnp.where(kpos == qpos, 1.0, 0.0).astype(BF16)
    pad_rows = jnp.zeros((HEAD_DIM - SUBLANES, blk), F32)

    for i in range(nb):
        rows = slice(i * blk, (i + 1) * blk)
        qi = qb[rows]
        sel = None
        if i > MOBA_TOPK:
            g = gate_t[0:SUBLANES, rows]
            rank = jnp.zeros((SUBLANES, blk), F32)
            for m_blk in range(i):
                gm = g[m_blk:m_blk + 1, :]
                beats = (gm > g) | ((gm == g) & (m_blk < blk_row))
                rank = rank + jnp.where(beats, 1.0, 0.0)
            sel_t = jnp.where((rank < float(MOBA_TOPK)) & (blk_row < i), 1.0, 0.0)
            sel_t = jnp.concatenate([sel_t, pad_rows], axis=0).astype(BF16)
            sel = _dot(eye_q, sel_t, NT)
        scores = []
        for n in range(i + 1):
            sn = _dot(qi, kb[n * blk:(n + 1) * blk], NT) * scale
            if n == i:
                sn = jnp.where(causal, sn, NEG_INF)
            elif sel is not None:
                selected = jnp.broadcast_to(sel[:, n:n + 1], sn.shape) > 0.5
                sn = jnp.where(selected, sn, NEG_INF)
            scores.append(sn)
        m_acc = None
        for sn in scores:
            folded = jnp.maximum(sn[:, :HEAD_DIM], sn[:, HEAD_DIM:])
            m_acc = folded if m_acc is None else jnp.maximum(m_acc, folded)
        m = m_acc.max(axis=-1, keepdims=True)
        l_acc = jnp.zeros((blk, HEAD_DIM), F32)
        acc = jnp.zeros((blk, HEAD_DIM), F32)
        for n, sn in enumerate(scores):
            p = jnp.exp(sn - m)
            l_acc = l_acc + (p[:, :HEAD_DIM] + p[:, HEAD_DIM:])
            acc = acc + _dot(p.astype(BF16), vb[n * blk:(n + 1) * blk])
        l = l_acc.sum(axis=-1, keepdims=True)
        o_ref[rows, :] = (acc / l).astype(o_ref.dtype)


def _moba(hcat, cos_full, sin_signed, batch, cast_jobs=()):
    t = hcat.shape[0]
    s = t // batch
    grid = (batch, A_HEADS)
    spec = lambda off: pl.BlockSpec((s, HEAD_DIM), lambda b, h: (b, off + h))
    tab = pl.BlockSpec((s, HEAD_DIM), lambda b, h: (0, 0), pipeline_mode=pl.Buffered(1))
    cast_in, cast_out, cast_shapes = _cast_job_specs(cast_jobs, grid)
    outs = pl.pallas_call(
        functools.partial(_moba_kernel, n_cast=len(cast_jobs)),
        grid=grid,
        in_specs=[spec(0), spec(A_HEADS), spec(2 * A_HEADS), tab, tab] + cast_in,
        out_specs=[pl.BlockSpec((s, HEAD_DIM), lambda b, h: (b, h))] + cast_out,
        out_shape=[jax.ShapeDtypeStruct((t, A_WIDTH), BF16)] + cast_shapes,
        compiler_params=_params("parallel", "parallel"),
        name="moba",
    )(hcat, hcat, hcat, cos_full, sin_signed, *[job[0] for job in cast_jobs])
    return outs[0], outs[1:]


def _gmlp_kernel(u_ref, v_ref, lng_ref, lnb_ref, ws_ref, bs_ref, o_ref):
    rows = u_ref.shape[0]
    tpos = lax.broadcasted_iota(jnp.int32, (GMLP_CHUNK, GMLP_CHUNK), 0)
    spos = lax.broadcasted_iota(jnp.int32, (GMLP_CHUNK, GMLP_CHUNK), 1)
    causal = spos <= tpos
    for g in range(G_GROUPS):
        cols = slice(g * G_DIM, (g + 1) * G_DIM)
        w = jnp.where(causal, ws_ref[g], 0.0).astype(BF16)
        bias = bs_ref[:, g:g + 1]
        ln_g = lng_ref[:, cols]
        ln_b = lnb_ref[:, cols]
        for c in range(rows // GMLP_CHUNK):
            r = slice(c * GMLP_CHUNK, (c + 1) * GMLP_CHUNK)
            vn = _layer_norm(jax.nn.gelu(v_ref[r, cols]), ln_g, ln_b)
            mixed = _dot(w, vn.astype(BF16)) + bias
            o_ref[r, cols] = (jax.nn.gelu(u_ref[r, cols]) * mixed).astype(o_ref.dtype)


def _gmlp(hcat, ln_g, ln_b, w_s, b_s, rows=512):
    t = hcat.shape[0]
    u_blk = 3 * A_WIDTH // G_WIDTH
    return pl.pallas_call(
        _gmlp_kernel,
        grid=(t // rows,),
        in_specs=[
            pl.BlockSpec((rows, G_WIDTH), lambda i: (i, u_blk)),
            pl.BlockSpec((rows, G_WIDTH), lambda i: (i, u_blk + 1)),
            pl.BlockSpec((1, G_WIDTH), lambda i: (0, 0)),
            pl.BlockSpec((1, G_WIDTH), lambda i: (0, 0)),
            pl.BlockSpec((G_GROUPS, GMLP_CHUNK, GMLP_CHUNK), lambda i: (0, 0, 0)),
            pl.BlockSpec((GMLP_CHUNK, G_GROUPS), lambda i: (0, 0)),
        ],
        out_specs=pl.BlockSpec((rows, G_WIDTH), lambda i: (i, 0)),
        out_shape=jax.ShapeDtypeStruct((t, G_WIDTH), BF16),
        compiler_params=_params("parallel"),
        name="gmlp",
    )(hcat, hcat, ln_g.reshape(1, G_WIDTH), ln_b.reshape(1, G_WIDTH), w_s, b_s.T)


GATE_BLK = 128


def _softplus(x):
    return jnp.maximum(x, 0.0) + jnp.log1p(jnp.exp(-jnp.abs(x)))


def _dn_gate_kernel(x_ref, w_ref, wt_ref, ng_ref, dtb_ref, ngc_ref, dtbc_ref,
                    bg_ref, gc_ref, gct_ref):
    s = x_ref.shape[0]
    hv = DN_V_HEADS
    x = x_ref[...].astype(BF16)
    ba = _dot(x, w_ref[...])
    bat = _dot(wt_ref[...], x, NT)
    lane = lax.broadcasted_iota(jnp.int32, (s, 2 * hv), 1)
    g = ng_ref[...] * _softplus(ba + dtb_ref[...])
    bg = jnp.where(lane < hv, jax.nn.sigmoid(ba), g)
    bg_ref[...] = bg
    gt = ngc_ref[...] * _softplus(bat + dtbc_ref[...])

    r = lax.broadcasted_iota(jnp.int32, (GATE_BLK, GATE_BLK), 0)
    c = lax.broadcasted_iota(jnp.int32, (GATE_BLK, GATE_BLK), 1)
    same = (r // DN_CHUNK) == (c // DN_CHUNK)
    lower = jnp.where(same & (c <= r), 1.0, 0.0).astype(BF16)
    upper = jnp.where(same & (r <= c), 1.0, 0.0).astype(BF16)
    for i in range(s // GATE_BLK):
        sl = slice(i * GATE_BLK, (i + 1) * GATE_BLK)
        gc_ref[sl, :] = _dot_exact_lhs(lower, bg[sl, :])
        gct_ref[:, sl] = _dot_exact_rhs(gt[:, sl], upper)


def _dn_gates(x, w_ba_t, a_log, dt_bias, rows=512):
    t, d = x.shape
    s = rows
    hv = DN_V_HEADS
    neg_rate = -jnp.exp(a_log.astype(F32))
    zeros = jnp.zeros((hv,), F32)
    ng = jnp.concatenate([zeros, neg_rate]).reshape(1, 2 * hv)
    dtb = jnp.concatenate([zeros, dt_bias.astype(F32)]).reshape(1, 2 * hv)
    small = lambda shape: pl.BlockSpec(shape, lambda b: (0, 0))
    return pl.pallas_call(
        _dn_gate_kernel,
        grid=(t // rows,),
        in_specs=[
            pl.BlockSpec((s, d), lambda b: (b, 0)),
            small((d, 2 * hv)),
            small((2 * hv, d)),
            small((1, 2 * hv)),
            small((1, 2 * hv)),
            small((2 * hv, 1)),
            small((2 * hv, 1)),
        ],
        out_specs=[
            pl.BlockSpec((s, 2 * hv), lambda b: (b, 0)),
            pl.BlockSpec((s, 2 * hv), lambda b: (b, 0)),
            pl.BlockSpec((2 * hv, s), lambda b: (0, b)),
        ],
        out_shape=[
            jax.ShapeDtypeStruct((t, 2 * hv), F32),
            jax.ShapeDtypeStruct((t, 2 * hv), F32),
            jax.ShapeDtypeStruct((2 * hv, t), F32),
        ],
        compiler_params=_params("parallel"),
        name="dn_gates",
    )(x, w_ba_t.T, w_ba_t, ng, dtb, ng.reshape(2 * hv, 1), dtb.reshape(2 * hv, 1))


DN_TB = 128
DN_HB = 32
CONV_PAD = 8


def _conv_stage(x_ref, buf_ref, first):
    tb = x_ref.shape[0]

    @pl.when(first)
    def _():
        buf_ref[0:CONV_PAD, :] = jnp.zeros((CONV_PAD, buf_ref.shape[1]), F32)

    buf_ref[CONV_PAD:CONV_PAD + tb, :] = x_ref[...]


def _conv_silu(buf_ref, w_ref, cols, tb):
    y = buf_ref[CONV_PAD:CONV_PAD + tb, cols] * w_ref[DN_CONV - 1:DN_CONV, cols]
    for j in range(1, DN_CONV):
        tap = DN_CONV - 1 - j
        y = y + buf_ref[CONV_PAD - j:CONV_PAD - j + tb, cols] * w_ref[tap:tap + 1, cols]
    return _silu(y)


def _conv_keep_tail(buf_ref, tb):
    buf_ref[0:CONV_PAD, :] = buf_ref[tb:tb + CONV_PAD, :]


def _hi_lo(a):
    hi = a.astype(BF16).astype(F32)
    return hi, a - hi


def _pair_lhs(x):
    hi, lo = _hi_lo(x)
    return jnp.concatenate([hi, lo], axis=1).astype(BF16)


def _pair_blockdiag(y, first_half):
    top = jnp.where(first_half, y, 0.0)
    return jnp.concatenate([top, y - top], axis=0)


def _pair_product(x, y, first_half):
    return _dot(x.astype(BF16), _pair_blockdiag(y, first_half).astype(BF16))


def _pair_product_hp(x, y, first_half):
    width = x.shape[1]
    y_hi, y_lo = _hi_lo(y)
    bd = jnp.concatenate([_pair_blockdiag(y_hi, first_half),
                          _pair_blockdiag(y_lo, first_half)], axis=1).astype(BF16)
    out = _dot(_pair_lhs(x), jnp.concatenate([bd, bd], axis=0))
    return out[:, :width] + out[:, width:]


def _delta_kernel(*refs):
    for group in range(DN_V_HEADS // DN_HB):
        pl.when(pl.program_id(1) == group)(functools.partial(_delta_group, group, *refs))


def _delta_group(group, q_ref, k_ref, v_ref, z_ref, wq_ref, wk_ref, wv_ref, bg_ref, gc_ref,
                 gct_ref, ng_ref, o_ref, state_ref, qbuf_ref, kbuf_ref, vbuf_ref):
    tb = q_ref.shape[0]
    hv = DN_V_HEADS
    c = DN_CHUNK
    dk = DN_HEAD_DIM
    rep = DN_V_HEADS // DN_QK_HEADS
    first = pl.program_id(2) == 0

    @pl.when(first)
    def _():
        state_ref[...] = jnp.zeros_like(state_ref)

    _conv_stage(q_ref, qbuf_ref, first)
    _conv_stage(k_ref, kbuf_ref, first)
    _conv_stage(v_ref, vbuf_ref, first)

    assert tb == 2 * c
    ri = lax.broadcasted_iota(jnp.int32, (c, 2 * c), 0)
    ci2 = lax.broadcasted_iota(jnp.int32, (c, 2 * c), 1)
    first_half = ci2 < c
    ci = jnp.where(first_half, ci2, ci2 - c)
    incl = ci <= ri
    strict = ci < ri
    eye = jnp.where(ci == ri, 1.0, 0.0)
    norm_g = ng_ref[...]
    n_qk = DN_HB // rep
    heads = range(DN_HB)


    q_n, k_n = [], []
    for jq in range(n_qk):
        qcols = slice(jq * dk, (jq + 1) * dk)
        q_h = _conv_silu(qbuf_ref, wq_ref, qcols, tb)
        k_h = _conv_silu(kbuf_ref, wk_ref, qcols, tb)
        q_n.append(q_h * lax.rsqrt(jnp.sum(q_h * q_h, axis=-1, keepdims=True) + RMS_EPS)
                   * (dk ** -0.5))
        k_n.append(k_h * lax.rsqrt(jnp.sum(k_h * k_h, axis=-1, keepdims=True) + RMS_EPS))

    qk_pair, kk_pair = [], []
    for jq in range(n_qk):
        q0, q1 = q_n[jq][:c], q_n[jq][c:]
        k0, k1 = k_n[jq][:c], k_n[jq][c:]
        zero = jnp.zeros_like(k0)
        lhs = jnp.concatenate([jnp.concatenate([q0, q1], axis=1),
                               jnp.concatenate([k0, k1], axis=1)], axis=0).astype(BF16)
        rhs = jnp.concatenate([jnp.concatenate([k0, zero], axis=1),
                               jnp.concatenate([zero, k1], axis=1)], axis=0).astype(BF16)
        g = _dot(lhs, rhs, NT)
        qk_pair.append(g[:c])
        kk_pair.append(g[c:])

    a_pair, attn_bf, kbg_bf, q_dec, k_dec, g_last, betas = [], [], [], [], [], [], []
    for hl in heads:
        jq = hl // rep
        head = group * DN_HB + hl
        beta_col = bg_ref[:, head:head + 1]
        betas.append(beta_col)
        gc_col = gc_ref[:, hv + head:hv + head + 1]
        gc_row = gct_ref[head:head + 1, :]
        beta_sel = jnp.where(first_half, beta_col[:c], beta_col[c:])
        gc_sel = jnp.where(first_half, gc_col[:c], gc_col[c:])
        decay = jnp.exp(jnp.where(incl, gc_sel - gc_row, -jnp.inf))
        a_pair.append(jnp.where(strict, kk_pair[jq] * beta_sel * decay, 0.0))
        attn_bf.append((qk_pair[jq] * decay).astype(BF16))
        egc = jnp.exp(gc_col)
        kbg_bf.append((k_n[jq] * beta_col * egc).astype(BF16))
        q_dec.append((q_n[jq] * egc).astype(BF16))
        gl = [gc_col[c - 1:c, :], gc_col[tb - 1:tb, :]]
        gl_rows = jnp.concatenate([jnp.broadcast_to(gl[0], (c, 1)),
                                   jnp.broadcast_to(gl[1], (c, 1))], axis=0)
        k_dec.append((k_n[jq] * jnp.exp(gl_rows - gc_col)).astype(BF16))
        g_last.append([jnp.exp(gl[0]), jnp.exp(gl[1])])

    vb_bf, z_gate = [], []

    def value_conv(hl):
        vcols = slice(hl * dk, (hl + 1) * dk)
        vb_bf.append((_conv_silu(vbuf_ref, wv_ref, vcols, tb) * betas[hl]).astype(BF16))

    def output_gate(hl):
        z_gate.append(norm_g * _silu(z_ref[:, hl * dk:(hl + 1) * dk]))

    independent = ([functools.partial(value_conv, hl) for hl in heads]
                   + [functools.partial(output_gate, hl) for hl in heads])
    n_slots = 2 * (int(math.log2(c)) - 1) + 2
    per_slot = -(-len(independent) // n_slots)

    def fill():
        for work in independent[:per_slot]:
            work()
        del independent[:per_slot]

    inv = [eye - jnp.where((ri // 2) == (ci // 2), a, 0.0) for a in a_pair]
    s = 2
    while s < c:
        off = ((ri // (2 * s)) == (ci // (2 * s))) & ((ri // s) != (ci // s))
        m1 = [_pair_product(jnp.where(off, a_pair[hl], 0.0), inv[hl], first_half)
              for hl in heads]
        fill()
        m2 = [_pair_product(inv[hl], m1[hl], first_half) for hl in heads]
        fill()
        inv = [inv[hl] - m2[hl] for hl in heads]
        s *= 2
    ax = [_pair_product_hp(a_pair[hl], inv[hl], first_half) for hl in heads]
    fill()
    resid = [eye - inv[hl] - ax[hl] for hl in heads]
    corr = [_pair_product(inv[hl], resid[hl], first_half) for hl in heads]
    fill()
    inv = [inv[hl] + corr[hl] for hl in heads]
    assert not independent

    u, w_bf = [], []
    zero_rows = jnp.zeros((c, 2 * dk), BF16)
    for hl in heads:
        lhs = _pair_lhs(inv[hl])
        rhs = jnp.concatenate([vb_bf[hl], kbg_bf[hl]], axis=1)
        r0, r1 = rhs[:c], rhs[c:]
        uw0 = _dot(lhs, jnp.concatenate([r0, zero_rows, r0, zero_rows], axis=0))
        uw1 = _dot(lhs, jnp.concatenate([zero_rows, r1, zero_rows, r1], axis=0))
        u.append([uw0[:, :dk], uw1[:, :dk]])
        w_bf.append([uw0[:, dk:].astype(BF16), uw1[:, dk:].astype(BF16)])

    states = [state_ref[hl] for hl in heads]
    zero_v = jnp.zeros((c, dk), BF16)
    for ch in range(2):
        rows = slice(ch * c, (ch + 1) * c)
        ws_qs = [_dot(jnp.concatenate([w_bf[hl][ch], q_dec[hl][rows]], axis=0),
                      states[hl].astype(BF16)) for hl in heads]
        v_new_bf = [(u[hl][ch] - ws_qs[hl][:c]).astype(BF16) for hl in heads]
        o_intra = []
        for hl in heads:
            v_rows = [v_new_bf[hl], zero_v] if ch == 0 else [zero_v, v_new_bf[hl]]
            o_intra.append(_dot(attn_bf[hl], jnp.concatenate(v_rows, axis=0)))
        states = [states[hl] * g_last[hl][ch] + _dot(k_dec[hl][rows], v_new_bf[hl], TN)
                  for hl in heads]
        for hl in heads:
            vcols = slice(hl * dk, (hl + 1) * dk)
            o_c = ws_qs[hl][c:] + o_intra[hl]
            o_n = (o_c * lax.rsqrt(jnp.mean(o_c * o_c, axis=-1, keepdims=True) + RMS_EPS)
                   * z_gate[hl][rows])
            o_ref[rows, vcols] = o_n.astype(o_ref.dtype)
    for hl in heads:
        state_ref[hl] = states[hl]
    _conv_keep_tail(qbuf_ref, tb)
    _conv_keep_tail(kbuf_ref, tb)
    _conv_keep_tail(vbuf_ref, tb)


def _delta(hcat, conv_w, bg, gcs, gct, norm_g, batch):
    t = hcat.shape[0]
    s = t // batch
    nt = s // DN_TB
    rep = DN_V_HEADS // DN_QK_HEADS
    wq = DN_HB // rep * DN_HEAD_DIM
    wv = DN_HB * DN_HEAD_DIM
    k_off = DN_QK_WIDTH // wq
    v_off = 2 * DN_QK_WIDTH // wv
    z_off = DN_CONV_DIM // wv
    hv2 = 2 * DN_V_HEADS
    row = lambda b, g, i: b * nt + i
    return pl.pallas_call(
        _delta_kernel,
        grid=(batch, DN_V_HEADS // DN_HB, nt),
        in_specs=[
            pl.BlockSpec((DN_TB, wq), lambda b, g, i: (row(b, g, i), g)),
            pl.BlockSpec((DN_TB, wq), lambda b, g, i: (row(b, g, i), k_off + g)),
            pl.BlockSpec((DN_TB, wv), lambda b, g, i: (row(b, g, i), v_off + g)),
            pl.BlockSpec((DN_TB, wv), lambda b, g, i: (row(b, g, i), z_off + g)),
            pl.BlockSpec((DN_CONV, wq), lambda b, g, i: (0, g)),
            pl.BlockSpec((DN_CONV, wq), lambda b, g, i: (0, k_off + g)),
            pl.BlockSpec((DN_CONV, wv), lambda b, g, i: (0, v_off + g)),
            pl.BlockSpec((DN_TB, hv2), lambda b, g, i: (row(b, g, i), 0)),
            pl.BlockSpec((DN_TB, hv2), lambda b, g, i: (row(b, g, i), 0)),
            pl.BlockSpec((DN_V_HEADS, DN_TB), lambda b, g, i: (1, row(b, g, i))),
            pl.BlockSpec((1, DN_HEAD_DIM), lambda b, g, i: (0, 0)),
        ],
        out_specs=pl.BlockSpec((DN_TB, wv), lambda b, g, i: (row(b, g, i), g)),
        out_shape=jax.ShapeDtypeStruct((t, DN_V_WIDTH), BF16),
        scratch_shapes=[
            pltpu.VMEM((DN_HB, DN_HEAD_DIM, DN_HEAD_DIM), F32),
            pltpu.VMEM((CONV_PAD + DN_TB, wq), F32),
            pltpu.VMEM((CONV_PAD + DN_TB, wq), F32),
            pltpu.VMEM((CONV_PAD + DN_TB, wv), F32),
        ],
        compiler_params=_params("parallel", "parallel", "arbitrary"),
        name="delta_rule",
    )(hcat, hcat, hcat, hcat, conv_w, conv_w, conv_w, bg, gcs, gct,
      norm_g.reshape(1, DN_HEAD_DIM))


def _rope_tables(s):
    half = HEAD_DIM // 2
    inv = jnp.exp(-math.log(ROPE_THETA) * jnp.arange(half, dtype=F32) * (2.0 / HEAD_DIM))
    ang = jnp.arange(s, dtype=F32)[:, None] * inv[None, :]
    cos, sin = jnp.cos(ang), jnp.sin(ang)
    return jnp.concatenate([cos, cos], axis=-1), jnp.concatenate([-sin, sin], axis=-1)


def kernel(x, ffn1_w_gate, ffn1_w_up, ffn1_w_down, ffn2_w_gate, ffn2_w_up, ffn2_w_down, ln_g, ln_b, ab_w_in, ab_gmlp_ln_g, ab_gmlp_ln_b, ab_gmlp_w_s, ab_gmlp_b_s, ab_w_out, dn_w_in, dn_conv_w, dn_a_log, dn_dt_bias, dn_norm_g, dn_w_out):
    batch, s, d = x.shape
    t = batch * s
    bf = lambda w: w.astype(BF16)
    xf = x.reshape(t, d)
    cos_full, sin_signed = _rope_tables(s)
    ab_in, ab_out, dn_out = bf(ab_w_in), bf(ab_w_out), bf(dn_w_out)
    dn_in_t = bf(jnp.swapaxes(dn_w_in, 1, 2))

    ffn_f32 = {1: (ffn1_w_gate, ffn1_w_up, ffn1_w_down), 2: (ffn2_w_gate, ffn2_w_up, ffn2_w_down)}
    ffn_bf16 = {}
    pending = [(half, layer) for layer in range(DEPTH) for half in (1, 2)]

    def cast_now(key):
        half, layer = key
        ffn_bf16[key] = tuple(bf(w[layer]) for w in ffn_f32[half])

    def take_cast_jobs():
        keys = list(pending)
        del pending[:]
        return keys, tuple((w, layer) for half, layer in keys for w in ffn_f32[half])

    def finish_cast_jobs(keys, results):
        for n, key in enumerate(keys):
            ffn_bf16[key] = tuple(results[3 * n:3 * n + 3])

    def half_step(xf, half, layer, slot):
        key = (half, layer)
        if key not in ffn_bf16:
            pending.remove(key)
            cast_now(key)
        return _ffn_ln(xf, *ffn_bf16[key], ln_g[layer, slot], ln_b[layer, slot])

    cast_now(pending.pop(0))
    for i in range(DEPTH):
        xf = half_step(xf, 1, i, 0)
        j = i // 2
        if i % 2 == 0:
            hcat = _matmul(xf, ab_in[j:j + 1], AB_IN, tn=AB_TN)
            keys, jobs = take_cast_jobs()
            a_out, done = _moba(hcat, cos_full, sin_signed, batch, cast_jobs=jobs)
            finish_cast_jobs(keys, done)
            g_out = _gmlp(hcat, ab_gmlp_ln_g[j], ab_gmlp_ln_b[j], ab_gmlp_w_s[j], ab_gmlp_b_s[j])
            mix, w_out = [a_out, g_out], ab_out[j:j + 1]
        else:
            hcat = _matmul(xf, dn_in_t[j:j + 1], DN_MAIN, w_is_transposed=True, tn=DN_TN)
            bg, gcs, gct = _dn_gates(xf, dn_in_t[j, DN_MAIN:], dn_a_log[j], dn_dt_bias[j])
            mix = [_delta(hcat, dn_conv_w[j], bg, gcs, gct, dn_norm_g[j], batch)]
            w_out = dn_out[j:j + 1]
        xf = _proj_ln(mix, w_out, xf, ln_g[i, 1], ln_b[i, 1])
        xf = half_step(xf, 2, i, 2)
    return xf.reshape(batch, s, d)
```

```python
import functools
import math

import jax
import jax.numpy as jnp
from jax import lax
from jax.experimental import pallas as pl
from jax.experimental.pallas import tpu as pltpu

F32 = jnp.float32
BF16 = jnp.bfloat16

D_MODEL = 2048
SEQ = 2048
DEPTH = 2
HEAD_DIM = 128
A_HEADS = 8
A_WIDTH = A_HEADS * HEAD_DIM
MOBA_BLOCK = 256
MOBA_TOPK = 3
ROPE_THETA = 10000.0
G_GROUPS = 8
G_DIM = 128
G_WIDTH = G_GROUPS * G_DIM
GMLP_CHUNK = 128
AB_IN = 3 * A_WIDTH + 2 * G_WIDTH
DN_QK_HEADS = 16
DN_V_HEADS = 32
DN_HEAD_DIM = 128
DN_QK_WIDTH = DN_QK_HEADS * DN_HEAD_DIM
DN_V_WIDTH = DN_V_HEADS * DN_HEAD_DIM
DN_CONV_DIM = 2 * DN_QK_WIDTH + DN_V_WIDTH
DN_MAIN = DN_CONV_DIM + DN_V_WIDTH
DN_CONV = 4
DN_CHUNK = 64
DEEPNORM_ALPHA = (2 * DEPTH) ** 0.25
LN_EPS = 1e-5
RMS_EPS = 1e-6
NEG_INF = -1e30

SUBLANES = 8
VMEM_LIMIT_BYTES = 60 * 1024 * 1024

NN = (((1,), (0,)), ((), ()))
NT = (((1,), (1,)), ((), ()))
TN = (((0,), (0,)), ((), ()))


def _dot(a, b, dims=NN):
    return lax.dot_general(a, b, dims, preferred_element_type=F32)


def _split2(a):
    hi = a.astype(BF16)
    lo = (a - hi.astype(F32)).astype(BF16)
    return hi, lo


def _dot_hp(a, b, dims=NN):
    a_hi, a_lo = _split2(a)
    b_hi, b_lo = _split2(b)
    return _dot(a_hi, b_hi, dims) + (_dot(a_hi, b_lo, dims) + _dot(a_lo, b_hi, dims))


def _dot_exact_lhs(a_bf16, b, dims=NN):
    b0 = b.astype(BF16)
    r1 = b - b0.astype(F32)
    b1 = r1.astype(BF16)
    b2 = (r1 - b1.astype(F32)).astype(BF16)
    return _dot(a_bf16, b0, dims) + (_dot(a_bf16, b1, dims) + _dot(a_bf16, b2, dims))


def _dot_exact_rhs(a, b_bf16, dims=NN):
    a0 = a.astype(BF16)
    r1 = a - a0.astype(F32)
    a1 = r1.astype(BF16)
    a2 = (r1 - a1.astype(F32)).astype(BF16)
    return _dot(a0, b_bf16, dims) + (_dot(a1, b_bf16, dims) + _dot(a2, b_bf16, dims))


def _layer_norm(y, g, b):
    mu = jnp.mean(y, axis=-1, keepdims=True)
    yc = y - mu
    var = jnp.mean(yc * yc, axis=-1, keepdims=True)
    return yc * lax.rsqrt(var + LN_EPS) * g + b


def _silu(x):
    return x * jax.nn.sigmoid(x)


def _params(*sem):
    return pltpu.CompilerParams(dimension_semantics=sem, vmem_limit_bytes=VMEM_LIMIT_BYTES)


FFN_ROWS = 512
FFN_ROWS_LAST = 256


def _ffn_kernel(x_ref, wg_ref, wu_ref, wd_ref, g_ref, b_ref, o_ref, xb_ref):
    j = pl.program_id(1)
    last_j = pl.num_programs(1) - 1

    def step(first, last):
        sub = FFN_ROWS_LAST if last else FFN_ROWS
        for r in range(0, o_ref.shape[0], sub):
            rows = slice(r, r + sub)
            if first:
                xb_ref[rows, :] = x_ref[rows, :].astype(BF16)
            xb = xb_ref[rows, :]
            gate = _dot(xb, wg_ref[...])
            up = _dot(xb, wu_ref[...])
            h = (_silu(gate) * up).astype(BF16)
            part = _dot(h, wd_ref[...])
            acc = part if first else o_ref[rows, :] + part
            if last:
                y = DEEPNORM_ALPHA * x_ref[rows, :] + 0.5 * acc
                acc = _layer_norm(y, g_ref[...], b_ref[...])
            o_ref[rows, :] = acc

    pl.when(j == 0)(functools.partial(step, True, False))
    pl.when((j > 0) & (j < last_j))(functools.partial(step, False, False))
    pl.when(j == last_j)(functools.partial(step, False, True))


def _ffn_ln(x, wg, wu, wd, ln_g, ln_b, tm=1024, tf=512):
    t, d = x.shape
    dff = wg.shape[1]
    assert dff // tf >= 2
    return pl.pallas_call(
        _ffn_kernel,
        grid=(t // tm, dff // tf),
        in_specs=[
            pl.BlockSpec((tm, d), lambda i, j: (i, 0)),
            pl.BlockSpec((d, tf), lambda i, j: (0, j)),
            pl.BlockSpec((d, tf), lambda i, j: (0, j)),
            pl.BlockSpec((tf, d), lambda i, j: (j, 0)),
            pl.BlockSpec((1, d), lambda i, j: (0, 0)),
            pl.BlockSpec((1, d), lambda i, j: (0, 0)),
        ],
        out_specs=pl.BlockSpec((tm, d), lambda i, j: (i, 0)),
        out_shape=jax.ShapeDtypeStruct((t, d), F32),
        scratch_shapes=[pltpu.VMEM((tm, d), BF16)],
        compiler_params=_params("parallel", "arbitrary"),
        name="ffn_ln",
    )(x, wg, wu, wd, ln_g.reshape(1, d), ln_b.reshape(1, d))


def _cast_job_specs(jobs, grid):
    n_steps = math.prod(grid)
    strides = [math.prod(grid[a + 1:]) for a in range(len(grid))]
    step = lambda *ids: sum(i * s for i, s in zip(ids, strides))
    in_specs, out_specs, out_shapes = [], [], []
    for w, layer in jobs:
        _, r, c = w.shape
        slab = r // n_steps
        assert slab * n_steps == r and slab % (2 * SUBLANES) == 0
        in_specs.append(pl.BlockSpec((None, slab, c),
                                     lambda *ids, layer=layer: (layer, step(*ids), 0)))
        out_specs.append(pl.BlockSpec((slab, c), lambda *ids: (step(*ids), 0)))
        out_shapes.append(jax.ShapeDtypeStruct((r, c), BF16))
    return in_specs, out_specs, out_shapes


def _run_cast_jobs(src_refs, dst_refs):
    for src, dst in zip(src_refs, dst_refs):
        dst[...] = src[...].astype(BF16)


AB_TN = 1280
DN_TN = 2048


def _matmul_kernel(x_ref, w_ref, o_ref, xb_ref, *, w_is_transposed):
    @pl.when(pl.program_id(1) == 0)
    def _():
        xb_ref[...] = x_ref[...].astype(BF16)

    o_ref[...] = _dot(xb_ref[...], w_ref[...], NT if w_is_transposed else NN)


def _matmul(x, w, n, w_is_transposed=False, tm=1024, tn=1024):
    t, k = x.shape
    if w_is_transposed:
        w_spec = pl.BlockSpec((None, tn, k), lambda i, j: (0, j, 0))
    else:
        w_spec = pl.BlockSpec((None, k, tn), lambda i, j: (0, 0, j))
    return pl.pallas_call(
        functools.partial(_matmul_kernel, w_is_transposed=w_is_transposed),
        grid=(t // tm, n // tn),
        in_specs=[pl.BlockSpec((tm, k), lambda i, j: (i, 0)), w_spec],
        out_specs=pl.BlockSpec((tm, tn), lambda i, j: (i, j)),
        out_shape=jax.ShapeDtypeStruct((t, n), F32),
        scratch_shapes=[pltpu.VMEM((tm, k), BF16)],
        compiler_params=_params("parallel", "arbitrary"),
        name="in_proj",
    )(x, w)


PROJ_ROWS = 256


def _proj_ln_kernel(*refs, n_in):
    a_refs = refs[:n_in]
    w_ref, x_ref, g_ref, b_ref, o_ref = refs[n_in:]
    for r in range(0, o_ref.shape[0], PROJ_ROWS):
        rows = slice(r, r + PROJ_ROWS)
        acc = None
        row = 0
        for a_ref in a_refs:
            kc = a_ref.shape[1]
            part = _dot(a_ref[rows, :], w_ref[row:row + kc, :])
            acc = part if acc is None else acc + part
            row += kc
        y = DEEPNORM_ALPHA * x_ref[rows, :] + acc
        o_ref[rows, :] = _layer_norm(y, g_ref[...], b_ref[...])


def _proj_ln(a_parts, w, x, ln_g, ln_b, tm=512):
    t, d = x.shape
    k = w.shape[1]
    in_specs = [pl.BlockSpec((tm, a.shape[1]), lambda i: (i, 0)) for a in a_parts]
    in_specs += [
        pl.BlockSpec((None, k, d), lambda i: (0, 0, 0), pipeline_mode=pl.Buffered(1)),
        pl.BlockSpec((tm, d), lambda i: (i, 0)),
        pl.BlockSpec((1, d), lambda i: (0, 0)),
        pl.BlockSpec((1, d), lambda i: (0, 0)),
    ]
    return pl.pallas_call(
        functools.partial(_proj_ln_kernel, n_in=len(a_parts)),
        grid=(t // tm,),
        in_specs=in_specs,
        out_specs=pl.BlockSpec((tm, d), lambda i: (i, 0)),
        out_shape=jax.ShapeDtypeStruct((t, d), F32),
        compiler_params=_params("parallel"),
        name="proj_ln",
    )(*a_parts, w, x, ln_g.reshape(1, d), ln_b.reshape(1, d))


def _rope(x, cos_full, sin_signed):
    return x * cos_full + pltpu.roll(x, HEAD_DIM // 2, 1) * sin_signed


def _moba_kernel(q_ref, k_ref, v_ref, cos_ref, sin_ref, *rest, n_cast):
    o_ref = rest[n_cast]
    _run_cast_jobs(rest[:n_cast], rest[n_cast + 1:])
    s = q_ref.shape[0]
    nb = s // MOBA_BLOCK
    blk = MOBA_BLOCK
    cos_full = cos_ref[...]
    sin_signed = sin_ref[...]
    qr = _rope(q_ref[...], cos_full, sin_signed)
    kr = _rope(k_ref[...], cos_full, sin_signed)

    row = lax.broadcasted_iota(jnp.int32, (HEAD_DIM, s), 0)
    col = lax.broadcasted_iota(jnp.int32, (HEAD_DIM, s), 1)
    avg = jnp.where(col // blk == row, 1.0 / blk, 0.0).astype(BF16)
    k_mean = _dot_exact_lhs(avg, kr)
    gate_t = _dot_hp(k_mean, qr, NT)

    qb = qr.astype(BF16)
    kb = kr.astype(BF16)
    vb = v_ref[...].astype(BF16)
    scale = HEAD_DIM ** -0.5

    assert nb <= SUBLANES and blk == 2 * HEAD_DIM
    blk_row = lax.broadcasted_iota(jnp.int32, (SUBLANES, blk), 0)
    qpos = lax.broadcasted_iota(jnp.int32, (blk, blk), 0)
    kpos = lax.broadcasted_iota(jnp.int32, (blk, blk), 1)
    causal = kpos <= qpos
    eye_q = jnp.where(kpos == qpos, 1.0, 0.0).astype(BF16)
    pad_rows = jnp.zeros((HEAD_DIM - SUBLANES, blk), F32)

    for i in range(nb):
        rows = slice(i * blk, (i + 1) * blk)
        qi = qb[rows]
        sel = None
        if i > MOBA_TOPK:
            g = gate_t[0:SUBLANES, rows]
            rank = jnp.zeros((SUBLANES, blk), F32)
            for m_blk in range(i):
                gm = g[m_blk:m_blk + 1, :]
                beats = (gm > g) | ((gm == g) & (m_blk < blk_row))
                rank = rank + jnp.where(beats, 1.0, 0.0)
            sel_t = jnp.where((rank < float(MOBA_TOPK)) & (blk_row < i), 1.0, 0.0)
            sel_t = jnp.concatenate([sel_t, pad_rows], axis=0).astype(BF16)
            sel = _dot(eye_q, sel_t, NT)
        scores = []
        for n in range(i + 1):
            sn = _dot(qi, kb[n * blk:(n + 1) * blk], NT) * scale
            if n == i:
                sn = jnp.where(causal, sn, NEG_INF)
            elif sel is not None:
                selected = jnp.broadcast_to(sel[:, n:n + 1], sn.shape) > 0.5
                sn = jnp.where(selected, sn, NEG_INF)
            scores.append(sn)
        m_acc = None
        for sn in scores:
            folded = jnp.maximum(sn[:, :HEAD_DIM], sn[:, HEAD_DIM:])
            m_acc = folded if m_acc is None else jnp.maximum(m_acc, folded)
        m = m_acc.max(axis=-1, keepdims=True)
        l_acc = jnp.zeros((blk, HEAD_DIM), F32)
        acc = jnp.zeros((blk, HEAD_DIM), F32)
        for n, sn in enumerate(scores):
            p = jnp.exp(sn - m)
            l_acc = l_acc + (p[:, :HEAD_DIM] + p[:, HEAD_DIM:])
            acc = acc + _dot(p.astype(BF16), vb[n * blk:(n + 1) * blk])
        l = l_acc.sum(axis=-1, keepdims=True)
        o_ref[rows, :] = (acc / l).astype(o_ref.dtype)


def _moba(hcat, cos_full, sin_signed, batch, cast_jobs=()):
    t = hcat.shape[0]
    s = t // batch
    grid = (batch, A_HEADS)
    spec = lambda off: pl.BlockSpec((s, HEAD_DIM), lambda b, h: (b, off + h))
    tab = pl.BlockSpec((s, HEAD_DIM), lambda b, h: (0, 0), pipeline_mode=pl.Buffered(1))
    cast_in, cast_out, cast_shapes = _cast_job_specs(cast_jobs, grid)
    outs = pl.pallas_call(
        functools.partial(_moba_kernel, n_cast=len(cast_jobs)),
        grid=grid,
        in_specs=[spec(0), spec(A_HEADS), spec(2 * A_HEADS), tab, tab] + cast_in,
        out_specs=[pl.BlockSpec((s, HEAD_DIM), lambda b, h: (b, h))] + cast_out,
        out_shape=[jax.ShapeDtypeStruct((t, A_WIDTH), BF16)] + cast_shapes,
        compiler_params=_params("parallel", "parallel"),
        name="moba",
    )(hcat, hcat, hcat, cos_full, sin_signed, *[job[0] for job in cast_jobs])
    return outs[0], outs[1:]


def _gmlp_kernel(u_ref, v_ref, lng_ref, lnb_ref, ws_ref, bs_ref, o_ref):
    rows = u_ref.shape[0]
    tpos = lax.broadcasted_iota(jnp.int32, (GMLP_CHUNK, GMLP_CHUNK), 0)
    spos = lax.broadcasted_iota(jnp.int32, (GMLP_CHUNK, GMLP_CHUNK), 1)
    causal = spos <= tpos
    for g in range(G_GROUPS):
        cols = slice(g * G_DIM, (g + 1) * G_DIM)
        w = jnp.where(causal, ws_ref[g], 0.0).astype(BF16)
        bias = bs_ref[:, g:g + 1]
        ln_g = lng_ref[:, cols]
        ln_b = lnb_ref[:, cols]
        for c in range(rows // GMLP_CHUNK):
            r = slice(c * GMLP_CHUNK, (c + 1) * GMLP_CHUNK)
            vn = _layer_norm(jax.nn.gelu(v_ref[r, cols]), ln_g, ln_b)
            mixed = _dot(w, vn.astype(BF16)) + bias
            o_ref[r, cols] = (jax.nn.gelu(u_ref[r, cols]) * mixed).astype(o_ref.dtype)


def _gmlp(hcat, ln_g, ln_b, w_s, b_s, rows=1024):
    t = hcat.shape[0]
    u_blk = 3 * A_WIDTH // G_WIDTH
    return pl.pallas_call(
        _gmlp_kernel,
        grid=(t // rows,),
        in_specs=[
            pl.BlockSpec((rows, G_WIDTH), lambda i: (i, u_blk)),
            pl.BlockSpec((rows, G_WIDTH), lambda i: (i, u_blk + 1)),
            pl.BlockSpec((1, G_WIDTH), lambda i: (0, 0)),
            pl.BlockSpec((1, G_WIDTH), lambda i: (0, 0)),
            pl.BlockSpec((G_GROUPS, GMLP_CHUNK, GMLP_CHUNK), lambda i: (0, 0, 0)),
            pl.BlockSpec((GMLP_CHUNK, G_GROUPS), lambda i: (0, 0)),
        ],
        out_specs=pl.BlockSpec((rows, G_WIDTH), lambda i: (i, 0)),
        out_shape=jax.ShapeDtypeStruct((t, G_WIDTH), BF16),
        compiler_params=_params("parallel"),
        name="gmlp",
    )(hcat, hcat, ln_g.reshape(1, G_WIDTH), ln_b.reshape(1, G_WIDTH), w_s, b_s.T)


GATE_BLK = 128


def _softplus(x):
    return jnp.maximum(x, 0.0) + jnp.log1p(jnp.exp(-jnp.abs(x)))


def _dn_gate_kernel(x_ref, w_ref, wt_ref, ng_ref, dtb_ref, ngc_ref, dtbc_ref,
                    bg_ref, gc_ref, gct_ref):
    s = x_ref.shape[0]
    hv = DN_V_HEADS
    x = x_ref[...].astype(BF16)
    ba = _dot(x, w_ref[...])
    bat = _dot(wt_ref[...], x, NT)
    lane = lax.broadcasted_iota(jnp.int32, (s, 2 * hv), 1)
    g = ng_ref[...] * _softplus(ba + dtb_ref[...])
    bg = jnp.where(lane < hv, jax.nn.sigmoid(ba), g)
    bg_ref[...] = bg
    gt = ngc_ref[...] * _softplus(bat + dtbc_ref[...])

    r = lax.broadcasted_iota(jnp.int32, (GATE_BLK, GATE_BLK), 0)
    c = lax.broadcasted_iota(jnp.int32, (GATE_BLK, GATE_BLK), 1)
    same = (r // DN_CHUNK) == (c // DN_CHUNK)
    lower = jnp.where(same & (c <= r), 1.0, 0.0).astype(BF16)
    upper = jnp.where(same & (r <= c), 1.0, 0.0).astype(BF16)
    for i in range(s // GATE_BLK):
        sl = slice(i * GATE_BLK, (i + 1) * GATE_BLK)
        gc_ref[sl, :] = _dot_exact_lhs(lower, bg[sl, :])
        gct_ref[:, sl] = _dot_exact_rhs(gt[:, sl], upper)


def _dn_gates(x, w_ba_t, a_log, dt_bias, rows=1024):
    t, d = x.shape
    s = rows
    hv = DN_V_HEADS
    neg_rate = -jnp.exp(a_log.astype(F32))
    zeros = jnp.zeros((hv,), F32)
    ng = jnp.concatenate([zeros, neg_rate]).reshape(1, 2 * hv)
    dtb = jnp.concatenate([zeros, dt_bias.astype(F32)]).reshape(1, 2 * hv)
    small = lambda shape: pl.BlockSpec(shape, lambda b: (0, 0))
    return pl.pallas_call(
        _dn_gate_kernel,
        grid=(t // rows,),
        in_specs=[
            pl.BlockSpec((s, d), lambda b: (b, 0)),
            small((d, 2 * hv)),
            small((2 * hv, d)),
            small((1, 2 * hv)),
            small((1, 2 * hv)),
            small((2 * hv, 1)),
            small((2 * hv, 1)),
        ],
        out_specs=[
            pl.BlockSpec((s, 2 * hv), lambda b: (b, 0)),
            pl.BlockSpec((s, 2 * hv), lambda b: (b, 0)),
            pl.BlockSpec((2 * hv, s), lambda b: (0, b)),
        ],
        out_shape=[
            jax.ShapeDtypeStruct((t, 2 * hv), F32),
            jax.ShapeDtypeStruct((t, 2 * hv), F32),
            jax.ShapeDtypeStruct((2 * hv, t), F32),
        ],
        compiler_params=_params("parallel"),
        name="dn_gates",
    )(x, w_ba_t.T, w_ba_t, ng, dtb, ng.reshape(2 * hv, 1), dtb.reshape(2 * hv, 1))


DN_TB = 128
DN_HB = 32
CONV_PAD = 8


def _conv_stage(x_ref, buf_ref, first):
    tb = x_ref.shape[0]

    @pl.when(first)
    def _():
        buf_ref[0:CONV_PAD, :] = jnp.zeros((CONV_PAD, buf_ref.shape[1]), F32)

    buf_ref[CONV_PAD:CONV_PAD + tb, :] = x_ref[...]


def _conv_silu(buf_ref, w_ref, cols, tb):
    y = buf_ref[CONV_PAD:CONV_PAD + tb, cols] * w_ref[DN_CONV - 1:DN_CONV, cols]
    for j in range(1, DN_CONV):
        tap = DN_CONV - 1 - j
        y = y + buf_ref[CONV_PAD - j:CONV_PAD - j + tb, cols] * w_ref[tap:tap + 1, cols]
    return _silu(y)


def _conv_keep_tail(buf_ref, tb):
    buf_ref[0:CONV_PAD, :] = buf_ref[tb:tb + CONV_PAD, :]


def _hi_lo(a):
    hi = a.astype(BF16).astype(F32)
    return hi, a - hi


def _pair_lhs(x):
    hi, lo = _hi_lo(x)
    return jnp.concatenate([hi, lo], axis=1).astype(BF16)


def _pair_blockdiag(y, first_half):
    top = jnp.where(first_half, y, 0.0)
    return jnp.concatenate([top, y - top], axis=0)


def _pair_product(x, y, first_half):
    return _dot(x.astype(BF16), _pair_blockdiag(y, first_half).astype(BF16))


def _pair_product_hp(x, y, first_half):
    width = x.shape[1]
    y_hi, y_lo = _hi_lo(y)
    bd = jnp.concatenate([_pair_blockdiag(y_hi, first_half),
                          _pair_blockdiag(y_lo, first_half)], axis=1).astype(BF16)
    out = _dot(_pair_lhs(x), jnp.concatenate([bd, bd], axis=0))
    return out[:, :width] + out[:, width:]


def _delta_kernel(*refs):
    for group in range(DN_V_HEADS // DN_HB):
        pl.when(pl.program_id(1) == group)(functools.partial(_delta_group, group, *refs))


def _delta_group(group, q_ref, k_ref, v_ref, z_ref, wq_ref, wk_ref, wv_ref, bg_ref, gc_ref,
                 gct_ref, ng_ref, o_ref, state_ref, qbuf_ref, kbuf_ref, vbuf_ref):
    tb = q_ref.shape[0]
    hv = DN_V_HEADS
    c = DN_CHUNK
    dk = DN_HEAD_DIM
    rep = DN_V_HEADS // DN_QK_HEADS
    first = pl.program_id(2) == 0

    @pl.when(first)
    def _():
        state_ref[...] = jnp.zeros_like(state_ref)

    _conv_stage(q_ref, qbuf_ref, first)
    _conv_stage(k_ref, kbuf_ref, first)
    _conv_stage(v_ref, vbuf_ref, first)

    assert tb == 2 * c
    ri = lax.broadcasted_iota(jnp.int32, (c, 2 * c), 0)
    ci2 = lax.broadcasted_iota(jnp.int32, (c, 2 * c), 1)
    first_half = ci2 < c
    ci = jnp.where(first_half, ci2, ci2 - c)
    incl = ci <= ri
    strict = ci < ri
    eye = jnp.where(ci == ri, 1.0, 0.0)
    norm_g = ng_ref[...]
    n_qk = DN_HB // rep
    heads = range(DN_HB)


    q_n, k_n = [], []
    for jq in range(n_qk):
        qcols = slice(jq * dk, (jq + 1) * dk)
        q_h = _conv_silu(qbuf_ref, wq_ref, qcols, tb)
        k_h = _conv_silu(kbuf_ref, wk_ref, qcols, tb)
        q_n.append(q_h * lax.rsqrt(jnp.sum(q_h * q_h, axis=-1, keepdims=True) + RMS_EPS)
                   * (dk ** -0.5))
        k_n.append(k_h * lax.rsqrt(jnp.sum(k_h * k_h, axis=-1, keepdims=True) + RMS_EPS))

    qk_pair, kk_pair = [], []
    for jq in range(n_qk):
        q0, q1 = q_n[jq][:c], q_n[jq][c:]
        k0, k1 = k_n[jq][:c], k_n[jq][c:]
        zero = jnp.zeros_like(k0)
        lhs = jnp.concatenate([jnp.concatenate([q0, q1], axis=1),
                               jnp.concatenate([k0, k1], axis=1)], axis=0).astype(BF16)
        rhs = jnp.concatenate([jnp.concatenate([k0, zero], axis=1),
                               jnp.concatenate([zero, k1], axis=1)], axis=0).astype(BF16)
        g = _dot(lhs, rhs, NT)
        qk_pair.append(g[:c])
        kk_pair.append(g[c:])

    a_pair, attn_bf, kbg_bf, q_dec, k_dec, g_last, betas = [], [], [], [], [], [], []
    for hl in heads:
        jq = hl // rep
        head = group * DN_HB + hl
        beta_col = bg_ref[:, head:head + 1]
        betas.append(beta_col)
        gc_col = gc_ref[:, hv + head:hv + head + 1]
        gc_row = gct_ref[head:head + 1, :]
        beta_sel = jnp.where(first_half, beta_col[:c], beta_col[c:])
        gc_sel = jnp.where(first_half, gc_col[:c], gc_col[c:])
        decay = jnp.exp(jnp.where(incl, gc_sel - gc_row, -jnp.inf))
        a_pair.append(jnp.where(strict, kk_pair[jq] * beta_sel * decay, 0.0))
        attn_bf.append((qk_pair[jq] * decay).astype(BF16))
        egc = jnp.exp(gc_col)
        kbg_bf.append((k_n[jq] * beta_col * egc).astype(BF16))
        q_dec.append((q_n[jq] * egc).astype(BF16))
        gl = [gc_col[c - 1:c, :], gc_col[tb - 1:tb, :]]
        gl_rows = jnp.concatenate([jnp.broadcast_to(gl[0], (c, 1)),
                                   jnp.broadcast_to(gl[1], (c, 1))], axis=0)
        k_dec.append((k_n[jq] * jnp.exp(gl_rows - gc_col)).astype(BF16))
        g_last.append([jnp.exp(gl[0]), jnp.exp(gl[1])])

    vb_bf, z_gate = [], []

    def value_conv(hl):
        vcols = slice(hl * dk, (hl + 1) * dk)
        vb_bf.append((_conv_silu(vbuf_ref, wv_ref, vcols, tb) * betas[hl]).astype(BF16))

    def output_gate(hl):
        z_gate.append(norm_g * _silu(z_ref[:, hl * dk:(hl + 1) * dk]))

    independent = ([functools.partial(value_conv, hl) for hl in heads]
                   + [functools.partial(output_gate, hl) for hl in heads])
    n_slots = 2 * (int(math.log2(c)) - 1) + 2
    per_slot = -(-len(independent) // n_slots)

    def fill():
        for work in independent[:per_slot]:
            work()
        del independent[:per_slot]

    inv = [eye - jnp.where((ri // 2) == (ci // 2), a, 0.0) for a in a_pair]
    s = 2
    while s < c:
        off = ((ri // (2 * s)) == (ci // (2 * s))) & ((ri // s) != (ci // s))
        m1 = [_pair_product(jnp.where(off, a_pair[hl], 0.0), inv[hl], first_half)
              for hl in heads]
        fill()
        m2 = [_pair_product(inv[hl], m1[hl], first_half) for hl in heads]
        fill()
        inv = [inv[hl] - m2[hl] for hl in heads]
        s *= 2
    ax = [_pair_product_hp(a_pair[hl], inv[hl], first_half) for hl in heads]
    fill()
    resid = [eye - inv[hl] - ax[hl] for hl in heads]
    corr = [_pair_product(inv[hl], resid[hl], first_half) for hl in heads]
    fill()
    inv = [inv[hl] + corr[hl] for hl in heads]
    assert not independent

    u, w_bf = [], []
    zero_rows = jnp.zeros((c, 2 * dk), BF16)
    for hl in heads:
        lhs = _pair_lhs(inv[hl])
        rhs = jnp.concatenate([vb_bf[hl], kbg_bf[hl]], axis=1)
        r0, r1 = rhs[:c], rhs[c:]
        uw0 = _dot(lhs, jnp.concatenate([r0, zero_rows, r0, zero_rows], axis=0))
        uw1 = _dot(lhs, jnp.concatenate([zero_rows, r1, zero_rows, r1], axis=0))
        u.append([uw0[:, :dk], uw1[:, :dk]])
        w_bf.append([uw0[:, dk:].astype(BF16), uw1[:, dk:].astype(BF16)])

    states = [state_ref[hl] for hl in heads]
    zero_v = jnp.zeros((c, dk), BF16)
    for ch in range(2):
        rows = slice(ch * c, (ch + 1) * c)
        ws_qs = [_dot(jnp.concatenate([w_bf[hl][ch], q_dec[hl][rows]], axis=0),
                      states[hl].astype(BF16)) for hl in heads]
        v_new_bf = [(u[hl][ch] - ws_qs[hl][:c]).astype(BF16) for hl in heads]
        o_intra = []
        for hl in heads:
            v_rows = [v_new_bf[hl], zero_v] if ch == 0 else [zero_v, v_new_bf[hl]]
            o_intra.append(_dot(attn_bf[hl], jnp.concatenate(v_rows, axis=0)))
        states = [states[hl] * g_last[hl][ch] + _dot(k_dec[hl][rows], v_new_bf[hl], TN)
                  for hl in heads]
        for hl in heads:
            vcols = slice(hl * dk, (hl + 1) * dk)
            o_c = ws_qs[hl][c:] + o_intra[hl]
            o_n = (o_c * lax.rsqrt(jnp.mean(o_c * o_c, axis=-1, keepdims=True) + RMS_EPS)
                   * z_gate[hl][rows])
            o_ref[rows, vcols] = o_n.astype(o_ref.dtype)
    for hl in heads:
        state_ref[hl] = states[hl]
    _conv_keep_tail(qbuf_ref, tb)
    _conv_keep_tail(kbuf_ref, tb)
    _conv_keep_tail(vbuf_ref, tb)


def _delta(hcat, conv_w, bg, gcs, gct, norm_g, batch):
    t = hcat.shape[0]
    s = t // batch
    nt = s // DN_TB
    rep = DN_V_HEADS // DN_QK_HEADS
    wq = DN_HB // rep * DN_HEAD_DIM
    wv = DN_HB * DN_HEAD_DIM
    k_off = DN_QK_WIDTH // wq
    v_off = 2 * DN_QK_WIDTH // wv
    z_off = DN_CONV_DIM // wv
    hv2 = 2 * DN_V_HEADS
    row = lambda b, g, i: b * nt + i
    return pl.pallas_call(
        _delta_kernel,
        grid=(batch, DN_V_HEADS // DN_HB, nt),
        in_specs=[
            pl.BlockSpec((DN_TB, wq), lambda b, g, i: (row(b, g, i), g)),
            pl.BlockSpec((DN_TB, wq), lambda b, g, i: (row(b, g, i), k_off + g)),
            pl.BlockSpec((DN_TB, wv), lambda b, g, i: (row(b, g, i), v_off + g)),
            pl.BlockSpec((DN_TB, wv), lambda b, g, i: (row(b, g, i), z_off + g)),
            pl.BlockSpec((DN_CONV, wq), lambda b, g, i: (0, g)),
            pl.BlockSpec((DN_CONV, wq), lambda b, g, i: (0, k_off + g)),
            pl.BlockSpec((DN_CONV, wv), lambda b, g, i: (0, v_off + g)),
            pl.BlockSpec((DN_TB, hv2), lambda b, g, i: (row(b, g, i), 0)),
            pl.BlockSpec((DN_TB, hv2), lambda b, g, i: (row(b, g, i), 0)),
            pl.BlockSpec((DN_V_HEADS, DN_TB), lambda b, g, i: (1, row(b, g, i))),
            pl.BlockSpec((1, DN_HEAD_DIM), lambda b, g, i: (0, 0)),
        ],
        out_specs=pl.BlockSpec((DN_TB, wv), lambda b, g, i: (row(b, g, i), g)),
        out_shape=jax.ShapeDtypeStruct((t, DN_V_WIDTH), BF16),
        scratch_shapes=[
            pltpu.VMEM((DN_HB, DN_HEAD_DIM, DN_HEAD_DIM), F32),
            pltpu.VMEM((CONV_PAD + DN_TB, wq), F32),
            pltpu.VMEM((CONV_PAD + DN_TB, wq), F32),
            pltpu.VMEM((CONV_PAD + DN_TB, wv), F32),
        ],
        compiler_params=_params("parallel", "parallel", "arbitrary"),
        name="delta_rule",
    )(hcat, hcat, hcat, hcat, conv_w, conv_w, conv_w, bg, gcs, gct,
      norm_g.reshape(1, DN_HEAD_DIM))


def _rope_tables(s):
    half = HEAD_DIM // 2
    inv = jnp.exp(-math.log(ROPE_THETA) * jnp.arange(half, dtype=F32) * (2.0 / HEAD_DIM))
    ang = jnp.arange(s, dtype=F32)[:, None] * inv[None, :]
    cos, sin = jnp.cos(ang), jnp.sin(ang)
    return jnp.concatenate([cos, cos], axis=-1), jnp.concatenate([-sin, sin], axis=-1)


def kernel(x, ffn1_w_gate, ffn1_w_up, ffn1_w_down, ffn2_w_gate, ffn2_w_up, ffn2_w_down, ln_g, ln_b, ab_w_in, ab_gmlp_ln_g, ab_gmlp_ln_b, ab_gmlp_w_s, ab_gmlp_b_s, ab_w_out, dn_w_in, dn_conv_w, dn_a_log, dn_dt_bias, dn_norm_g, dn_w_out):
    batch, s, d = x.shape
    t = batch * s
    bf = lambda w: w.astype(BF16)
    xf = x.reshape(t, d)
    cos_full, sin_signed = _rope_tables(s)
    ab_in, ab_out, dn_out = bf(ab_w_in), bf(ab_w_out), bf(dn_w_out)
    dn_in_t = bf(jnp.swapaxes(dn_w_in, 1, 2))

    ffn_f32 = {1: (ffn1_w_gate, ffn1_w_up, ffn1_w_down), 2: (ffn2_w_gate, ffn2_w_up, ffn2_w_down)}
    ffn_bf16 = {}
    pending = [(half, layer) for layer in range(DEPTH) for half in (1, 2)]

    def cast_now(key):
        half, layer = key
        ffn_bf16[key] = tuple(bf(w[layer]) for w in ffn_f32[half])

    def take_cast_jobs():
        keys = list(pending)
        del pending[:]
        return keys, tuple((w, layer) for half, layer in keys for w in ffn_f32[half])

    def finish_cast_jobs(keys, results):
        for n, key in enumerate(keys):
            ffn_bf16[key] = tuple(results[3 * n:3 * n + 3])

    def half_step(xf, half, layer, slot):
        key = (half, layer)
        if key not in ffn_bf16:
            pending.remove(key)
            cast_now(key)
        return _ffn_ln(xf, *ffn_bf16[key], ln_g[layer, slot], ln_b[layer, slot])

    cast_now(pending.pop(0))
    for i in range(DEPTH):
        xf = half_step(xf, 1, i, 0)
        j = i // 2
        if i % 2 == 0:
            hcat = _matmul(xf, ab_in[j:j + 1], AB_IN, tn=AB_TN)
            keys, jobs = take_cast_jobs()
            a_out, done = _moba(hcat, cos_full, sin_signed, batch, cast_jobs=jobs)
            finish_cast_jobs(keys, done)
            g_out = _gmlp(hcat, ab_gmlp_ln_g[j], ab_gmlp_ln_b[j], ab_gmlp_w_s[j], ab_gmlp_b_s[j])
            mix, w_out = [a_out, g_out], ab_out[j:j + 1]
        else:
            hcat = _matmul(xf, dn_in_t[j:j + 1], DN_MAIN, w_is_transposed=True, tn=DN_TN)
            bg, gcs, gct = _dn_gates(xf, dn_in_t[j, DN_MAIN:], dn_a_log[j], dn_dt_bias[j])
            mix = [_delta(hcat, dn_conv_w[j], bg, gcs, gct, dn_norm_g[j], batch)]
            w_out = dn_out[j:j + 1]
        tm = 1024 if w_out.shape[1] <= D_MODEL else 512
        xf = _proj_ln(mix, w_out, xf, ln_g[i, 1], ln_b[i, 1], tm=tm)
        xf = half_step(xf, 2, i, 2)
    return xf.reshape(batch, s, d)
```
